```python
import math
import jax, jax.numpy as jnp
from jax import lax
import numpy as np

D_MODEL = 1024
BATCH = 2
SEQ = 16384
DEPTH = 2

GRID_W = 64
CTX_LEN = 256
HEAD_DIM = 64
EPS = 1e-6
NEG_INF = -1e30

A_HEADS = D_MODEL // 256
A_WIDTH = A_HEADS * HEAD_DIM
CHUNK = 128
B_HEADS = (3 * D_MODEL) // 512
B_WIDTH = B_HEADS * HEAD_DIM
NA_ROWS = 8
NA_COLS = 16
C_HEADS = (3 * D_MODEL) // 512
C_NOPE = 64
C_ROPE = 32
C_QK = C_NOPE + C_ROPE
C_VDIM = 64
C_Q_LORA = (3 * D_MODEL) // 8
C_KV_LORA = D_MODEL // 4
C_WIDTH = C_HEADS * C_VDIM
Q_BLOCK = 128
ROPE_BASE = 10000.0

MIX_WIDTH = A_WIDTH + B_WIDTH + C_WIDTH
IN_A = 2 * A_WIDTH
IN_B = 3 * B_WIDTH
IN_C = C_Q_LORA + C_KV_LORA + C_ROPE
IN_WIDTH = IN_A + IN_B + IN_C

N_GROUPS = 8
EXPERTS_PER_GROUP = 8
N_EXPERTS = N_GROUPS * EXPERTS_PER_GROUP
TOP_K = 2
D_EXPERT = D_MODEL // 2
MOE_BLOCK = 128

kernel_name = 'hybrid_dit_gmlp_natten_mla_hmoe'


def rmsnorm(x, g):
    xf = x.astype(jnp.float32)
    y = xf * lax.rsqrt(jnp.mean(xf * xf, axis=-1, keepdims=True) + EPS)
    return (y * g.astype(jnp.float32)).astype(x.dtype)


def modulate(h, shift, scale):
    return h * (1 + scale) + shift


def rope_table(pos):
    half = C_ROPE // 2
    inv = ROPE_BASE ** (-jnp.arange(0, half, 2, dtype=jnp.float32) / half)
    ang = pos.astype(jnp.float32)[:, None] * inv[None, :]
    return jnp.cos(ang), jnp.sin(ang)


def rope_1d(x, cos, sin):
    x1, x2 = jnp.split(x, 2, axis=-1)
    return jnp.concatenate([x1 * cos - x2 * sin, x2 * cos + x1 * sin], axis=-1)


def axial_rope(x, cos_r, sin_r, cos_c, sin_c):
    half = C_ROPE // 2
    return jnp.concatenate([rope_1d(x[..., :half], cos_r, sin_r), rope_1d(x[..., half:], cos_c, sin_c)], axis=-1)


def chunk_gmlp(z, v_norm, w_s, b_s):
    bn, t, _ = z.shape
    u, v = jnp.split(jax.nn.gelu(z), 2, axis=-1)
    v = rmsnorm(v, v_norm).reshape(bn, t // CHUNK, CHUNK, A_HEADS, HEAD_DIM)
    v = jnp.einsum('hpq,bnqhd->bnphd', w_s, v) + b_s.T[:, :, None]
    return u * v.reshape(bn, t, A_WIDTH)


def heads_qkv(z, q_norm, k_norm):
    bn, t, _ = z.shape
    q, k, v = jnp.split(z, 3, axis=-1)
    q = rmsnorm(q.reshape(bn, t, B_HEADS, HEAD_DIM), q_norm)
    k = rmsnorm(k.reshape(bn, t, B_HEADS, HEAD_DIM), k_norm)
    return q, k, v.reshape(bn, t, B_HEADS, HEAD_DIM)


def dense_attend(q, k, v):
    s = jnp.einsum('bqhd,bkhd->bhqk', q, k).astype(jnp.float32) * (HEAD_DIM ** -0.5)
    p = jax.nn.softmax(s, axis=-1).astype(v.dtype)
    o = jnp.einsum('bhqk,bkhd->bqhd', p, v)
    return o.reshape(o.shape[0], o.shape[1], -1)


def neighbourhood_attention(q, k, v, k_ctx, v_ctx, rpb):
    bn, l, h, dh = q.shape
    rows = l // GRID_W
    kh = min(NA_ROWS, rows)
    r = jnp.arange(rows)
    r0 = jnp.clip(r - kh // 2, 0, rows - kh)
    key_rows = r0[:, None] + jnp.arange(kh)[None, :]
    cq = jnp.arange(GRID_W)
    c0 = jnp.clip(cq - NA_COLS // 2, 0, GRID_W - NA_COLS)
    in_win = (cq[None, :] >= c0[:, None]) & (cq[None, :] < c0[:, None] + NA_COLS)
    dc = jnp.clip(cq[None, :] - cq[:, None], -(NA_COLS - 1), NA_COLS - 1) + NA_COLS - 1
    dr = key_rows - r[:, None] + NA_ROWS - 1
    bias = jnp.take(rpb[:, dr], dc, axis=-1)
    bias = bias.transpose(0, 1, 3, 2, 4).astype(jnp.float32)
    qg = q.reshape(bn, rows, GRID_W, h, dh)
    kg = k.reshape(bn, rows, GRID_W, h, dh)[:, key_rows]
    vg = v.reshape(bn, rows, GRID_W, h, dh)[:, key_rows]
    scale = dh ** -0.5
    s_loc = jnp.einsum('brqhd,brkwhd->bhrqkw', qg, kg).astype(jnp.float32) * scale + bias[None]
    s_loc = jnp.where(in_win[:, None, :], s_loc, NEG_INF)
    s_ctx = jnp.einsum('brqhd,bchd->bhrqc', qg, k_ctx).astype(jnp.float32) * scale
    s = jnp.concatenate([s_loc.reshape(bn, h, rows, GRID_W, kh * GRID_W), s_ctx], axis=-1)
    p = jax.nn.softmax(s, axis=-1).astype(v.dtype)
    p_loc = p[..., :kh * GRID_W].reshape(bn, h, rows, GRID_W, kh, GRID_W)
    p_ctx = p[..., kh * GRID_W:]
    o = jnp.einsum('bhrqkw,brkwhd->brqhd', p_loc, vg) + jnp.einsum('bhrqc,bchd->brqhd', p_ctx, v_ctx)
    return o.reshape(bn, l, h * dh)


def mla_project(z, q_a_norm, w_q_up, kv_a_norm, w_kv_up, q_norm, k_norm):
    bn, t, _ = z.shape
    q_lat, kv_lat, k_rope = jnp.split(z, [C_Q_LORA, C_Q_LORA + C_KV_LORA], axis=-1)
    q = (rmsnorm(q_lat, q_a_norm) @ w_q_up).reshape(bn, t, C_HEADS, C_QK)
    kv = (rmsnorm(kv_lat, kv_a_norm) @ w_kv_up).reshape(bn, t, C_HEADS, C_NOPE + C_VDIM)
    k_nope, v = jnp.split(kv, [C_NOPE], axis=-1)
    q_nope = rmsnorm(q[..., :C_NOPE], q_norm[:C_NOPE])
    q_rope = rmsnorm(q[..., C_NOPE:], q_norm[C_NOPE:])
    k_nope = rmsnorm(k_nope, k_norm[:C_NOPE])
    k_rope = rmsnorm(k_rope, k_norm[C_NOPE:])
    return q_nope, q_rope, k_nope, k_rope, v


def mla_attend(q_nope, q_rope, k_nope, k_rope, v):
    s = jnp.einsum('bqhd,bkhd->bhqk', q_nope, k_nope) + jnp.einsum('bqhd,bkd->bhqk', q_rope, k_rope)
    p = jax.nn.softmax(s.astype(jnp.float32) * (C_QK ** -0.5), axis=-1).astype(v.dtype)
    o = jnp.einsum('bhqk,bkhd->bqhd', p, v)
    return o.reshape(o.shape[0], o.shape[1], -1)


def mla_latent_attention(q_nope, q_rope, k_nope, k_rope, v):
    bn, l, h, _ = q_nope.shape
    nb = l // Q_BLOCK
    qn = q_nope.reshape(bn, nb, Q_BLOCK, h, C_NOPE).transpose(1, 0, 2, 3, 4)
    qr = q_rope.reshape(bn, nb, Q_BLOCK, h, C_ROPE).transpose(1, 0, 2, 3, 4)
    o = lax.map(lambda a: mla_attend(a[0], a[1], k_nope, k_rope, v), (qn, qr))
    return o.transpose(1, 0, 2, 3).reshape(bn, l, h * C_VDIM)


def hier_moe(t, w_group, w_router, w1, w3, w2):
    n, d = t.shape
    tok = jnp.arange(n)
    g_logits = (t @ w_group).astype(jnp.float32)
    g_idx = jnp.argmax(g_logits, axis=-1)
    g_gate = jax.nn.softmax(g_logits, axis=-1)[tok, g_idx]
    e_logits = (t @ w_router).astype(jnp.float32).reshape(n, N_GROUPS, EXPERTS_PER_GROUP)[tok, g_idx]
    top_l, top_j = lax.top_k(e_logits, TOP_K)
    weights = g_gate[:, None] * jax.nn.softmax(top_l, axis=-1)
    expert = g_idx[:, None] * EXPERTS_PER_GROUP + top_j
    a = n * TOP_K
    flat_e = expert.reshape(a)
    flat_tok = jnp.arange(a) // TOP_K
    flat_w = weights.reshape(a)
    order = jnp.argsort(flat_e)
    se, stok, sw = flat_e[order], flat_tok[order], flat_w[order]
    counts = jnp.bincount(flat_e, length=N_EXPERTS)
    starts = jnp.cumsum(counts) - counts
    pcounts = (counts + MOE_BLOCK - 1) // MOE_BLOCK * MOE_BLOCK
    pends = jnp.cumsum(pcounts)
    pstarts = pends - pcounts
    dest = pstarts[se] + (jnp.arange(a) - starts[se])
    n_blocks = -(-a // MOE_BLOCK) + N_EXPERTS
    p_slots = n_blocks * MOE_BLOCK
    slot_tok = jnp.zeros((p_slots,), jnp.int32).at[dest].set(stok.astype(jnp.int32))
    slot_w = jnp.zeros((p_slots,), t.dtype).at[dest].set(sw.astype(t.dtype))
    block_e = jnp.minimum(jnp.searchsorted(pends, jnp.arange(n_blocks) * MOE_BLOCK, side='right'), N_EXPERTS - 1)
    xs = t[slot_tok].reshape(n_blocks, MOE_BLOCK, d)

    def expert_block(args):
        xb, e = args
        hb = jax.nn.silu(xb @ w1[e]) * (xb @ w3[e])
        return hb @ w2[e]

    ys = lax.map(expert_block, (xs, block_e)).reshape(p_slots, d)
    return jnp.zeros_like(t).at[slot_tok].add(ys * slot_w[:, None])


def setup_inputs(seed: int = 0) -> dict:
    key = jax.random.key(seed)
    ks = jax.random.split(key, 27)
    f32 = jnp.float32

    def nrm(i, shape, s):
        return jax.random.normal(ks[i], shape, f32) * s

    def gain(i, shape):
        return 1.0 + 0.01 * jax.random.normal(ks[i], shape, f32)

    L, D = DEPTH, D_MODEL
    return {
        'x': nrm(0, (BATCH, SEQ, D), 1.0),
        'c': nrm(1, (BATCH, D), 1.0),
        'ctx': nrm(2, (BATCH, CTX_LEN, D), 1.0),
        'c_ctx': nrm(3, (D,), 1.0),
        'w_ada': nrm(4, (L, D, 6 * D), 0.5 * D ** -0.5),
        'b_ada': nrm(5, (L, 6 * D), 0.02),
        'norm_mix': gain(6, (L, D)),
        'w_in': nrm(7, (L, D, IN_WIDTH), D ** -0.5),
        'a_v_norm': gain(8, (L, A_WIDTH)),
        'a_w_s': nrm(9, (L, A_HEADS, CHUNK, CHUNK), CHUNK ** -0.5),
        'a_b_s': 1.0 + nrm(10, (L, A_HEADS, CHUNK), 0.02),
        'b_q_norm': gain(11, (L, HEAD_DIM)),
        'b_k_norm': gain(12, (L, HEAD_DIM)),
        'b_rpb': nrm(13, (L, B_HEADS, 2 * NA_ROWS - 1, 2 * NA_COLS - 1), 0.1),
        'c_q_a_norm': gain(14, (L, C_Q_LORA)),
        'c_w_q_up': nrm(15, (L, C_Q_LORA, C_HEADS * C_QK), C_Q_LORA ** -0.5),
        'c_kv_a_norm': gain(16, (L, C_KV_LORA)),
        'c_w_kv_up': nrm(17, (L, C_KV_LORA, C_HEADS * (C_NOPE + C_VDIM)), C_KV_LORA ** -0.5),
        'c_q_norm': gain(18, (L, C_QK)),
        'c_k_norm': gain(19, (L, C_QK)),
        'w_out': nrm(20, (L, MIX_WIDTH, D), MIX_WIDTH ** -0.5),
        'norm_ffn': gain(21, (L, D)),
        'moe_w_group': nrm(22, (L, D, N_GROUPS), D ** -0.5),
        'moe_w_router': nrm(23, (L, D, N_EXPERTS), D ** -0.5),
        'moe_w1': nrm(24, (L, N_EXPERTS, D, D_EXPERT), D ** -0.5),
        'moe_w3': nrm(25, (L, N_EXPERTS, D, D_EXPERT), D ** -0.5),
        'moe_w2': nrm(26, (L, N_EXPERTS, D_EXPERT, D), D_EXPERT ** -0.5),
    }


def reference(x, c, ctx, c_ctx, w_ada, b_ada, norm_mix, w_in, a_v_norm, a_w_s, a_b_s, b_q_norm, b_k_norm, b_rpb,
              c_q_a_norm, c_w_q_up, c_kv_a_norm, c_w_kv_up, c_q_norm, c_k_norm, w_out, norm_ffn,
              moe_w_group, moe_w_router, moe_w1, moe_w3, moe_w2):
    bn, l, d = x.shape
    nc = ctx.shape[1]
    pos = jnp.arange(l)
    cos_r, sin_r = rope_table(pos // GRID_W)
    cos_c, sin_c = rope_table(pos % GRID_W)
    cos_r, sin_r, cos_c, sin_c = (a.astype(x.dtype) for a in (cos_r, sin_r, cos_c, sin_c))
    silu_c = jax.nn.silu(c)
    silu_cc = jax.nn.silu(c_ctx)
    xc = ctx
    for i in range(DEPTH):
        need_ctx = i < DEPTH - 1
        mod = silu_c @ w_ada[i] + b_ada[i]
        mod_c = silu_cc @ w_ada[i] + b_ada[i]
        sh1, s1, g1, sh2, s2, g2 = [m[:, None, :] for m in jnp.split(mod, 6, axis=-1)]
        sh1c, s1c, g1c, sh2c, s2c, g2c = jnp.split(mod_c, 6, axis=-1)

        h = modulate(rmsnorm(x, norm_mix[i]), sh1, s1)
        hc = modulate(rmsnorm(xc, norm_mix[i]), sh1c, s1c)
        z_a, z_b, z_c = jnp.split(h @ w_in[i], [IN_A, IN_A + IN_B], axis=-1)
        zc_a, zc_b, zc_c = jnp.split(hc @ w_in[i], [IN_A, IN_A + IN_B], axis=-1)
        o_a = chunk_gmlp(z_a, a_v_norm[i], a_w_s[i], a_b_s[i])
        q_b, k_b, v_b = heads_qkv(z_b, b_q_norm[i], b_k_norm[i])
        qc_b, kc_b, vc_b = heads_qkv(zc_b, b_q_norm[i], b_k_norm[i])
        o_b = neighbourhood_attention(q_b, k_b, v_b, kc_b, vc_b, b_rpb[i])
        qn, qr, kn, kr, v_c = mla_project(z_c, c_q_a_norm[i], c_w_q_up[i], c_kv_a_norm[i], c_w_kv_up[i], c_q_norm[i], c_k_norm[i])
        qr = axial_rope(qr, cos_r[:, None], sin_r[:, None], cos_c[:, None], sin_c[:, None])
        kr = axial_rope(kr, cos_r, sin_r, cos_c, sin_c)
        cqn, cqr, ckn, ckr, cv = mla_project(zc_c, c_q_a_norm[i], c_w_q_up[i], c_kv_a_norm[i], c_w_kv_up[i], c_q_norm[i], c_k_norm[i])
        o_c = mla_latent_attention(qn, qr, jnp.concatenate([kn, ckn], axis=1), jnp.concatenate([kr, ckr], axis=1),
                                   jnp.concatenate([v_c, cv], axis=1))
        x = x + g1 * (jnp.concatenate([o_a, o_b, o_c], axis=-1) @ w_out[i])
        if need_ctx:
            oc_a = chunk_gmlp(zc_a, a_v_norm[i], a_w_s[i], a_b_s[i])
            oc_b = dense_attend(qc_b, kc_b, vc_b)
            oc_c = mla_attend(cqn, cqr, ckn, ckr, cv)
            xc = xc + g1c * (jnp.concatenate([oc_a, oc_b, oc_c], axis=-1) @ w_out[i])

        h2 = modulate(rmsnorm(x, norm_ffn[i]), sh2, s2).reshape(bn * l, d)
        if need_ctx:
            h2c = modulate(rmsnorm(xc, norm_ffn[i]), sh2c, s2c).reshape(bn * nc, d)
            y = hier_moe(jnp.concatenate([h2, h2c], axis=0), moe_w_group[i], moe_w_router[i], moe_w1[i], moe_w3[i], moe_w2[i])
            x = x + g2 * y[:bn * l].reshape(bn, l, d)
            xc = xc + g2c * y[bn * l:].reshape(bn, nc, d)
        else:
            y = hier_moe(h2, moe_w_group[i], moe_w_router[i], moe_w1[i], moe_w3[i], moe_w2[i])
            x = x + g2 * y.reshape(bn, l, d)
    return x
```

```python
import functools

import numpy as np
import jax
import jax.numpy as jnp
from jax import lax
from jax.experimental import pallas as pl
from jax.experimental.pallas import tpu as pltpu

F32 = jnp.float32
BF16 = jnp.bfloat16

D_MODEL = 1024
GRID_W = 64
HEAD_DIM = 64
EPS = 1e-6
NEG_INF = -1e30

A_HEADS = 4
A_WIDTH = A_HEADS * HEAD_DIM
CHUNK = 128
B_HEADS = 6
B_WIDTH = B_HEADS * HEAD_DIM
NA_ROWS = 8
NA_COLS = 16
C_HEADS = 6
C_NOPE = 64
C_ROPE = 32
C_QK = C_NOPE + C_ROPE
C_VDIM = 64
C_Q_LORA = 384
C_KV_LORA = 256
C_WIDTH = C_HEADS * C_VDIM
ROPE_BASE = 10000.0
IN_A = 2 * A_WIDTH
IN_B = 3 * B_WIDTH
IN_C = C_Q_LORA + C_KV_LORA + C_ROPE
IN_AUG = IN_A + IN_B + IN_C + C_ROPE

N_GROUPS = 8
EXPERTS_PER_GROUP = 8
N_EXPERTS = N_GROUPS * EXPERTS_PER_GROUP
TOP_K = 2
D_EXPERT = D_MODEL // 2
ROUTER_PAD = 128

V7X_VMEM_LIMIT_BYTES = 56 * 1024 * 1024
LANE = 128
QPAD = 128

NA_QROWS = 8
NA_KBLK_ROWS = 4
MOE_BLOCK = 256

HIGHEST = lax.Precision.HIGHEST


def _cparams(sem):
    return pltpu.CompilerParams(dimension_semantics=sem, vmem_limit_bytes=V7X_VMEM_LIMIT_BYTES)


def _ada_body(c_ref, w_ref, b_ref, o_ref):
    cc = c_ref[...]
    s = cc * (1.0 / (1.0 + jnp.exp(-cc)))
    o_ref[...] = jnp.dot(s, w_ref[...], preferred_element_type=F32, precision=HIGHEST) + b_ref[...]


def _ada(cc8, w, b):
    n_out = w.shape[1]
    tn = 1024
    return pl.pallas_call(
        _ada_body,
        grid=(n_out // tn,),
        in_specs=[pl.BlockSpec((8, D_MODEL), lambda j: (0, 0)),
                  pl.BlockSpec((D_MODEL, tn), lambda j: (0, j)),
                  pl.BlockSpec((1, tn), lambda j: (0, j))],
        out_specs=pl.BlockSpec((8, tn), lambda j: (0, j)),
        out_shape=jax.ShapeDtypeStruct((8, n_out), F32),
        compiler_params=_cparams(("arbitrary",)),
        name="ada_mod",
    )(cc8, w, b)


def _rms_rows(v):
    return lax.rsqrt(jnp.mean(v * v, axis=0, keepdims=True) + EPS)


def _gelu_tanh(x):
    return 0.5 * x * (1.0 + jnp.tanh(0.7978845608028654 * (x + 0.044715 * (x * x * x))))


def _mixin_body(x_ref, sh_ref, sc_ref, gn_ref, wt_ref, avn_ref, wst_ref, bs_ref, gq_ref, gk_ref,
                cqa_ref, wq_ref, ckva_ref, wkv_ref, gcqn_ref, gcqr_ref, gcqrp_ref, gckn_ref, gckr_ref,
                gckrp_ref, cos_ref, sin_ref,
                oa_ref, qb_ref, kb_ref, vb_ref, qc_ref, kc_ref, vc_ref, *, tm):
    x = x_ref[0]
    h = x * lax.rsqrt(jnp.mean(x * x, axis=-1, keepdims=True) + EPS) * gn_ref[...]
    h = h * (1.0 + sc_ref[0]) + sh_ref[0]
    zt = lax.dot_general(wt_ref[...], h.astype(BF16), (((1,), (1,)), ((), ())),
                         preferred_element_type=F32)

    ga = _gelu_tanh(zt[0:IN_A])
    u = ga[0:A_WIDTH]
    v = ga[A_WIDTH:IN_A]
    vn = (v * _rms_rows(v) * avn_ref[...]).astype(BF16)
    for hd in range(A_HEADS):
        r0, r1 = hd * HEAD_DIM, (hd + 1) * HEAD_DIM
        for c in range(tm // CHUNK):
            c0, c1 = c * CHUNK, (c + 1) * CHUNK
            vm = jnp.dot(vn[r0:r1, c0:c1], wst_ref[hd], preferred_element_type=F32) + bs_ref[hd]
            oa_ref[0, r0:r1, c0:c1] = (u[r0:r1, c0:c1] * vm).astype(BF16)

    zb = zt[IN_A:IN_A + IN_B]
    zeros64 = jnp.zeros((HEAD_DIM, tm), F32)
    kn = []
    for hd in range(B_HEADS):
        q = zb[hd * HEAD_DIM:(hd + 1) * HEAD_DIM]
        k = zb[B_WIDTH + hd * HEAD_DIM:B_WIDTH + (hd + 1) * HEAD_DIM]
        qn = q * _rms_rows(q) * gq_ref[...]
        kn.append(k * _rms_rows(k) * gk_ref[...])
        pair = [qn, zeros64] if hd % 2 == 0 else [zeros64, qn]
        qb_ref[0, hd] = jnp.concatenate(pair, axis=0).astype(BF16)
    for p in range(B_HEADS // 2):
        kt = jnp.concatenate([kn[2 * p], kn[2 * p + 1]], axis=0)
        kb_ref[0, :, p * LANE:(p + 1) * LANE] = kt.T.astype(BF16)
    vb_ref[0] = zb[2 * B_WIDTH:3 * B_WIDTH].astype(BF16)

    zc = zt[IN_A + IN_B:IN_AUG]
    ql = zc[0:C_Q_LORA]
    kvl = zc[C_Q_LORA:C_Q_LORA + C_KV_LORA]
    kr = zc[C_Q_LORA + C_KV_LORA:C_Q_LORA + C_KV_LORA + C_ROPE]
    krp = zc[C_Q_LORA + C_KV_LORA + C_ROPE:C_Q_LORA + C_KV_LORA + 2 * C_ROPE]
    qln = (ql * _rms_rows(ql) * cqa_ref[...]).astype(BF16)
    kvln = (kvl * _rms_rows(kvl) * ckva_ref[...]).astype(BF16)
    qt = jnp.dot(wq_ref[...], qln, preferred_element_type=F32)
    kvt = jnp.dot(wkv_ref[...], kvln, preferred_element_type=F32)
    cos = cos_ref[...]
    sin = sin_ref[...]
    krn = _rms_rows(kr) * (gckr_ref[...] * kr * cos + gckrp_ref[...] * krp * sin)
    zeros32 = jnp.zeros((QPAD - C_QK, tm), F32)
    nq = C_HEADS * C_NOPE
    nr = C_HEADS * C_ROPE
    for hd in range(C_HEADS):
        qn = qt[hd * C_NOPE:(hd + 1) * C_NOPE]
        qn = qn * _rms_rows(qn) * gcqn_ref[...]
        qr = qt[nq + hd * C_ROPE:nq + (hd + 1) * C_ROPE]
        qrp = qt[nq + nr + hd * C_ROPE:nq + nr + (hd + 1) * C_ROPE]
        qrn = _rms_rows(qr) * (gcqr_ref[...] * qr * cos + gcqrp_ref[...] * qrp * sin)
        qc_ref[0, hd] = jnp.concatenate([qn, qrn, zeros32], axis=0).astype(BF16)
        kn_c = kvt[hd * C_NOPE:(hd + 1) * C_NOPE]
        kn_c = kn_c * _rms_rows(kn_c) * gckn_ref[...]
        kc_ref[0, hd] = jnp.concatenate([kn_c, krn, zeros32], axis=0).T.astype(BF16)
    vc_ref[0] = kvt[nq:nq + C_WIDTH].astype(BF16)


def _mixin(x, sh, sc, lw, cos_t, sin_t, tm):
    bn, l, _ = x.shape
    const2 = lambda b, t: (0, 0)
    const3 = lambda b, t: (0, 0, 0)
    in_specs = [
        pl.BlockSpec((1, tm, D_MODEL), lambda b, t: (b, t, 0)),
        pl.BlockSpec((1, 1, D_MODEL), lambda b, t: (b, 0, 0)),
        pl.BlockSpec((1, 1, D_MODEL), lambda b, t: (b, 0, 0)),
        pl.BlockSpec((1, D_MODEL), const2),
        pl.BlockSpec((IN_AUG, D_MODEL), const2),
        pl.BlockSpec((A_WIDTH, 1), const2),
        pl.BlockSpec((A_HEADS, CHUNK, CHUNK), const3),
        pl.BlockSpec((A_HEADS, 1, CHUNK), const3),
        pl.BlockSpec((HEAD_DIM, 1), const2),
        pl.BlockSpec((HEAD_DIM, 1), const2),
        pl.BlockSpec((C_Q_LORA, 1), const2),
        pl.BlockSpec((C_HEADS * (C_NOPE + 2 * C_ROPE), C_Q_LORA), const2),
        pl.BlockSpec((C_KV_LORA, 1), const2),
        pl.BlockSpec((C_HEADS * (C_NOPE + C_VDIM), C_KV_LORA), const2),
        pl.BlockSpec((C_NOPE, 1), const2),
        pl.BlockSpec((C_ROPE, 1), const2),
        pl.BlockSpec((C_ROPE, 1), const2),
        pl.BlockSpec((C_NOPE, 1), const2),
        pl.BlockSpec((C_ROPE, 1), const2),
        pl.BlockSpec((C_ROPE, 1), const2),
        pl.BlockSpec((C_ROPE, tm), lambda b, t: (0, t)),
        pl.BlockSpec((C_ROPE, tm), lambda b, t: (0, t)),
    ]
    out_shape = (
        jax.ShapeDtypeStruct((bn, A_WIDTH, l), BF16),
        jax.ShapeDtypeStruct((bn, B_HEADS, QPAD, l), BF16),
        jax.ShapeDtypeStruct((bn, l, B_WIDTH), BF16),
        jax.ShapeDtypeStruct((bn, B_WIDTH, l), BF16),
        jax.ShapeDtypeStruct((bn, C_HEADS, QPAD, l), BF16),
        jax.ShapeDtypeStruct((bn, C_HEADS, l, QPAD), BF16),
        jax.ShapeDtypeStruct((bn, C_WIDTH, l), BF16),
    )
    out_specs = (
        pl.BlockSpec((1, A_WIDTH, tm), lambda b, t: (b, 0, t)),
        pl.BlockSpec((1, B_HEADS, QPAD, tm), lambda b, t: (b, 0, 0, t)),
        pl.BlockSpec((1, tm, B_WIDTH), lambda b, t: (b, t, 0)),
        pl.BlockSpec((1, B_WIDTH, tm), lambda b, t: (b, 0, t)),
        pl.BlockSpec((1, C_HEADS, QPAD, tm), lambda b, t: (b, 0, 0, t)),
        pl.BlockSpec((1, C_HEADS, tm, QPAD), lambda b, t: (b, 0, t, 0)),
        pl.BlockSpec((1, C_WIDTH, tm), lambda b, t: (b, 0, t)),
    )
    return pl.pallas_call(
        functools.partial(_mixin_body, tm=tm),
        grid=(bn, l // tm),
        in_specs=in_specs,
        out_specs=out_specs,
        out_shape=out_shape,
        compiler_params=_cparams(("arbitrary", "arbitrary")),
        name="mix_in",
    )(x, sh, sc, lw["norm_mix"], lw["w_in_t"], lw["avn"], lw["wst"], lw["bs"], lw["gq"], lw["gk"],
      lw["cqa"], lw["wq_t"], lw["ckva"], lw["wkv_t"], lw["gcqn"], lw["gcqr"], lw["gcqrp"], lw["gckn"],
      lw["gckr"], lw["gckrp"], cos_t, sin_t)


def _na_body(q_ref, k0_ref, k1_ref, k2_ref, k3_ref, v0_ref, v1_ref, v2_ref, v3_ref, kx_ref, vx_ref, t_ref,
             o_ref):
    q = q_ref[0, 0]
    k_refs = (k0_ref, k1_ref, k2_ref, k3_ref)
    v_refs = (v0_ref, v1_ref, v2_ref, v3_ref)
    s = jnp.concatenate([jnp.dot(kr[0], q, preferred_element_type=F32) for kr in k_refs], axis=0)
    s = s + t_ref[0, 0]
    sx = jnp.dot(kx_ref[0], q, preferred_element_type=F32)
    m = jnp.maximum(jnp.max(s, axis=0, keepdims=True), jnp.max(sx, axis=0, keepdims=True))
    p = jnp.exp(s - m)
    px = jnp.exp(sx - m)
    den = jnp.sum(p, axis=0, keepdims=True) + jnp.sum(px, axis=0, keepdims=True)
    pb = p.astype(BF16)
    nk = k0_ref.shape[1]
    o = jnp.dot(vx_ref[0], px.astype(BF16), preferred_element_type=F32)
    for j, vr in enumerate(v_refs):
        o = o + jnp.dot(vr[0], pb[j * nk:(j + 1) * nk], preferred_element_type=F32)
    o_ref[0] = (o / den).astype(BF16)


def _na_attention(qb, kb, vb, kxb, vxb, table):
    bn, _, _, l = qb.shape
    nq = NA_QROWS * GRID_W
    nk = NA_KBLK_ROWS * GRID_W
    nblk = l // nq
    nkb = l // nk
    nctx = kxb.shape[1]

    def kmap(j):
        return lambda b, h, i: (b, jnp.clip(2 * i - 1 + j, 0, nkb - 1), h // 2)

    def vmap_(j):
        return lambda b, h, i: (b, h, jnp.clip(2 * i - 1 + j, 0, nkb - 1))

    def tmap(b, h, i):
        return (h, jnp.where(i == 0, 0, jnp.where(i == nblk - 1, 2, 1)), 0, 0)

    in_specs = ([pl.BlockSpec((1, 1, QPAD, nq), lambda b, h, i: (b, h, 0, i))]
                + [pl.BlockSpec((1, nk, LANE), kmap(j)) for j in range(4)]
                + [pl.BlockSpec((1, HEAD_DIM, nk), vmap_(j)) for j in range(4)]
                + [pl.BlockSpec((1, nctx, LANE), lambda b, h, i: (b, 0, h // 2)),
                   pl.BlockSpec((1, HEAD_DIM, nctx), lambda b, h, i: (b, h, 0)),
                   pl.BlockSpec((1, 1, 4 * nk, nq), tmap)])
    return pl.pallas_call(
        _na_body,
        grid=(bn, B_HEADS, nblk),
        in_specs=in_specs,
        out_specs=pl.BlockSpec((1, HEAD_DIM, nq), lambda b, h, i: (b, h, i)),
        out_shape=jax.ShapeDtypeStruct((bn, B_WIDTH, l), BF16),
        compiler_params=_cparams(("arbitrary", "arbitrary", "arbitrary")),
        name="na_attention",
    )(qb, kb, kb, kb, kb, vb, vb, vb, vb, kxb, vxb, table)


def _na_bias_table(rpb, rows):
    nblk = rows // NA_QROWS
    qc = np.arange(GRID_W)
    kc = np.arange(GRID_W)
    c0 = np.clip(qc - NA_COLS // 2, 0, GRID_W - NA_COLS)
    valid_col = (kc[:, None] >= c0[None, :]) & (kc[:, None] < c0[None, :] + NA_COLS)
    dc = np.clip(kc[:, None] - qc[None, :], -(NA_COLS - 1), NA_COLS - 1) + NA_COLS - 1
    tabs = []
    for i in (0, 1, nblk - 1):
        kr = NA_KBLK_ROWS * (2 * i - 1) + np.arange(4 * NA_KBLK_ROWS)
        qr = NA_QROWS * i + np.arange(NA_QROWS)
        r0 = np.clip(qr - NA_ROWS // 2, 0, rows - NA_ROWS)
        valid_row = ((kr[:, None] >= r0[None, :]) & (kr[:, None] < r0[None, :] + NA_ROWS)
                     & (kr[:, None] >= 0) & (kr[:, None] < rows))
        dr = np.clip(kr[:, None] - qr[None, :] + NA_ROWS - 1, 0, 2 * NA_ROWS - 2)
        bias = rpb[:, dr[:, None, :, None], dc[None, :, None, :]]
        valid = valid_row[:, None, :, None] & valid_col[None, :, None, :]
        tabs.append(jnp.where(valid[None], bias.astype(F32), NEG_INF).reshape(
            rpb.shape[0], 4 * NA_KBLK_ROWS * GRID_W, NA_QROWS * GRID_W))
    return jnp.stack(tabs, axis=1)


def _flash_body(*refs, has_extra):
    if has_extra:
        q_ref, k_ref, v_ref, kx_ref, vx_ref, o_ref, m_sc, l_sc, acc_sc = refs
    else:
        q_ref, k_ref, v_ref, o_ref, m_sc, l_sc, acc_sc = refs
    kv = pl.program_id(3)
    nkv = pl.num_programs(3)

    @pl.when(kv == 0)
    def _():
        m_sc[...] = jnp.full(m_sc.shape, -jnp.inf, F32)
        l_sc[...] = jnp.zeros(l_sc.shape, F32)
        acc_sc[...] = jnp.zeros(acc_sc.shape, F32)

    def step(k, v):
        s = jnp.dot(k, q_ref[0, 0], preferred_element_type=F32)
        m_old = m_sc[...]
        m_new = jnp.maximum(m_old, jnp.max(s, axis=0, keepdims=True))
        alpha = jnp.exp(m_old - m_new)
        p = jnp.exp(s - m_new)
        l_sc[...] = alpha * l_sc[...] + jnp.sum(p, axis=0, keepdims=True)
        acc_sc[...] = alpha * acc_sc[...] + jnp.dot(v, p.astype(BF16), preferred_element_type=F32)
        m_sc[...] = m_new

    step(k_ref[0, 0], v_ref[0])

    @pl.when(kv == nkv - 1)
    def _():
        if has_extra:
            step(kx_ref[0, 0], vx_ref[0])
        o_ref[0] = (acc_sc[...] / l_sc[...]).astype(BF16)


def _flash(q, k, v, k_spec, kx=None, vx=None, kx_spec=None, *, tq, tk, name):
    bn, nh, _, lq = q.shape
    lk = v.shape[2]
    has_extra = kx is not None
    in_specs = [pl.BlockSpec((1, 1, QPAD, tq), lambda b, h, i, j: (b, h, 0, i)),
                k_spec,
                pl.BlockSpec((1, HEAD_DIM, tk), lambda b, h, i, j: (b, h, j))]
    args = [q, k, v]
    if has_extra:
        nx = vx.shape[2]
        in_specs += [kx_spec, pl.BlockSpec((1, HEAD_DIM, nx), lambda b, h, i, j: (b, h, 0))]
        args += [kx, vx]
    return pl.pallas_call(
        functools.partial(_flash_body, has_extra=has_extra),
        grid=(bn, nh, lq // tq, lk // tk),
        in_specs=in_specs,
        out_specs=pl.BlockSpec((1, HEAD_DIM, tq), lambda b, h, i, j: (b, h, i)),
        out_shape=jax.ShapeDtypeStruct((bn, nh * HEAD_DIM, lq), BF16),
        scratch_shapes=[pltpu.VMEM((1, tq), F32), pltpu.VMEM((1, tq), F32), pltpu.VMEM((HEAD_DIM, tq), F32)],
        compiler_params=_cparams(("arbitrary", "arbitrary", "arbitrary", "arbitrary")),
        name=name,
    )(*args)


def _mixout_body(x_ref, oa_ref, ob_ref, oc_ref, wt_ref, g1_ref, sh2_ref, sc2_ref, gn2_ref, wr_ref,
                 xo_ref, h2_ref, lg_ref):
    ot = jnp.concatenate([oa_ref[0], ob_ref[0], oc_ref[0]], axis=0)
    out_t = jnp.dot(wt_ref[...], ot, preferred_element_type=F32)
    xn = x_ref[0] + g1_ref[0] * out_t.T
    xo_ref[0] = xn
    h2 = xn * lax.rsqrt(jnp.mean(xn * xn, axis=-1, keepdims=True) + EPS) * gn2_ref[...]
    h2 = h2 * (1.0 + sc2_ref[0]) + sh2_ref[0]
    h2_ref[0] = h2
    lg_ref[0] = jnp.dot(h2, wr_ref[...], preferred_element_type=F32, precision=HIGHEST)


def _mixout(x, oa, ob, oc, g1, sh2, sc2, lw, tm):
    bn, l, _ = x.shape
    const2 = lambda b, t: (0, 0)
    modspec = pl.BlockSpec((1, 1, D_MODEL), lambda b, t: (b, 0, 0))
    rowspec = pl.BlockSpec((1, tm, D_MODEL), lambda b, t: (b, t, 0))
    return pl.pallas_call(
        _mixout_body,
        grid=(bn, l // tm),
        in_specs=[rowspec,
                  pl.BlockSpec((1, A_WIDTH, tm), lambda b, t: (b, 0, t)),
                  pl.BlockSpec((1, B_WIDTH, tm), lambda b, t: (b, 0, t)),
                  pl.BlockSpec((1, C_WIDTH, tm), lambda b, t: (b, 0, t)),
                  pl.BlockSpec((D_MODEL, D_MODEL), const2),
                  modspec, modspec, modspec,
                  pl.BlockSpec((1, D_MODEL), const2),
                  pl.BlockSpec((D_MODEL, ROUTER_PAD), const2)],
        out_specs=(rowspec, rowspec, pl.BlockSpec((1, tm, ROUTER_PAD), lambda b, t: (b, t, 0))),
        out_shape=(jax.ShapeDtypeStruct((bn, l, D_MODEL), F32),
                   jax.ShapeDtypeStruct((bn, l, D_MODEL), F32),
                   jax.ShapeDtypeStruct((bn, l, ROUTER_PAD), F32)),
        compiler_params=_cparams(("arbitrary", "arbitrary")),
        name="mix_out",
    )(x, oa, ob, oc, lw["w_out_t"], g1, sh2, sc2, lw["norm_ffn"], lw["w_route"])


def _moe_body(be_ref, nv_ref, src_ref, dst_ref, sw_ref, w1_ref, w3_ref, w2_ref, h_hbm, y_hbm,
              xbuf, ybuf, w1b, w3b, w2b, gsem, ssem, *, tb):
    i = pl.program_id(0)

    def gather_copy(r, row):
        return pltpu.make_async_copy(h_hbm.at[pl.ds(row, 1)], xbuf.at[pl.ds(r, 1)], gsem)

    def scatter_copy(r, row):
        return pltpu.make_async_copy(ybuf.at[pl.ds(r, 1)], y_hbm.at[pl.ds(row, 1)], ssem)

    nv = nv_ref[i]

    @pl.when(nv > 0)
    def _():
        def g_start(r, carry):
            gather_copy(r, src_ref[0, 0, r]).start()
            return carry
        lax.fori_loop(0, tb, g_start, 0)

        prev = be_ref[jnp.maximum(i - 1, 0)]

        @pl.when((i == 0) | (prev != be_ref[i]))
        def _():
            w1b[...] = w1_ref[0].astype(BF16)
            w3b[...] = w3_ref[0].astype(BF16)
            w2b[...] = w2_ref[0].astype(BF16)

        def g_wait(r, carry):
            gather_copy(r, 0).wait()
            return carry
        lax.fori_loop(0, tb, g_wait, 0)

        xb = xbuf[...].astype(BF16)
        a = jnp.dot(xb, w1b[...], preferred_element_type=F32)
        b = jnp.dot(xb, w3b[...], preferred_element_type=F32)
        hm = (a * (1.0 / (1.0 + jnp.exp(-a))) * b).astype(BF16)
        ybuf[...] = jnp.dot(hm, w2b[...], preferred_element_type=F32) * sw_ref[...]

        def s_start(r, carry):
            scatter_copy(r, dst_ref[0, 0, r]).start()
            return carry
        lax.fori_loop(0, nv, s_start, 0)

        def s_wait(r, carry):
            scatter_copy(r, 0).wait()
            return carry
        lax.fori_loop(0, nv, s_wait, 0)


def _moe(h2, block_e, n_valid, src, dst, sw, w1, w3, w2, n_out_rows, tb):
    nb = block_e.shape[0]
    grid_spec = pltpu.PrefetchScalarGridSpec(
        num_scalar_prefetch=2,
        grid=(nb,),
        in_specs=[
            pl.BlockSpec((1, 1, tb), lambda i, be, nu: (i, 0, 0), memory_space=pltpu.SMEM),
            pl.BlockSpec((1, 1, tb), lambda i, be, nu: (i, 0, 0), memory_space=pltpu.SMEM),
            pl.BlockSpec((tb, 1), lambda i, be, nu: (i, 0)),
            pl.BlockSpec((1, D_MODEL, D_EXPERT), lambda i, be, nu: (be[i], 0, 0)),
            pl.BlockSpec((1, D_MODEL, D_EXPERT), lambda i, be, nu: (be[i], 0, 0)),
            pl.BlockSpec((1, D_EXPERT, D_MODEL), lambda i, be, nu: (be[i], 0, 0)),
            pl.BlockSpec(memory_space=pl.ANY),
        ],
        out_specs=pl.BlockSpec(memory_space=pl.ANY),
        scratch_shapes=[pltpu.VMEM((tb, D_MODEL), F32), pltpu.VMEM((tb, D_MODEL), F32),
                        pltpu.VMEM((D_MODEL, D_EXPERT), BF16), pltpu.VMEM((D_MODEL, D_EXPERT), BF16),
                        pltpu.VMEM((D_EXPERT, D_MODEL), BF16),
                        pltpu.SemaphoreType.DMA(()), pltpu.SemaphoreType.DMA(())],
    )
    return pl.pallas_call(
        functools.partial(_moe_body, tb=tb),
        grid_spec=grid_spec,
        out_shape=jax.ShapeDtypeStruct((n_out_rows, D_MODEL), F32),
        compiler_params=_cparams(("arbitrary",)),
        name="moe_experts",
    )(block_e, n_valid, src.reshape(nb, 1, tb), dst.reshape(nb, 1, tb), sw.reshape(nb * tb, 1), w1, w3, w2, h2)


def _route(lg):
    n = lg.shape[0]
    gl = lg[:, :N_GROUPS]
    el = lg[:, N_GROUPS:N_GROUPS + N_EXPERTS].reshape(n, N_GROUPS, EXPERTS_PER_GROUP)
    g_idx = jnp.argmax(gl, axis=-1)
    g_gate = jnp.take_along_axis(jax.nn.softmax(gl, axis=-1), g_idx[:, None], axis=1)[:, 0]
    e_sel = jnp.take_along_axis(el, g_idx[:, None, None], axis=1)[:, 0, :]
    top_l, top_j = lax.top_k(e_sel, TOP_K)
    weights = g_gate[:, None] * jax.nn.softmax(top_l, axis=-1)
    expert = g_idx[:, None] * EXPERTS_PER_GROUP + top_j
    return expert.astype(jnp.int32), weights


def _dispatch(expert, weights, tb):
    n = expert.shape[0]
    a = n * TOP_K
    flat_e = expert.reshape(a)
    flat_w = weights.reshape(a)
    flat_tok = jnp.arange(a, dtype=jnp.int32) // TOP_K
    flat_k = jnp.arange(a, dtype=jnp.int32) % TOP_K
    order = jnp.argsort(flat_e)
    se = flat_e[order]
    counts = jnp.bincount(flat_e, length=N_EXPERTS)
    starts = jnp.cumsum(counts) - counts
    pcounts = (counts + tb - 1) // tb * tb
    pends = jnp.cumsum(pcounts)
    pstarts = pends - pcounts
    dest = pstarts[se] + (jnp.arange(a) - starts[se])
    nb = -(-a // tb) + N_EXPERTS
    p_slots = nb * tb
    stok = flat_tok[order]
    src = jnp.zeros((p_slots,), jnp.int32).at[dest].set(stok)
    dst = jnp.zeros((p_slots,), jnp.int32).at[dest].set(flat_k[order] * n + stok)
    sw = jnp.zeros((p_slots,), F32).at[dest].set(flat_w[order])
    n_used = (pends[-1] // tb).astype(jnp.int32)
    blk = jnp.arange(nb, dtype=jnp.int32)
    block_e = jnp.minimum(jnp.searchsorted(pends, blk * tb, side="right"), N_EXPERTS - 1).astype(jnp.int32)
    last_e = block_e[jnp.maximum(n_used - 1, 0)]
    block_e = jnp.where(blk < n_used, block_e, last_e)
    n_valid = jnp.clip(counts[block_e] - (blk * tb - pstarts[block_e]), 0, tb)
    n_valid = jnp.where(blk < n_used, n_valid, 0).astype(jnp.int32)
    return block_e, n_valid, src, dst, sw


def _combine_body(x_ref, y0_ref, y1_ref, g2_ref, o_ref):
    o_ref[0] = x_ref[0] + g2_ref[0] * (y0_ref[0] + y1_ref[0])


def _combine(x, y3, g2, row0, tm):
    bn, l, _ = x.shape
    nt = l // tm
    t0 = row0 // tm
    rowspec = pl.BlockSpec((1, tm, D_MODEL), lambda b, t: (b, t, 0))
    return pl.pallas_call(
        _combine_body,
        grid=(bn, nt),
        in_specs=[rowspec,
                  pl.BlockSpec((1, tm, D_MODEL), lambda b, t: (0, t0 + b * nt + t, 0)),
                  pl.BlockSpec((1, tm, D_MODEL), lambda b, t: (1, t0 + b * nt + t, 0)),
                  pl.BlockSpec((1, 1, D_MODEL), lambda b, t: (b, 0, 0))],
        out_specs=rowspec,
        out_shape=jax.ShapeDtypeStruct(x.shape, F32),
        compiler_params=_cparams(("arbitrary", "arbitrary")),
        name="moe_combine",
    )(x, y3, y3, g2)


_ROPE_PERM = np.concatenate([np.arange(8, 16), np.arange(0, 8), np.arange(24, 32), np.arange(16, 24)])
_ROPE_SIGN = np.concatenate([-np.ones(8), np.ones(8), -np.ones(8), np.ones(8)]).astype(np.float32)


def _rope_tables_t(l):
    half = C_ROPE // 2
    inv = ROPE_BASE ** (-jnp.arange(0, half, 2, dtype=F32) / half)
    pos = jnp.arange(l)
    ang_r = (pos // GRID_W).astype(F32)[None, :] * inv[:, None]
    ang_c = (pos % GRID_W).astype(F32)[None, :] * inv[:, None]
    cos_t = jnp.concatenate([jnp.cos(ang_r)] * 2 + [jnp.cos(ang_c)] * 2, axis=0)
    sin_t = jnp.concatenate([jnp.sin(ang_r)] * 2 + [jnp.sin(ang_c)] * 2, axis=0)
    return cos_t, sin_t


def _col(v):
    return v.astype(F32)[:, None]


def _layer_weights(i, p):
    w_in = p["w_in"][i]
    kr0 = IN_A + IN_B + C_Q_LORA + C_KV_LORA
    kr_cols = w_in[:, kr0:kr0 + C_ROPE]
    w_aug = jnp.concatenate([w_in, kr_cols[:, _ROPE_PERM] * _ROPE_SIGN], axis=1)

    wq = p["c_w_q_up"][i].reshape(C_Q_LORA, C_HEADS, C_QK)
    wq_n = wq[:, :, :C_NOPE].reshape(C_Q_LORA, C_HEADS * C_NOPE)
    wq_r = wq[:, :, C_NOPE:]
    wq_rp = (wq_r[:, :, _ROPE_PERM] * _ROPE_SIGN).reshape(C_Q_LORA, C_HEADS * C_ROPE)
    wq_all = jnp.concatenate([wq_n, wq_r.reshape(C_Q_LORA, C_HEADS * C_ROPE), wq_rp], axis=1)

    wkv = p["c_w_kv_up"][i].reshape(C_KV_LORA, C_HEADS, C_NOPE + C_VDIM)
    wkv_all = jnp.concatenate([wkv[:, :, :C_NOPE].reshape(C_KV_LORA, C_HEADS * C_NOPE),
                               wkv[:, :, C_NOPE:].reshape(C_KV_LORA, C_HEADS * C_VDIM)], axis=1)

    cqn = p["c_q_norm"][i].astype(F32) * (C_QK ** -0.5)
    ckn = p["c_k_norm"][i].astype(F32)
    w_route = jnp.concatenate([p["moe_w_group"][i], p["moe_w_router"][i],
                               jnp.zeros((D_MODEL, ROUTER_PAD - N_GROUPS - N_EXPERTS), F32)], axis=1)
    return {
        "norm_mix": p["norm_mix"][i][None, :],
        "w_in_t": w_aug.T.astype(BF16),
        "avn": _col(p["a_v_norm"][i]),
        "wst": jnp.transpose(p["a_w_s"][i], (0, 2, 1)).astype(BF16),
        "bs": p["a_b_s"][i][:, None, :],
        "gq": _col(p["b_q_norm"][i] * (HEAD_DIM ** -0.5)),
        "gk": _col(p["b_k_norm"][i]),
        "cqa": _col(p["c_q_a_norm"][i]),
        "wq_t": wq_all.T.astype(BF16),
        "ckva": _col(p["c_kv_a_norm"][i]),
        "wkv_t": wkv_all.T.astype(BF16),
        "gcqn": _col(cqn[:C_NOPE]),
        "gcqr": _col(cqn[C_NOPE:]),
        "gcqrp": _col(cqn[C_NOPE:][_ROPE_PERM]),
        "gckn": _col(ckn[:C_NOPE]),
        "gckr": _col(ckn[C_NOPE:]),
        "gckrp": _col(ckn[C_NOPE:][_ROPE_PERM]),
        "w_out_t": p["w_out"][i].T.astype(BF16),
        "norm_ffn": p["norm_ffn"][i][None, :],
        "w_route": w_route,
    }


def _pick_tile(l, pref):
    t = min(l, pref)
    while l % t:
        t //= 2
    return t


def kernel(x, c, ctx, c_ctx, w_ada, b_ada, norm_mix, w_in, a_v_norm, a_w_s, a_b_s, b_q_norm, b_k_norm, b_rpb,
           c_q_a_norm, c_w_q_up, c_kv_a_norm, c_w_kv_up, c_q_norm, c_k_norm, w_out, norm_ffn,
           moe_w_group, moe_w_router, moe_w1, moe_w3, moe_w2):
    p = dict(norm_mix=norm_mix, w_in=w_in, a_v_norm=a_v_norm, a_w_s=a_w_s, a_b_s=a_b_s, b_q_norm=b_q_norm,
             b_k_norm=b_k_norm, c_q_a_norm=c_q_a_norm, c_w_q_up=c_w_q_up, c_kv_a_norm=c_kv_a_norm,
             c_w_kv_up=c_w_kv_up, c_q_norm=c_q_norm, c_k_norm=c_k_norm, w_out=w_out, norm_ffn=norm_ffn,
             moe_w_group=moe_w_group, moe_w_router=moe_w_router)
    bn, l, d = x.shape
    nc = ctx.shape[1]
    depth = w_ada.shape[0]
    rows = l // GRID_W
    tm = _pick_tile(l, 512)
    tq = _pick_tile(l, 512)
    tk = _pick_tile(l, 512)

    cos_t, sin_t = _rope_tables_t(l)
    cos_x = jnp.ones((C_ROPE, nc), F32)
    sin_x = jnp.zeros((C_ROPE, nc), F32)
    cc8 = jnp.concatenate([c, c_ctx[None, :], jnp.zeros((8 - bn - 1, d), F32)], axis=0)

    xc = ctx
    for i in range(depth):
        need_ctx = i < depth - 1
        lw = _layer_weights(i, p)
        mod = _ada(cc8, w_ada[i], b_ada[i][None, :])
        mods = [mod[:bn, j * d:(j + 1) * d][:, None, :] for j in range(6)]
        modx = [jnp.broadcast_to(mod[bn, j * d:(j + 1) * d][None, None, :], (bn, 1, d)) for j in range(6)]
        sh1, s1, g1, sh2, s2, g2 = mods
        sh1x, s1x, g1x, sh2x, s2x, g2x = modx

        oa, qb, kb, vb, qc, kc, vc = _mixin(x, sh1, s1, lw, cos_t, sin_t, tm)
        oax, qbx, kbx, vbx, qcx, kcx, vcx = _mixin(xc, sh1x, s1x, lw, cos_x, sin_x, nc)

        table = _na_bias_table(b_rpb[i], rows)
        ob = _na_attention(qb, kb, vb, kbx, vbx, table)
        oc = _flash(qc, kc, vc, pl.BlockSpec((1, 1, tk, QPAD), lambda b, h, ii, j: (b, h, j, 0)),
                    kcx, vcx, pl.BlockSpec((1, 1, nc, QPAD), lambda b, h, ii, j: (b, h, 0, 0)),
                    tq=tq, tk=tk, name="mla_attention")
        x, h2, lg = _mixout(x, oa, ob, oc, g1, sh2, s2, lw, tm)

        n_lat = bn * l
        if need_ctx:
            obx = _flash(qbx, kbx[:, None], vbx,
                         pl.BlockSpec((1, 1, nc, LANE), lambda b, h, ii, j: (b, 0, j, h // 2)),
                         tq=nc, tk=nc, name="ctx_dense_attention")
            ocx = _flash(qcx, kcx, vcx, pl.BlockSpec((1, 1, nc, QPAD), lambda b, h, ii, j: (b, h, j, 0)),
                         tq=nc, tk=nc, name="ctx_mla_attention")
            xc, h2x, lgx = _mixout(xc, oax, obx, ocx, g1x, sh2x, s2x, lw, nc)
            h2_all = jnp.concatenate([h2.reshape(n_lat, d), h2x.reshape(bn * nc, d)], axis=0)
            lg_all = jnp.concatenate([lg.reshape(n_lat, ROUTER_PAD), lgx.reshape(bn * nc, ROUTER_PAD)], axis=0)
        else:
            h2_all = h2.reshape(n_lat, d)
            lg_all = lg.reshape(n_lat, ROUTER_PAD)

        n_tok = h2_all.shape[0]
        expert, weights = _route(lg_all)
        block_e, n_valid, src, dst, sw = _dispatch(expert, weights, MOE_BLOCK)
        y = _moe(h2_all, block_e, n_valid, src, dst, sw, moe_w1[i], moe_w3[i], moe_w2[i],
                 TOP_K * n_tok, MOE_BLOCK)
        y3 = y.reshape(TOP_K, n_tok, d)
        x = _combine(x, y3, g2, 0, tm)
        if need_ctx:
            xc = _combine(xc, y3, g2x, n_lat, nc)
    return x
```

```python
import functools

import numpy as np
import jax
import jax.numpy as jnp
from jax import lax
from jax.experimental import pallas as pl
from jax.experimental.pallas import tpu as pltpu

F32 = jnp.float32
BF16 = jnp.bfloat16

D_MODEL = 1024
GRID_W = 64
HEAD_DIM = 64
EPS = 1e-6
NEG_INF = -1e30

A_HEADS = 4
A_WIDTH = A_HEADS * HEAD_DIM
CHUNK = 128
B_HEADS = 6
B_WIDTH = B_HEADS * HEAD_DIM
NA_ROWS = 8
NA_COLS = 16
C_HEADS = 6
C_NOPE = 64
C_ROPE = 32
C_QK = C_NOPE + C_ROPE
C_VDIM = 64
C_Q_LORA = 384
C_KV_LORA = 256
C_WIDTH = C_HEADS * C_VDIM
ROPE_BASE = 10000.0
IN_A = 2 * A_WIDTH
IN_B = 3 * B_WIDTH
IN_C = C_Q_LORA + C_KV_LORA + C_ROPE
IN_AUG = IN_A + IN_B + IN_C + C_ROPE

N_GROUPS = 8
EXPERTS_PER_GROUP = 8
N_EXPERTS = N_GROUPS * EXPERTS_PER_GROUP
TOP_K = 2
D_EXPERT = D_MODEL // 2
ROUTER_PAD = 128

V7X_VMEM_LIMIT_BYTES = 56 * 1024 * 1024
LANE = 128
QPAD = 128

NA_QROWS = 8
NA_KBLK_ROWS = 4
MOE_BLOCK = 256

HIGHEST = lax.Precision.HIGHEST


def _cparams(sem):
    return pltpu.CompilerParams(dimension_semantics=sem, vmem_limit_bytes=V7X_VMEM_LIMIT_BYTES)


def _ada_body(c_ref, w_ref, b_ref, o_ref):
    cc = c_ref[...]
    s = cc * (1.0 / (1.0 + jnp.exp(-cc)))
    o_ref[...] = jnp.dot(s, w_ref[...], preferred_element_type=F32, precision=HIGHEST) + b_ref[...]


def _ada(cc8, w, b):
    n_out = w.shape[1]
    tn = 1024
    return pl.pallas_call(
        _ada_body,
        grid=(n_out // tn,),
        in_specs=[pl.BlockSpec((8, D_MODEL), lambda j: (0, 0)),
                  pl.BlockSpec((D_MODEL, tn), lambda j: (0, j)),
                  pl.BlockSpec((1, tn), lambda j: (0, j))],
        out_specs=pl.BlockSpec((8, tn), lambda j: (0, j)),
        out_shape=jax.ShapeDtypeStruct((8, n_out), F32),
        compiler_params=_cparams(("arbitrary",)),
        name="ada_mod",
    )(cc8, w, b)


def _rms_rows(v):
    return lax.rsqrt(jnp.mean(v * v, axis=0, keepdims=True) + EPS)


def _gelu_tanh(x):
    return 0.5 * x * (1.0 + jnp.tanh(0.7978845608028654 * (x + 0.044715 * (x * x * x))))


def _mixin_body(x_ref, sh_ref, sc_ref, gn_ref, wt_ref, avn_ref, wst_ref, bs_ref, gq_ref, gk_ref,
                cqa_ref, wq_ref, ckva_ref, wkv_ref, gcqn_ref, gcqr_ref, gcqrp_ref, gckn_ref, gckr_ref,
                gckrp_ref, cos_ref, sin_ref,
                oa_ref, qb_ref, kb_ref, vb_ref, qc_ref, kc_ref, vc_ref, *, tm):
    x = x_ref[0]
    h = x * lax.rsqrt(jnp.mean(x * x, axis=-1, keepdims=True) + EPS) * gn_ref[...]
    h = h * (1.0 + sc_ref[0]) + sh_ref[0]
    zt = lax.dot_general(wt_ref[...], h.astype(BF16), (((1,), (1,)), ((), ())),
                         preferred_element_type=F32)

    ga = _gelu_tanh(zt[0:IN_A])
    u = ga[0:A_WIDTH]
    v = ga[A_WIDTH:IN_A]
    vn = (v * _rms_rows(v) * avn_ref[...]).astype(BF16)
    for hd in range(A_HEADS):
        r0, r1 = hd * HEAD_DIM, (hd + 1) * HEAD_DIM
        for c in range(tm // CHUNK):
            c0, c1 = c * CHUNK, (c + 1) * CHUNK
            vm = jnp.dot(vn[r0:r1, c0:c1], wst_ref[hd], preferred_element_type=F32) + bs_ref[hd]
            oa_ref[0, r0:r1, c0:c1] = (u[r0:r1, c0:c1] * vm).astype(BF16)

    zb = zt[IN_A:IN_A + IN_B]
    zeros64 = jnp.zeros((HEAD_DIM, tm), F32)
    kn = []
    for hd in range(B_HEADS):
        q = zb[hd * HEAD_DIM:(hd + 1) * HEAD_DIM]
        k = zb[B_WIDTH + hd * HEAD_DIM:B_WIDTH + (hd + 1) * HEAD_DIM]
        qn = q * _rms_rows(q) * gq_ref[...]
        kn.append(k * _rms_rows(k) * gk_ref[...])
        pair = [qn, zeros64] if hd % 2 == 0 else [zeros64, qn]
        qb_ref[0, hd] = jnp.concatenate(pair, axis=0).astype(BF16)
    for p in range(B_HEADS // 2):
        kt = jnp.concatenate([kn[2 * p], kn[2 * p + 1]], axis=0)
        kb_ref[0, :, p * LANE:(p + 1) * LANE] = kt.T.astype(BF16)
    vb_ref[0] = zb[2 * B_WIDTH:3 * B_WIDTH].astype(BF16)

    zc = zt[IN_A + IN_B:IN_AUG]
    ql = zc[0:C_Q_LORA]
    kvl = zc[C_Q_LORA:C_Q_LORA + C_KV_LORA]
    kr = zc[C_Q_LORA + C_KV_LORA:C_Q_LORA + C_KV_LORA + C_ROPE]
    krp = zc[C_Q_LORA + C_KV_LORA + C_ROPE:C_Q_LORA + C_KV_LORA + 2 * C_ROPE]
    qln = (ql * _rms_rows(ql) * cqa_ref[...]).astype(BF16)
    kvln = (kvl * _rms_rows(kvl) * ckva_ref[...]).astype(BF16)
    qt = jnp.dot(wq_ref[...], qln, preferred_element_type=F32)
    kvt = jnp.dot(wkv_ref[...], kvln, preferred_element_type=F32)
    cos = cos_ref[...]
    sin = sin_ref[...]
    krn = _rms_rows(kr) * (gckr_ref[...] * kr * cos + gckrp_ref[...] * krp * sin)
    zeros32 = jnp.zeros((QPAD - C_QK, tm), F32)
    nq = C_HEADS * C_NOPE
    nr = C_HEADS * C_ROPE
    for hd in range(C_HEADS):
        qn = qt[hd * C_NOPE:(hd + 1) * C_NOPE]
        qn = qn * _rms_rows(qn) * gcqn_ref[...]
        qr = qt[nq + hd * C_ROPE:nq + (hd + 1) * C_ROPE]
        qrp = qt[nq + nr + hd * C_ROPE:nq + nr + (hd + 1) * C_ROPE]
        qrn = _rms_rows(qr) * (gcqr_ref[...] * qr * cos + gcqrp_ref[...] * qrp * sin)
        qc_ref[0, hd] = jnp.concatenate([qn, qrn, zeros32], axis=0).astype(BF16)
        kn_c = kvt[hd * C_NOPE:(hd + 1) * C_NOPE]
        kn_c = kn_c * _rms_rows(kn_c) * gckn_ref[...]
        kc_ref[0, hd] = jnp.concatenate([kn_c, krn, zeros32], axis=0).T.astype(BF16)
    vc_ref[0, 0] = kvt[nq:nq + C_WIDTH].astype(BF16)


def _mixin(x, sh, sc, lw, cos_t, sin_t, tm):
    bn, l, _ = x.shape
    const2 = lambda b, t: (0, 0)
    const3 = lambda b, t: (0, 0, 0)
    in_specs = [
        pl.BlockSpec((1, tm, D_MODEL), lambda b, t: (b, t, 0)),
        pl.BlockSpec((1, 1, D_MODEL), lambda b, t: (b, 0, 0)),
        pl.BlockSpec((1, 1, D_MODEL), lambda b, t: (b, 0, 0)),
        pl.BlockSpec((1, D_MODEL), const2),
        pl.BlockSpec((IN_AUG, D_MODEL), const2),
        pl.BlockSpec((A_WIDTH, 1), const2),
        pl.BlockSpec((A_HEADS, CHUNK, CHUNK), const3),
        pl.BlockSpec((A_HEADS, 1, CHUNK), const3),
        pl.BlockSpec((HEAD_DIM, 1), const2),
        pl.BlockSpec((HEAD_DIM, 1), const2),
        pl.BlockSpec((C_Q_LORA, 1), const2),
        pl.BlockSpec((C_HEADS * (C_NOPE + 2 * C_ROPE), C_Q_LORA), const2),
        pl.BlockSpec((C_KV_LORA, 1), const2),
        pl.BlockSpec((C_HEADS * (C_NOPE + C_VDIM), C_KV_LORA), const2),
        pl.BlockSpec((C_NOPE, 1), const2),
        pl.BlockSpec((C_ROPE, 1), const2),
        pl.BlockSpec((C_ROPE, 1), const2),
        pl.BlockSpec((C_NOPE, 1), const2),
        pl.BlockSpec((C_ROPE, 1), const2),
        pl.BlockSpec((C_ROPE, 1), const2),
        pl.BlockSpec((C_ROPE, tm), lambda b, t: (0, t)),
        pl.BlockSpec((C_ROPE, tm), lambda b, t: (0, t)),
    ]
    out_shape = (
        jax.ShapeDtypeStruct((bn, A_WIDTH, l), BF16),
        jax.ShapeDtypeStruct((bn, B_HEADS, QPAD, l), BF16),
        jax.ShapeDtypeStruct((bn, l, B_WIDTH), BF16),
        jax.ShapeDtypeStruct((bn, B_WIDTH, l), BF16),
        jax.ShapeDtypeStruct((bn, C_HEADS, QPAD, l), BF16),
        jax.ShapeDtypeStruct((bn, C_HEADS, l, QPAD), BF16),
        jax.ShapeDtypeStruct((bn, l // tm, C_WIDTH, tm), BF16),
    )
    out_specs = (
        pl.BlockSpec((1, A_WIDTH, tm), lambda b, t: (b, 0, t)),
        pl.BlockSpec((1, B_HEADS, QPAD, tm), lambda b, t: (b, 0, 0, t)),
        pl.BlockSpec((1, tm, B_WIDTH), lambda b, t: (b, t, 0)),
        pl.BlockSpec((1, B_WIDTH, tm), lambda b, t: (b, 0, t)),
        pl.BlockSpec((1, C_HEADS, QPAD, tm), lambda b, t: (b, 0, 0, t)),
        pl.BlockSpec((1, C_HEADS, tm, QPAD), lambda b, t: (b, 0, t, 0)),
        pl.BlockSpec((1, 1, C_WIDTH, tm), lambda b, t: (b, t, 0, 0)),
    )
    return pl.pallas_call(
        functools.partial(_mixin_body, tm=tm),
        grid=(bn, l // tm),
        in_specs=in_specs,
        out_specs=out_specs,
        out_shape=out_shape,
        compiler_params=_cparams(("arbitrary", "arbitrary")),
        name="mix_in",
    )(x, sh, sc, lw["norm_mix"], lw["w_in_t"], lw["avn"], lw["wst"], lw["bs"], lw["gq"], lw["gk"],
      lw["cqa"], lw["wq_t"], lw["ckva"], lw["wkv_t"], lw["gcqn"], lw["gcqr"], lw["gcqrp"], lw["gckn"],
      lw["gckr"], lw["gckrp"], cos_t, sin_t)


def _na_body(q_ref, k0_ref, k1_ref, k2_ref, k3_ref, v0_ref, v1_ref, v2_ref, v3_ref, kx_ref, vx_ref, t_ref,
             o_ref):
    q = q_ref[0, 0]
    k_refs = (k0_ref, k1_ref, k2_ref, k3_ref)
    v_refs = (v0_ref, v1_ref, v2_ref, v3_ref)
    s = jnp.concatenate([jnp.dot(kr[0], q, preferred_element_type=F32) for kr in k_refs], axis=0)
    s = s + t_ref[0, 0]
    sx = jnp.dot(kx_ref[0], q, preferred_element_type=F32)
    m = jnp.maximum(jnp.max(s, axis=0, keepdims=True), jnp.max(sx, axis=0, keepdims=True))
    p = jnp.exp(s - m)
    px = jnp.exp(sx - m)
    den = jnp.sum(p, axis=0, keepdims=True) + jnp.sum(px, axis=0, keepdims=True)
    pb = p.astype(BF16)
    nk = k0_ref.shape[1]
    o = jnp.dot(vx_ref[0], px.astype(BF16), preferred_element_type=F32)
    for j, vr in enumerate(v_refs):
        o = o + jnp.dot(vr[0], pb[j * nk:(j + 1) * nk], preferred_element_type=F32)
    o_ref[0] = (o / den).astype(BF16)


def _na_attention(qb, kb, vb, kxb, vxb, table):
    bn, _, _, l = qb.shape
    nq = NA_QROWS * GRID_W
    nk = NA_KBLK_ROWS * GRID_W
    nblk = l // nq
    nkb = l // nk
    nctx = kxb.shape[1]

    def kmap(j):
        return lambda b, h, i: (b, jnp.clip(2 * i - 1 + j, 0, nkb - 1), h // 2)

    def vmap_(j):
        return lambda b, h, i: (b, h, jnp.clip(2 * i - 1 + j, 0, nkb - 1))

    def tmap(b, h, i):
        return (h, jnp.where(i == 0, 0, jnp.where(i == nblk - 1, 2, 1)), 0, 0)

    in_specs = ([pl.BlockSpec((1, 1, QPAD, nq), lambda b, h, i: (b, h, 0, i))]
                + [pl.BlockSpec((1, nk, LANE), kmap(j)) for j in range(4)]
                + [pl.BlockSpec((1, HEAD_DIM, nk), vmap_(j)) for j in range(4)]
                + [pl.BlockSpec((1, nctx, LANE), lambda b, h, i: (b, 0, h // 2)),
                   pl.BlockSpec((1, HEAD_DIM, nctx), lambda b, h, i: (b, h, 0)),
                   pl.BlockSpec((1, 1, 4 * nk, nq), tmap)])
    return pl.pallas_call(
        _na_body,
        grid=(bn, B_HEADS, nblk),
        in_specs=in_specs,
        out_specs=pl.BlockSpec((1, HEAD_DIM, nq), lambda b, h, i: (b, h, i)),
        out_shape=jax.ShapeDtypeStruct((bn, B_WIDTH, l), BF16),
        compiler_params=_cparams(("arbitrary", "arbitrary", "arbitrary")),
        name="na_attention",
    )(qb, kb, kb, kb, kb, vb, vb, vb, vb, kxb, vxb, table)


def _na_bias_table(rpb, rows):
    nblk = rows // NA_QROWS
    qc = np.arange(GRID_W)
    kc = np.arange(GRID_W)
    c0 = np.clip(qc - NA_COLS // 2, 0, GRID_W - NA_COLS)
    valid_col = (kc[:, None] >= c0[None, :]) & (kc[:, None] < c0[None, :] + NA_COLS)
    dc = np.clip(kc[:, None] - qc[None, :], -(NA_COLS - 1), NA_COLS - 1) + NA_COLS - 1
    dc_onehot = (dc[None] == np.arange(2 * NA_COLS - 1)[:, None, None]).astype(np.float32)
    tabs = []
    for i in (0, 1, nblk - 1):
        kr = NA_KBLK_ROWS * (2 * i - 1) + np.arange(4 * NA_KBLK_ROWS)
        qr = NA_QROWS * i + np.arange(NA_QROWS)
        r0 = np.clip(qr - NA_ROWS // 2, 0, rows - NA_ROWS)
        valid_row = ((kr[:, None] >= r0[None, :]) & (kr[:, None] < r0[None, :] + NA_ROWS)
                     & (kr[:, None] >= 0) & (kr[:, None] < rows))
        dr = np.clip(kr[:, None] - qr[None, :] + NA_ROWS - 1, 0, 2 * NA_ROWS - 2)
        bias = jnp.einsum("hkqd,dcx->hkcqx", rpb[:, dr].astype(F32), dc_onehot, precision=HIGHEST)
        valid = valid_row[:, None, :, None] & valid_col[None, :, None, :]
        tabs.append(jnp.where(valid[None], bias.astype(F32), NEG_INF).reshape(
            rpb.shape[0], 4 * NA_KBLK_ROWS * GRID_W, NA_QROWS * GRID_W))
    return jnp.stack(tabs, axis=1)


def _flash_body(*refs, has_extra):
    if has_extra:
        q_ref, k_ref, v_ref, kx_ref, vx_ref, o_ref, m_sc, l_sc, acc_sc = refs
    else:
        q_ref, k_ref, v_ref, o_ref, m_sc, l_sc, acc_sc = refs
    kv = pl.program_id(3)
    nkv = pl.num_programs(3)

    @pl.when(kv == 0)
    def _():
        m_sc[...] = jnp.full(m_sc.shape, -jnp.inf, F32)
        l_sc[...] = jnp.zeros(l_sc.shape, F32)
        acc_sc[...] = jnp.zeros(acc_sc.shape, F32)

    def step(k, v):
        s = jnp.dot(k, q_ref[0, 0], preferred_element_type=F32)
        m_old = m_sc[...]
        m_new = jnp.maximum(m_old, jnp.max(s, axis=0, keepdims=True))
        alpha = jnp.exp(m_old - m_new)
        p = jnp.exp(s - m_new)
        l_sc[...] = alpha * l_sc[...] + jnp.sum(p, axis=0, keepdims=True)
        acc_sc[...] = alpha * acc_sc[...] + jnp.dot(v, p.astype(BF16), preferred_element_type=F32)
        m_sc[...] = m_new

    step(k_ref[0, 0], v_ref[0])

    @pl.when(kv == nkv - 1)
    def _():
        if has_extra:
            step(kx_ref[0, 0], vx_ref[0])
        o_ref[0] = (acc_sc[...] / l_sc[...]).astype(BF16)


def _flash(q, k, v, k_spec, kx=None, vx=None, kx_spec=None, *, tq, tk, name):
    bn, nh, _, lq = q.shape
    lk = v.shape[2]
    has_extra = kx is not None
    in_specs = [pl.BlockSpec((1, 1, QPAD, tq), lambda b, h, i, j: (b, h, 0, i)),
                k_spec,
                pl.BlockSpec((1, HEAD_DIM, tk), lambda b, h, i, j: (b, h, j))]
    args = [q, k, v]
    if has_extra:
        nx = vx.shape[2]
        in_specs += [kx_spec, pl.BlockSpec((1, HEAD_DIM, nx), lambda b, h, i, j: (b, h, 0))]
        args += [kx, vx]
    return pl.pallas_call(
        functools.partial(_flash_body, has_extra=has_extra),
        grid=(bn, nh, lq // tq, lk // tk),
        in_specs=in_specs,
        out_specs=pl.BlockSpec((1, HEAD_DIM, tq), lambda b, h, i, j: (b, h, i)),
        out_shape=jax.ShapeDtypeStruct((bn, nh * HEAD_DIM, lq), BF16),
        scratch_shapes=[pltpu.VMEM((1, tq), F32), pltpu.VMEM((1, tq), F32), pltpu.VMEM((HEAD_DIM, tq), F32)],
        compiler_params=_cparams(("arbitrary", "arbitrary", "arbitrary", "arbitrary")),
        name=name,
    )(*args)


ONES_ROWS = 16


def _mla_body(q_ref, k_ref, v_ref, kx_ref, vx_ref, o_ref, *, tk, nchunks, unroll):
    q = q_ref[0, 0]
    tq = q.shape[1]

    def chunk(k, v, carry):
        m, acc = carry
        s = jnp.dot(k, q, preferred_element_type=F32).astype(BF16)
        m_new = jnp.maximum(m, jnp.max(s, axis=0, keepdims=True).astype(F32))
        alpha = jnp.exp(m - m_new)
        p = jnp.exp(s - m_new.astype(BF16))
        v_ext = jnp.concatenate([v, jnp.ones((ONES_ROWS, v.shape[1]), v.dtype)], axis=0)
        acc = alpha * acc + jnp.dot(v_ext, p, preferred_element_type=F32)
        return m_new, acc

    def body(j, carry):
        off = pl.multiple_of(j * tk, tk)
        return chunk(k_ref[0, 0, pl.ds(off, tk), :], v_ref[0, j], carry)

    carry = (jnp.full((1, tq), -jnp.inf, F32), jnp.zeros((HEAD_DIM + ONES_ROWS, tq), F32))
    carry = lax.fori_loop(0, nchunks, body, carry, unroll=unroll)
    _, acc = chunk(kx_ref[0, 0], vx_ref[0, 0], carry)
    o_ref[0] = (acc[0:HEAD_DIM] / acc[HEAD_DIM:HEAD_DIM + 1]).astype(BF16)


def _mla_attention(q, k, v, kx, vx, tq):
    bn, nh, _, l = q.shape
    nchunks, tk = v.shape[1], v.shape[3]
    nc = kx.shape[2]
    return pl.pallas_call(
        functools.partial(_mla_body, tk=tk, nchunks=nchunks, unroll=2),
        grid=(bn, nh, l // tq),
        in_specs=[pl.BlockSpec((1, 1, QPAD, tq), lambda b, h, i: (b, h, 0, i)),
                  pl.BlockSpec((1, 1, l, QPAD), lambda b, h, i: (b, h, 0, 0)),
                  pl.BlockSpec((1, nchunks, HEAD_DIM, tk), lambda b, h, i: (b, 0, h, 0)),
                  pl.BlockSpec((1, 1, nc, QPAD), lambda b, h, i: (b, h, 0, 0)),
                  pl.BlockSpec((1, 1, HEAD_DIM, nc), lambda b, h, i: (b, 0, h, 0))],
        out_specs=pl.BlockSpec((1, HEAD_DIM, tq), lambda b, h, i: (b, h, i)),
        out_shape=jax.ShapeDtypeStruct((bn, nh * HEAD_DIM, l), BF16),
        compiler_params=_cparams(("arbitrary", "arbitrary", "arbitrary")),
        name="mla_attention",
    )(q, k, v, kx, vx)


def _mixout_body(x_ref, oa_ref, ob_ref, oc_ref, wt_ref, g1_ref, sh2_ref, sc2_ref, gn2_ref, wr_ref,
                 xo_ref, h2_ref, lg_ref):
    ot = jnp.concatenate([oa_ref[0], ob_ref[0], oc_ref[0]], axis=0)
    out_t = jnp.dot(wt_ref[...], ot, preferred_element_type=F32)
    xn = x_ref[0] + g1_ref[0] * out_t.T
    xo_ref[0] = xn
    h2 = xn * lax.rsqrt(jnp.mean(xn * xn, axis=-1, keepdims=True) + EPS) * gn2_ref[...]
    h2 = h2 * (1.0 + sc2_ref[0]) + sh2_ref[0]
    h2_ref[0] = h2
    lg_ref[0] = jnp.dot(h2, wr_ref[...], preferred_element_type=F32, precision=HIGHEST)


def _mixout(x, oa, ob, oc, g1, sh2, sc2, lw, tm):
    bn, l, _ = x.shape
    const2 = lambda b, t: (0, 0)
    modspec = pl.BlockSpec((1, 1, D_MODEL), lambda b, t: (b, 0, 0))
    rowspec = pl.BlockSpec((1, tm, D_MODEL), lambda b, t: (b, t, 0))
    return pl.pallas_call(
        _mixout_body,
        grid=(bn, l // tm),
        in_specs=[rowspec,
                  pl.BlockSpec((1, A_WIDTH, tm), lambda b, t: (b, 0, t)),
                  pl.BlockSpec((1, B_WIDTH, tm), lambda b, t: (b, 0, t)),
                  pl.BlockSpec((1, C_WIDTH, tm), lambda b, t: (b, 0, t)),
                  pl.BlockSpec((D_MODEL, D_MODEL), const2),
                  modspec, modspec, modspec,
                  pl.BlockSpec((1, D_MODEL), const2),
                  pl.BlockSpec((D_MODEL, ROUTER_PAD), const2)],
        out_specs=(rowspec, rowspec, pl.BlockSpec((1, tm, ROUTER_PAD), lambda b, t: (b, t, 0))),
        out_shape=(jax.ShapeDtypeStruct((bn, l, D_MODEL), F32),
                   jax.ShapeDtypeStruct((bn, l, D_MODEL), F32),
                   jax.ShapeDtypeStruct((bn, l, ROUTER_PAD), F32)),
        compiler_params=_cparams(("arbitrary", "arbitrary")),
        name="mix_out",
    )(x, oa, ob, oc, lw["w_out_t"], g1, sh2, sc2, lw["norm_ffn"], lw["w_route"])


def _moe_body(be_ref, nv_ref, src_ref, srcn_ref, dst_ref, sw_ref, w1_ref, w3_ref, w2_ref, h_hbm, y_hbm,
              xbuf, ybuf, w1b, w3b, w2b, gsem, ssem, *, tb, nb):
    i = pl.program_id(0)
    slot = lax.rem(i, 2)
    other = 1 - slot
    nv = nv_ref[i]
    nv_prev = jnp.where(i > 0, nv_ref[jnp.maximum(i - 1, 0)], 0)
    nv_next = jnp.where(i + 1 < nb, nv_ref[jnp.minimum(i + 1, nb - 1)], 0)

    def gather_start(idx_ref, s):
        def one(r, carry):
            pltpu.make_async_copy(h_hbm.at[pl.ds(idx_ref[0, 0, r], 1)], xbuf.at[s, pl.ds(r, 1)],
                                  gsem.at[s]).start()
            return carry
        lax.fori_loop(0, tb, one, 0, unroll=8)

    def gather_wait(s):
        pltpu.make_async_copy(h_hbm.at[pl.ds(0, tb)], xbuf.at[s], gsem.at[s]).wait()

    def scatter_row(r, row, s):
        return pltpu.make_async_copy(ybuf.at[s, pl.ds(r, 1)], y_hbm.at[pl.ds(row, 1)], ssem.at[s])

    def scatter_start(s, n):
        @pl.when(n == tb)
        def _():
            def one(r, carry):
                scatter_row(r, dst_ref[0, 0, r], s).start()
                return carry
            lax.fori_loop(0, tb, one, 0, unroll=8)

        @pl.when(n < tb)
        def _():
            def one(r, carry):
                scatter_row(r, dst_ref[0, 0, r], s).start()
                return carry
            lax.fori_loop(0, n, one, 0)

    def scatter_wait(s, n):
        @pl.when(n == tb)
        def _():
            pltpu.make_async_copy(ybuf.at[s], y_hbm.at[pl.ds(0, tb)], ssem.at[s]).wait()

        @pl.when(n < tb)
        def _():
            def one(r, carry):
                scatter_row(r, 0, s).wait()
                return carry
            lax.fori_loop(0, n, one, 0)

    @pl.when((i == 0) & (nv > 0))
    def _():
        gather_start(src_ref, slot)

    @pl.when(nv_next > 0)
    def _():
        gather_start(srcn_ref, other)

    @pl.when(nv_prev > 0)
    def _():
        scatter_wait(other, nv_prev)

    @pl.when(nv > 0)
    def _():
        prev_e = be_ref[jnp.maximum(i - 1, 0)]

        @pl.when((i == 0) | (prev_e != be_ref[i]))
        def _():
            w1b[...] = w1_ref[0].astype(BF16)
            w3b[...] = w3_ref[0].astype(BF16)
            w2b[...] = w2_ref[0].astype(BF16)

        gather_wait(slot)
        xb = xbuf[slot].astype(BF16)
        a = jnp.dot(xb, w1b[...], preferred_element_type=F32)
        b = jnp.dot(xb, w3b[...], preferred_element_type=F32)
        hm = (a * (1.0 / (1.0 + jnp.exp(-a))) * b).astype(BF16)
        ybuf[slot] = jnp.dot(hm, w2b[...], preferred_element_type=F32) * sw_ref[...]
        scatter_start(slot, nv)

        @pl.when(i == nb - 1)
        def _():
            scatter_wait(slot, nv)


def _moe(h2, block_e, n_valid, src, dst, sw, w1, w3, w2, n_out_rows, tb):
    nb = block_e.shape[0]
    src3 = src.reshape(nb, 1, tb)
    grid_spec = pltpu.PrefetchScalarGridSpec(
        num_scalar_prefetch=2,
        grid=(nb,),
        in_specs=[
            pl.BlockSpec((1, 1, tb), lambda i, be, nv: (i, 0, 0), memory_space=pltpu.SMEM),
            pl.BlockSpec((1, 1, tb), lambda i, be, nv: (jnp.minimum(i + 1, nb - 1), 0, 0),
                         memory_space=pltpu.SMEM),
            pl.BlockSpec((1, 1, tb), lambda i, be, nv: (i, 0, 0), memory_space=pltpu.SMEM),
            pl.BlockSpec((tb, 1), lambda i, be, nv: (i, 0)),
            pl.BlockSpec((1, D_MODEL, D_EXPERT), lambda i, be, nv: (be[i], 0, 0)),
            pl.BlockSpec((1, D_MODEL, D_EXPERT), lambda i, be, nv: (be[i], 0, 0)),
            pl.BlockSpec((1, D_EXPERT, D_MODEL), lambda i, be, nv: (be[i], 0, 0)),
            pl.BlockSpec(memory_space=pl.ANY),
        ],
        out_specs=pl.BlockSpec(memory_space=pl.ANY),
        scratch_shapes=[pltpu.VMEM((2, tb, D_MODEL), F32), pltpu.VMEM((2, tb, D_MODEL), F32),
                        pltpu.VMEM((D_MODEL, D_EXPERT), BF16), pltpu.VMEM((D_MODEL, D_EXPERT), BF16),
                        pltpu.VMEM((D_EXPERT, D_MODEL), BF16),
                        pltpu.SemaphoreType.DMA((2,)), pltpu.SemaphoreType.DMA((2,))],
    )
    return pl.pallas_call(
        functools.partial(_moe_body, tb=tb, nb=nb),
        grid_spec=grid_spec,
        out_shape=jax.ShapeDtypeStruct((n_out_rows, D_MODEL), F32),
        compiler_params=_cparams(("arbitrary",)),
        name="moe_experts",
    )(block_e, n_valid, src3, src3, dst.reshape(nb, 1, tb), sw.reshape(nb * tb, 1), w1, w3, w2, h2)


def _route(lg):
    n = lg.shape[0]
    gl = lg[:, :N_GROUPS]
    el = lg[:, N_GROUPS:N_GROUPS + N_EXPERTS].reshape(n, N_GROUPS, EXPERTS_PER_GROUP)
    g_idx = jnp.argmax(gl, axis=-1)
    g_gate = jnp.take_along_axis(jax.nn.softmax(gl, axis=-1), g_idx[:, None], axis=1)[:, 0]
    e_sel = jnp.take_along_axis(el, g_idx[:, None, None], axis=1)[:, 0, :]
    top_l, top_j = lax.top_k(e_sel, TOP_K)
    weights = g_gate[:, None] * jax.nn.softmax(top_l, axis=-1)
    expert = g_idx[:, None] * EXPERTS_PER_GROUP + top_j
    return expert.astype(jnp.int32), weights


def _dispatch(expert, weights, tb):
    n = expert.shape[0]
    a = n * TOP_K
    flat_e = expert.reshape(a)
    flat_w = weights.reshape(a)
    order = jnp.argsort(flat_e).astype(jnp.int32)
    counts = jnp.sum((flat_e[:, None] == jnp.arange(N_EXPERTS)[None, :]).astype(jnp.int32), axis=0)
    starts = jnp.cumsum(counts) - counts
    pcounts = (counts + tb - 1) // tb * tb
    pends = jnp.cumsum(pcounts)
    pstarts = pends - pcounts
    nb = -(-a // tb) + N_EXPERTS
    n_used = (pends[-1] // tb).astype(jnp.int32)
    blk = jnp.arange(nb, dtype=jnp.int32)
    block_e = jnp.minimum(jnp.searchsorted(pends, blk * tb, side="right"), N_EXPERTS - 1).astype(jnp.int32)
    last_e = block_e[jnp.maximum(n_used - 1, 0)]
    block_e = jnp.where(blk < n_used, block_e, last_e)
    rank0 = blk * tb - pstarts[block_e]
    n_valid = jnp.where(blk < n_used, jnp.clip(counts[block_e] - rank0, 0, tb), 0).astype(jnp.int32)
    r = jnp.arange(tb, dtype=jnp.int32)[None, :]
    valid = r < n_valid[:, None]
    pos = jnp.clip((starts[block_e] + rank0)[:, None] + r, 0, a - 1)
    aidx = order[pos]
    tok = aidx // TOP_K
    src = jnp.where(valid, tok, 0).reshape(nb * tb)
    dst = jnp.where(valid, (aidx % TOP_K) * n + tok, 0).reshape(nb * tb)
    sw = jnp.where(valid, flat_w[aidx], 0.0).reshape(nb * tb)
    return block_e, n_valid, src, dst, sw


def _combine_body(x_ref, y0_ref, y1_ref, g2_ref, o_ref):
    o_ref[0] = x_ref[0] + g2_ref[0] * (y0_ref[0] + y1_ref[0])


def _combine(x, y3, g2, row0, tm):
    bn, l, _ = x.shape
    nt = l // tm
    t0 = row0 // tm
    rowspec = pl.BlockSpec((1, tm, D_MODEL), lambda b, t: (b, t, 0))
    return pl.pallas_call(
        _combine_body,
        grid=(bn, nt),
        in_specs=[rowspec,
                  pl.BlockSpec((1, tm, D_MODEL), lambda b, t: (0, t0 + b * nt + t, 0)),
                  pl.BlockSpec((1, tm, D_MODEL), lambda b, t: (1, t0 + b * nt + t, 0)),
                  pl.BlockSpec((1, 1, D_MODEL), lambda b, t: (b, 0, 0))],
        out_specs=rowspec,
        out_shape=jax.ShapeDtypeStruct(x.shape, F32),
        compiler_params=_cparams(("arbitrary", "arbitrary")),
        name="moe_combine",
    )(x, y3, y3, g2)


_ROPE_PERM = np.concatenate([np.arange(8, 16), np.arange(0, 8), np.arange(24, 32), np.arange(16, 24)])
_ROPE_SIGN = np.concatenate([-np.ones(8), np.ones(8), -np.ones(8), np.ones(8)]).astype(np.float32)


def _rope_tables_t(l):
    half = C_ROPE // 2
    inv = ROPE_BASE ** (-jnp.arange(0, half, 2, dtype=F32) / half)
    pos = jnp.arange(l)
    ang_r = (pos // GRID_W).astype(F32)[None, :] * inv[:, None]
    ang_c = (pos % GRID_W).astype(F32)[None, :] * inv[:, None]
    cos_t = jnp.concatenate([jnp.cos(ang_r)] * 2 + [jnp.cos(ang_c)] * 2, axis=0)
    sin_t = jnp.concatenate([jnp.sin(ang_r)] * 2 + [jnp.sin(ang_c)] * 2, axis=0)
    return cos_t, sin_t


def _col(v):
    return v.astype(F32)[:, None]


def _layer_weights(i, p):
    w_in = p["w_in"][i]
    kr0 = IN_A + IN_B + C_Q_LORA + C_KV_LORA
    kr_cols = w_in[:, kr0:kr0 + C_ROPE]
    w_aug = jnp.concatenate([w_in, kr_cols[:, _ROPE_PERM] * _ROPE_SIGN], axis=1)

    wq = p["c_w_q_up"][i].reshape(C_Q_LORA, C_HEADS, C_QK)
    wq_n = wq[:, :, :C_NOPE].reshape(C_Q_LORA, C_HEADS * C_NOPE)
    wq_r = wq[:, :, C_NOPE:]
    wq_rp = (wq_r[:, :, _ROPE_PERM] * _ROPE_SIGN).reshape(C_Q_LORA, C_HEADS * C_ROPE)
    wq_all = jnp.concatenate([wq_n, wq_r.reshape(C_Q_LORA, C_HEADS * C_ROPE), wq_rp], axis=1)

    wkv = p["c_w_kv_up"][i].reshape(C_KV_LORA, C_HEADS, C_NOPE + C_VDIM)
    wkv_all = jnp.concatenate([wkv[:, :, :C_NOPE].reshape(C_KV_LORA, C_HEADS * C_NOPE),
                               wkv[:, :, C_NOPE:].reshape(C_KV_LORA, C_HEADS * C_VDIM)], axis=1)

    cqn = p["c_q_norm"][i].astype(F32) * (C_QK ** -0.5)
    ckn = p["c_k_norm"][i].astype(F32)
    w_route = jnp.concatenate([p["moe_w_group"][i], p["moe_w_router"][i],
                               jnp.zeros((D_MODEL, ROUTER_PAD - N_GROUPS - N_EXPERTS), F32)], axis=1)
    return {
        "norm_mix": p["norm_mix"][i][None, :],
        "w_in_t": w_aug.T.astype(BF16),
        "avn": _col(p["a_v_norm"][i]),
        "wst": jnp.transpose(p["a_w_s"][i], (0, 2, 1)).astype(BF16),
        "bs": p["a_b_s"][i][:, None, :],
        "gq": _col(p["b_q_norm"][i] * (HEAD_DIM ** -0.5)),
        "gk": _col(p["b_k_norm"][i]),
        "cqa": _col(p["c_q_a_norm"][i]),
        "wq_t": wq_all.T.astype(BF16),
        "ckva": _col(p["c_kv_a_norm"][i]),
        "wkv_t": wkv_all.T.astype(BF16),
        "gcqn": _col(cqn[:C_NOPE]),
        "gcqr": _col(cqn[C_NOPE:]),
        "gcqrp": _col(cqn[C_NOPE:][_ROPE_PERM]),
        "gckn": _col(ckn[:C_NOPE]),
        "gckr": _col(ckn[C_NOPE:]),
        "gckrp": _col(ckn[C_NOPE:][_ROPE_PERM]),
        "w_out_t": p["w_out"][i].T.astype(BF16),
        "norm_ffn": p["norm_ffn"][i][None, :],
        "w_route": w_route,
    }


def _pick_tile(l, pref):
    t = min(l, pref)
    while l % t:
        t //= 2
    return t


def kernel(x, c, ctx, c_ctx, w_ada, b_ada, norm_mix, w_in, a_v_norm, a_w_s, a_b_s, b_q_norm, b_k_norm, b_rpb,
           c_q_a_norm, c_w_q_up, c_kv_a_norm, c_w_kv_up, c_q_norm, c_k_norm, w_out, norm_ffn,
           moe_w_group, moe_w_router, moe_w1, moe_w3, moe_w2):
    p = dict(norm_mix=norm_mix, w_in=w_in, a_v_norm=a_v_norm, a_w_s=a_w_s, a_b_s=a_b_s, b_q_norm=b_q_norm,
             b_k_norm=b_k_norm, c_q_a_norm=c_q_a_norm, c_w_q_up=c_w_q_up, c_kv_a_norm=c_kv_a_norm,
             c_w_kv_up=c_w_kv_up, c_q_norm=c_q_norm, c_k_norm=c_k_norm, w_out=w_out, norm_ffn=norm_ffn,
             moe_w_group=moe_w_group, moe_w_router=moe_w_router)
    bn, l, d = x.shape
    nc = ctx.shape[1]
    depth = w_ada.shape[0]
    rows = l // GRID_W
    tm = _pick_tile(l, 512)
    tq = _pick_tile(l, 512)
    tk = _pick_tile(l, 512)

    cos_t, sin_t = _rope_tables_t(l)
    cos_x = jnp.ones((C_ROPE, nc), F32)
    sin_x = jnp.zeros((C_ROPE, nc), F32)
    cc8 = jnp.concatenate([c, c_ctx[None, :], jnp.zeros((8 - bn - 1, d), F32)], axis=0)

    xc = ctx
    for i in range(depth):
        need_ctx = i < depth - 1
        lw = _layer_weights(i, p)
        mod = _ada(cc8, w_ada[i], b_ada[i][None, :])
        mods = [mod[:bn, j * d:(j + 1) * d][:, None, :] for j in range(6)]
        modx = [jnp.broadcast_to(mod[bn, j * d:(j + 1) * d][None, None, :], (bn, 1, d)) for j in range(6)]
        sh1, s1, g1, sh2, s2, g2 = mods
        sh1x, s1x, g1x, sh2x, s2x, g2x = modx

        oa, qb, kb, vb, qc, kc, vc = _mixin(x, sh1, s1, lw, cos_t, sin_t, tm)
        oax, qbx, kbx, vbx, qcx, kcx, vcx = _mixin(xc, sh1x, s1x, lw, cos_x, sin_x, nc)

        table = _na_bias_table(b_rpb[i], rows)
        ob = _na_attention(qb, kb, vb, kbx, vbx, table)
        oc = _mla_attention(qc, kc, vc, kcx, vcx, tq)
        x, h2, lg = _mixout(x, oa, ob, oc, g1, sh2, s2, lw, tm)

        n_lat = bn * l
        if need_ctx:
            obx = _flash(qbx, kbx[:, None], vbx,
                         pl.BlockSpec((1, 1, nc, LANE), lambda b, h, ii, j: (b, 0, j, h // 2)),
                         tq=nc, tk=nc, name="ctx_dense_attention")
            ocx = _flash(qcx, kcx, vcx.reshape(bn, C_WIDTH, nc),
                         pl.BlockSpec((1, 1, nc, QPAD), lambda b, h, ii, j: (b, h, j, 0)),
                         tq=nc, tk=nc, name="ctx_mla_attention")
            xc, h2x, lgx = _mixout(xc, oax, obx, ocx, g1x, sh2x, s2x, lw, nc)
            h2_all = jnp.concatenate([h2.reshape(n_lat, d), h2x.reshape(bn * nc, d)], axis=0)
            lg_all = jnp.concatenate([lg.reshape(n_lat, ROUTER_PAD), lgx.reshape(bn * nc, ROUTER_PAD)], axis=0)
        else:
            h2_all = h2.reshape(n_lat, d)
            lg_all = lg.reshape(n_lat, ROUTER_PAD)

        n_tok = h2_all.shape[0]
        expert, weights = _route(lg_all)
        block_e, n_valid, src, dst, sw = _dispatch(expert, weights, MOE_BLOCK)
        y = _moe(h2_all, block_e, n_valid, src, dst, sw, moe_w1[i], moe_w3[i], moe_w2[i],
                 TOP_K * n_tok, MOE_BLOCK)
        y3 = y.reshape(TOP_K, n_tok, d)
        x = _combine(x, y3, g2, 0, tm)
        if need_ctx:
            xc = _combine(xc, y3, g2x, n_lat, nc)
    return x
```

```python
import functools

import numpy as np
import jax
import jax.numpy as jnp
from jax import lax
from jax.experimental import pallas as pl
from jax.experimental.pallas import tpu as pltpu

F32 = jnp.float32
BF16 = jnp.bfloat16

D_MODEL = 1024
GRID_W = 64
HEAD_DIM = 64
EPS = 1e-6
NEG_INF = -1e30

A_HEADS = 4
A_WIDTH = A_HEADS * HEAD_DIM
CHUNK = 128
B_HEADS = 6
B_WIDTH = B_HEADS * HEAD_DIM
NA_ROWS = 8
NA_COLS = 16
C_HEADS = 6
C_NOPE = 64
C_ROPE = 32
C_QK = C_NOPE + C_ROPE
C_VDIM = 64
C_Q_LORA = 384
C_KV_LORA = 256
C_WIDTH = C_HEADS * C_VDIM
ROPE_BASE = 10000.0
IN_A = 2 * A_WIDTH
IN_B = 3 * B_WIDTH
IN_C = C_Q_LORA + C_KV_LORA + C_ROPE
IN_AUG = IN_A + IN_B + IN_C + C_ROPE

N_GROUPS = 8
EXPERTS_PER_GROUP = 8
N_EXPERTS = N_GROUPS * EXPERTS_PER_GROUP
TOP_K = 2
D_EXPERT = D_MODEL // 2
ROUTER_PAD = 128

V7X_VMEM_LIMIT_BYTES = 56 * 1024 * 1024
LANE = 128
QPAD = 128

MLA_KEY_CHUNK = 256
MLA_QUERY_TILE = 1024
NA_QROWS = 8
NA_KBLK_ROWS = 4
MOE_BLOCK = 256

HIGHEST = lax.Precision.HIGHEST


def _cparams(sem):
    return pltpu.CompilerParams(dimension_semantics=sem, vmem_limit_bytes=V7X_VMEM_LIMIT_BYTES)


def _ada_body(c_ref, w_ref, b_ref, o_ref):
    cc = c_ref[...]
    s = cc * (1.0 / (1.0 + jnp.exp(-cc)))
    o_ref[...] = jnp.dot(s, w_ref[...], preferred_element_type=F32, precision=HIGHEST) + b_ref[...]


def _ada(cc8, w, b):
    n_out = w.shape[1]
    tn = 1024
    return pl.pallas_call(
        _ada_body,
        grid=(n_out // tn,),
        in_specs=[pl.BlockSpec((8, D_MODEL), lambda j: (0, 0)),
                  pl.BlockSpec((D_MODEL, tn), lambda j: (0, j)),
                  pl.BlockSpec((1, tn), lambda j: (0, j))],
        out_specs=pl.BlockSpec((8, tn), lambda j: (0, j)),
        out_shape=jax.ShapeDtypeStruct((8, n_out), F32),
        compiler_params=_cparams(("arbitrary",)),
        name="ada_mod",
    )(cc8, w, b)


def _rms_rows(v):
    return lax.rsqrt(jnp.mean(v * v, axis=0, keepdims=True) + EPS)


def _gelu_tanh(x):
    return 0.5 * x * (1.0 + jnp.tanh(0.7978845608028654 * (x + 0.044715 * (x * x * x))))


def _mixin_body(x_ref, sh_ref, sc_ref, gn_ref, wt_ref, avn_ref, wst_ref, bs_ref, gq_ref, gk_ref,
                cqa_ref, wq_ref, ckva_ref, wkv_ref, gcqn_ref, gcqr_ref, gcqrp_ref, gckn_ref, gckr_ref,
                gckrp_ref, cos_ref, sin_ref,
                oa_ref, qb_ref, kb_ref, vb_ref, qc_ref, kc_ref, vc_ref, *, tm):
    x = x_ref[0]
    h = x * lax.rsqrt(jnp.mean(x * x, axis=-1, keepdims=True) + EPS) * gn_ref[...]
    h = h * (1.0 + sc_ref[0]) + sh_ref[0]
    zt = lax.dot_general(wt_ref[...], h.astype(BF16), (((1,), (1,)), ((), ())),
                         preferred_element_type=F32)

    ga = _gelu_tanh(zt[0:IN_A])
    u = ga[0:A_WIDTH]
    v = ga[A_WIDTH:IN_A]
    vn = (v * _rms_rows(v) * avn_ref[...]).astype(BF16)
    for hd in range(A_HEADS):
        r0, r1 = hd * HEAD_DIM, (hd + 1) * HEAD_DIM
        for c in range(tm // CHUNK):
            c0, c1 = c * CHUNK, (c + 1) * CHUNK
            vm = jnp.dot(vn[r0:r1, c0:c1], wst_ref[hd], preferred_element_type=F32) + bs_ref[hd]
            oa_ref[0, r0:r1, c0:c1] = (u[r0:r1, c0:c1] * vm).astype(BF16)

    zb = zt[IN_A:IN_A + IN_B]
    zeros64 = jnp.zeros((HEAD_DIM, tm), F32)
    kn = []
    for hd in range(B_HEADS):
        q = zb[hd * HEAD_DIM:(hd + 1) * HEAD_DIM]
        k = zb[B_WIDTH + hd * HEAD_DIM:B_WIDTH + (hd + 1) * HEAD_DIM]
        qn = q * _rms_rows(q) * gq_ref[...]
        kn.append(k * _rms_rows(k) * gk_ref[...])
        pair = [qn, zeros64] if hd % 2 == 0 else [zeros64, qn]
        qb_ref[0, hd] = jnp.concatenate(pair, axis=0).astype(BF16)
    for p in range(B_HEADS // 2):
        kt = jnp.concatenate([kn[2 * p], kn[2 * p + 1]], axis=0)
        kb_ref[0, :, p * LANE:(p + 1) * LANE] = kt.T.astype(BF16)
    vb_ref[0] = zb[2 * B_WIDTH:3 * B_WIDTH].astype(BF16)

    zc = zt[IN_A + IN_B:IN_AUG]
    ql = zc[0:C_Q_LORA]
    kvl = zc[C_Q_LORA:C_Q_LORA + C_KV_LORA]
    kr = zc[C_Q_LORA + C_KV_LORA:C_Q_LORA + C_KV_LORA + C_ROPE]
    krp = zc[C_Q_LORA + C_KV_LORA + C_ROPE:C_Q_LORA + C_KV_LORA + 2 * C_ROPE]
    qln = (ql * _rms_rows(ql) * cqa_ref[...]).astype(BF16)
    kvln = (kvl * _rms_rows(kvl) * ckva_ref[...]).astype(BF16)
    qt = jnp.dot(wq_ref[...], qln, preferred_element_type=F32)
    kvt = jnp.dot(wkv_ref[...], kvln, preferred_element_type=F32)
    cos = cos_ref[...]
    sin = sin_ref[...]
    krn = _rms_rows(kr) * (gckr_ref[...] * kr * cos + gckrp_ref[...] * krp * sin)
    zeros32 = jnp.zeros((QPAD - C_QK, tm), F32)
    nq = C_HEADS * C_NOPE
    nr = C_HEADS * C_ROPE
    for hd in range(C_HEADS):
        qn = qt[hd * C_NOPE:(hd + 1) * C_NOPE]
        qn = qn * _rms_rows(qn) * gcqn_ref[...]
        qr = qt[nq + hd * C_ROPE:nq + (hd + 1) * C_ROPE]
        qrp = qt[nq + nr + hd * C_ROPE:nq + nr + (hd + 1) * C_ROPE]
        qrn = _rms_rows(qr) * (gcqr_ref[...] * qr * cos + gcqrp_ref[...] * qrp * sin)
        qc_ref[0, hd] = jnp.concatenate([qn, qrn, zeros32], axis=0).astype(BF16)
        kn_c = kvt[hd * C_NOPE:(hd + 1) * C_NOPE]
        kn_c = kn_c * _rms_rows(kn_c) * gckn_ref[...]
        kc_ref[0, hd] = jnp.concatenate([kn_c, krn, zeros32], axis=0).T.astype(BF16)
    vt = vc_ref.shape[3]
    v_rows = kvt[nq:nq + C_WIDTH].astype(BF16)
    for c in range(tm // vt):
        vc_ref[0, c] = v_rows[:, c * vt:(c + 1) * vt]


def _mixin(x, sh, sc, lw, cos_t, sin_t, tm):
    bn, l, _ = x.shape
    vt = min(MLA_KEY_CHUNK, tm)
    const2 = lambda b, t: (0, 0)
    const3 = lambda b, t: (0, 0, 0)
    in_specs = [
        pl.BlockSpec((1, tm, D_MODEL), lambda b, t: (b, t, 0)),
        pl.BlockSpec((1, 1, D_MODEL), lambda b, t: (b, 0, 0)),
        pl.BlockSpec((1, 1, D_MODEL), lambda b, t: (b, 0, 0)),
        pl.BlockSpec((1, D_MODEL), const2),
        pl.BlockSpec((IN_AUG, D_MODEL), const2),
        pl.BlockSpec((A_WIDTH, 1), const2),
        pl.BlockSpec((A_HEADS, CHUNK, CHUNK), const3),
        pl.BlockSpec((A_HEADS, 1, CHUNK), const3),
        pl.BlockSpec((HEAD_DIM, 1), const2),
        pl.BlockSpec((HEAD_DIM, 1), const2),
        pl.BlockSpec((C_Q_LORA, 1), const2),
        pl.BlockSpec((C_HEADS * (C_NOPE + 2 * C_ROPE), C_Q_LORA), const2),
        pl.BlockSpec((C_KV_LORA, 1), const2),
        pl.BlockSpec((C_HEADS * (C_NOPE + C_VDIM), C_KV_LORA), const2),
        pl.BlockSpec((C_NOPE, 1), const2),
        pl.BlockSpec((C_ROPE, 1), const2),
        pl.BlockSpec((C_ROPE, 1), const2),
        pl.BlockSpec((C_NOPE, 1), const2),
        pl.BlockSpec((C_ROPE, 1), const2),
        pl.BlockSpec((C_ROPE, 1), const2),
        pl.BlockSpec((C_ROPE, tm), lambda b, t: (0, t)),
        pl.BlockSpec((C_ROPE, tm), lambda b, t: (0, t)),
    ]
    out_shape = (
        jax.ShapeDtypeStruct((bn, A_WIDTH, l), BF16),
        jax.ShapeDtypeStruct((bn, B_HEADS, QPAD, l), BF16),
        jax.ShapeDtypeStruct((bn, l, B_WIDTH), BF16),
        jax.ShapeDtypeStruct((bn, B_WIDTH, l), BF16),
        jax.ShapeDtypeStruct((bn, C_HEADS, QPAD, l), BF16),
        jax.ShapeDtypeStruct((bn, C_HEADS, l, QPAD), BF16),
        jax.ShapeDtypeStruct((bn, l // vt, C_WIDTH, vt), BF16),
    )
    out_specs = (
        pl.BlockSpec((1, A_WIDTH, tm), lambda b, t: (b, 0, t)),
        pl.BlockSpec((1, B_HEADS, QPAD, tm), lambda b, t: (b, 0, 0, t)),
        pl.BlockSpec((1, tm, B_WIDTH), lambda b, t: (b, t, 0)),
        pl.BlockSpec((1, B_WIDTH, tm), lambda b, t: (b, 0, t)),
        pl.BlockSpec((1, C_HEADS, QPAD, tm), lambda b, t: (b, 0, 0, t)),
        pl.BlockSpec((1, C_HEADS, tm, QPAD), lambda b, t: (b, 0, t, 0)),
        pl.BlockSpec((1, tm // vt, C_WIDTH, vt), lambda b, t: (b, t, 0, 0)),
    )
    return pl.pallas_call(
        functools.partial(_mixin_body, tm=tm),
        grid=(bn, l // tm),
        in_specs=in_specs,
        out_specs=out_specs,
        out_shape=out_shape,
        compiler_params=_cparams(("arbitrary", "arbitrary")),
        name="mix_in",
    )(x, sh, sc, lw["norm_mix"], lw["w_in_t"], lw["avn"], lw["wst"], lw["bs"], lw["gq"], lw["gk"],
      lw["cqa"], lw["wq_t"], lw["ckva"], lw["wkv_t"], lw["gcqn"], lw["gcqr"], lw["gcqrp"], lw["gckn"],
      lw["gckr"], lw["gckrp"], cos_t, sin_t)


def _na_body(q_ref, k0_ref, k1_ref, k2_ref, k3_ref, v0_ref, v1_ref, v2_ref, v3_ref, kx_ref, vx_ref, t_ref,
             o_ref):
    q = q_ref[0, 0]
    k_refs = (k0_ref, k1_ref, k2_ref, k3_ref)
    v_refs = (v0_ref, v1_ref, v2_ref, v3_ref)
    s = jnp.concatenate([jnp.dot(kr[0], q, preferred_element_type=F32) for kr in k_refs], axis=0)
    s = s + t_ref[0, 0]
    sx = jnp.dot(kx_ref[0], q, preferred_element_type=F32)
    m = jnp.maximum(jnp.max(s, axis=0, keepdims=True), jnp.max(sx, axis=0, keepdims=True))
    p = jnp.exp(s - m)
    px = jnp.exp(sx - m)
    den = jnp.sum(p, axis=0, keepdims=True) + jnp.sum(px, axis=0, keepdims=True)
    pb = p.astype(BF16)
    nk = k0_ref.shape[1]
    o = jnp.dot(vx_ref[0], px.astype(BF16), preferred_element_type=F32)
    for j, vr in enumerate(v_refs):
        o = o + jnp.dot(vr[0], pb[j * nk:(j + 1) * nk], preferred_element_type=F32)
    o_ref[0] = (o / den).astype(BF16)


def _na_attention(qb, kb, vb, kxb, vxb, table):
    bn, _, _, l = qb.shape
    nq = NA_QROWS * GRID_W
    nk = NA_KBLK_ROWS * GRID_W
    nblk = l // nq
    nkb = l // nk
    nctx = kxb.shape[1]

    def kmap(j):
        return lambda b, h, i: (b, jnp.clip(2 * i - 1 + j, 0, nkb - 1), h // 2)

    def vmap_(j):
        return lambda b, h, i: (b, h, jnp.clip(2 * i - 1 + j, 0, nkb - 1))

    def tmap(b, h, i):
        return (h, jnp.where(i == 0, 0, jnp.where(i == nblk - 1, 2, 1)), 0, 0)

    in_specs = ([pl.BlockSpec((1, 1, QPAD, nq), lambda b, h, i: (b, h, 0, i))]
                + [pl.BlockSpec((1, nk, LANE), kmap(j)) for j in range(4)]
                + [pl.BlockSpec((1, HEAD_DIM, nk), vmap_(j)) for j in range(4)]
                + [pl.BlockSpec((1, nctx, LANE), lambda b, h, i: (b, 0, h // 2)),
                   pl.BlockSpec((1, HEAD_DIM, nctx), lambda b, h, i: (b, h, 0)),
                   pl.BlockSpec((1, 1, 4 * nk, nq), tmap)])
    return pl.pallas_call(
        _na_body,
        grid=(bn, B_HEADS, nblk),
        in_specs=in_specs,
        out_specs=pl.BlockSpec((1, HEAD_DIM, nq), lambda b, h, i: (b, h, i)),
        out_shape=jax.ShapeDtypeStruct((bn, B_WIDTH, l), BF16),
        compiler_params=_cparams(("arbitrary", "arbitrary", "arbitrary")),
        name="na_attention",
    )(qb, kb, kb, kb, kb, vb, vb, vb, vb, kxb, vxb, table)


def _na_bias_table(rpb, rows):
    nblk = rows // NA_QROWS
    qc = np.arange(GRID_W)
    kc = np.arange(GRID_W)
    c0 = np.clip(qc - NA_COLS // 2, 0, GRID_W - NA_COLS)
    valid_col = (kc[:, None] >= c0[None, :]) & (kc[:, None] < c0[None, :] + NA_COLS)
    dc = np.clip(kc[:, None] - qc[None, :], -(NA_COLS - 1), NA_COLS - 1) + NA_COLS - 1
    dc_onehot = (dc[None] == np.arange(2 * NA_COLS - 1)[:, None, None]).astype(np.float32)
    tabs = []
    for i in (0, 1, nblk - 1):
        kr = NA_KBLK_ROWS * (2 * i - 1) + np.arange(4 * NA_KBLK_ROWS)
        qr = NA_QROWS * i + np.arange(NA_QROWS)
        r0 = np.clip(qr - NA_ROWS // 2, 0, rows - NA_ROWS)
        valid_row = ((kr[:, None] >= r0[None, :]) & (kr[:, None] < r0[None, :] + NA_ROWS)
                     & (kr[:, None] >= 0) & (kr[:, None] < rows))
        dr = np.clip(kr[:, None] - qr[None, :] + NA_ROWS - 1, 0, 2 * NA_ROWS - 2)
        bias = jnp.einsum("hkqd,dcx->hkcqx", rpb[:, dr].astype(F32), dc_onehot, precision=HIGHEST)
        valid = valid_row[:, None, :, None] & valid_col[None, :, None, :]
        tabs.append(jnp.where(valid[None], bias.astype(F32), NEG_INF).reshape(
            rpb.shape[0], 4 * NA_KBLK_ROWS * GRID_W, NA_QROWS * GRID_W))
    return jnp.stack(tabs, axis=1)


def _flash_body(*refs, has_extra):
    if has_extra:
        q_ref, k_ref, v_ref, kx_ref, vx_ref, o_ref, m_sc, l_sc, acc_sc = refs
    else:
        q_ref, k_ref, v_ref, o_ref, m_sc, l_sc, acc_sc = refs
    kv = pl.program_id(3)
    nkv = pl.num_programs(3)

    @pl.when(kv == 0)
    def _():
        m_sc[...] = jnp.full(m_sc.shape, -jnp.inf, F32)
        l_sc[...] = jnp.zeros(l_sc.shape, F32)
        acc_sc[...] = jnp.zeros(acc_sc.shape, F32)

    def step(k, v):
        s = jnp.dot(k, q_ref[0, 0], preferred_element_type=F32)
        m_old = m_sc[...]
        m_new = jnp.maximum(m_old, jnp.max(s, axis=0, keepdims=True))
        alpha = jnp.exp(m_old - m_new)
        p = jnp.exp(s - m_new)
        l_sc[...] = alpha * l_sc[...] + jnp.sum(p, axis=0, keepdims=True)
        acc_sc[...] = alpha * acc_sc[...] + jnp.dot(v, p.astype(BF16), preferred_element_type=F32)
        m_sc[...] = m_new

    step(k_ref[0, 0], v_ref[0])

    @pl.when(kv == nkv - 1)
    def _():
        if has_extra:
            step(kx_ref[0, 0], vx_ref[0])
        o_ref[0] = (acc_sc[...] / l_sc[...]).astype(BF16)


def _flash(q, k, v, k_spec, kx=None, vx=None, kx_spec=None, *, tq, tk, name):
    bn, nh, _, lq = q.shape
    lk = v.shape[2]
    has_extra = kx is not None
    in_specs = [pl.BlockSpec((1, 1, QPAD, tq), lambda b, h, i, j: (b, h, 0, i)),
                k_spec,
                pl.BlockSpec((1, HEAD_DIM, tk), lambda b, h, i, j: (b, h, j))]
    args = [q, k, v]
    if has_extra:
        nx = vx.shape[2]
        in_specs += [kx_spec, pl.BlockSpec((1, HEAD_DIM, nx), lambda b, h, i, j: (b, h, 0))]
        args += [kx, vx]
    return pl.pallas_call(
        functools.partial(_flash_body, has_extra=has_extra),
        grid=(bn, nh, lq // tq, lk // tk),
        in_specs=in_specs,
        out_specs=pl.BlockSpec((1, HEAD_DIM, tq), lambda b, h, i, j: (b, h, i)),
        out_shape=jax.ShapeDtypeStruct((bn, nh * HEAD_DIM, lq), BF16),
        scratch_shapes=[pltpu.VMEM((1, tq), F32), pltpu.VMEM((1, tq), F32), pltpu.VMEM((HEAD_DIM, tq), F32)],
        compiler_params=_cparams(("arbitrary", "arbitrary", "arbitrary", "arbitrary")),
        name=name,
    )(*args)


ONES_ROWS = 16


def _mla_body(q_ref, k_ref, v_ref, kx_ref, vx_ref, o_ref, *, tk, nchunks):
    q = q_ref[0, 0]
    tq = q.shape[1]

    def scores(k):
        return jnp.dot(k, q, preferred_element_type=F32).astype(BF16)

    def absorb(s, v, carry):
        m, acc = carry
        m_new = jnp.maximum(m, jnp.max(s, axis=0, keepdims=True).astype(F32))
        alpha = jnp.exp(m - m_new)
        p = jnp.exp(s - m_new.astype(BF16))
        v_ext = jnp.concatenate([v, jnp.ones((ONES_ROWS, v.shape[1]), v.dtype)], axis=0)
        acc = alpha * acc + jnp.dot(v_ext, p, preferred_element_type=F32)
        return m_new, acc

    carry = (jnp.full((1, tq), -jnp.inf, F32), jnp.zeros((HEAD_DIM + ONES_ROWS, tq), F32))
    s_cur = scores(k_ref[0, 0, 0:tk, :])
    for j in range(nchunks):
        if j + 1 < nchunks:
            s_next = scores(k_ref[0, 0, (j + 1) * tk:(j + 2) * tk, :])
        else:
            s_next = scores(kx_ref[0, 0])
        carry = absorb(s_cur, v_ref[0, j], carry)
        s_cur = s_next
    _, acc = absorb(s_cur, vx_ref[0, 0], carry)
    o_ref[0] = (acc[0:HEAD_DIM] / acc[HEAD_DIM:HEAD_DIM + 1]).astype(BF16)


def _mla_attention(q, k, v, kx, vx, tq):
    bn, nh, _, l = q.shape
    nchunks, tk = v.shape[1], v.shape[3]
    nc = kx.shape[2]
    return pl.pallas_call(
        functools.partial(_mla_body, tk=tk, nchunks=nchunks),
        grid=(bn, nh, l // tq),
        in_specs=[pl.BlockSpec((1, 1, QPAD, tq), lambda b, h, i: (b, h, 0, i)),
                  pl.BlockSpec((1, 1, l, QPAD), lambda b, h, i: (b, h, 0, 0)),
                  pl.BlockSpec((1, nchunks, HEAD_DIM, tk), lambda b, h, i: (b, 0, h, 0)),
                  pl.BlockSpec((1, 1, nc, QPAD), lambda b, h, i: (b, h, 0, 0)),
                  pl.BlockSpec((1, 1, HEAD_DIM, nc), lambda b, h, i: (b, 0, h, 0))],
        out_specs=pl.BlockSpec((1, HEAD_DIM, tq), lambda b, h, i: (b, h, i)),
        out_shape=jax.ShapeDtypeStruct((bn, nh * HEAD_DIM, l), BF16),
        compiler_params=_cparams(("arbitrary", "arbitrary", "arbitrary")),
        name="mla_attention",
    )(q, k, v, kx, vx)


def _mixout_body(x_ref, oa_ref, ob_ref, oc_ref, wt_ref, g1_ref, sh2_ref, sc2_ref, gn2_ref, wr_ref, *rest):
    xo_ref, h2_ref, lg_ref = rest[-3:]
    ot = jnp.concatenate([oa_ref[0], ob_ref[0], oc_ref[0]], axis=0)
    out_t = jnp.dot(wt_ref[...], ot, preferred_element_type=F32)
    xn = x_ref[0] + g1_ref[0] * out_t.T
    xo_ref[0] = xn
    h2 = xn * lax.rsqrt(jnp.mean(xn * xn, axis=-1, keepdims=True) + EPS) * gn2_ref[...]
    h2 = h2 * (1.0 + sc2_ref[0]) + sh2_ref[0]
    h2_ref[...] = h2
    lg_ref[...] = jnp.dot(h2, wr_ref[...], preferred_element_type=F32, precision=HIGHEST)


def _mixout(x, oa, ob, oc, g1, sh2, sc2, lw, tm, n_rows, row0=0, bufs=None):
    bn, l, _ = x.shape
    nt = l // tm
    steps = bn * nt
    extra = 0
    if bufs is None and n_rows > bn * l:
        assert n_rows - bn * l == tm, "spare rows must be exactly one tile"
        extra = 1
    blk0 = row0 // tm

    def bt(i):
        ii = jnp.minimum(i, steps - 1)
        return ii // nt, ii % nt

    const2 = lambda i: (0, 0)
    modspec = pl.BlockSpec((1, 1, D_MODEL), lambda i: (bt(i)[0], 0, 0))
    rowspec = pl.BlockSpec((1, tm, D_MODEL), lambda i: (bt(i)[0], bt(i)[1], 0))
    in_specs = [rowspec,
                pl.BlockSpec((1, A_WIDTH, tm), lambda i: (bt(i)[0], 0, bt(i)[1])),
                pl.BlockSpec((1, B_WIDTH, tm), lambda i: (bt(i)[0], 0, bt(i)[1])),
                pl.BlockSpec((1, C_WIDTH, tm), lambda i: (bt(i)[0], 0, bt(i)[1])),
                pl.BlockSpec((D_MODEL, D_MODEL), const2),
                modspec, modspec, modspec,
                pl.BlockSpec((1, D_MODEL), const2),
                pl.BlockSpec((D_MODEL, ROUTER_PAD), const2)]
    args = [x, oa, ob, oc, lw["w_out_t"], g1, sh2, sc2, lw["norm_ffn"], lw["w_route"]]
    aliases = {}
    if bufs is not None:
        aliases = {len(args): 1, len(args) + 1: 2}
        in_specs += [pl.BlockSpec(memory_space=pl.ANY), pl.BlockSpec(memory_space=pl.ANY)]
        args += list(bufs)
    return pl.pallas_call(
        _mixout_body,
        grid=(steps + extra,),
        in_specs=in_specs,
        out_specs=(rowspec,
                   pl.BlockSpec((tm, D_MODEL), lambda i: (blk0 + i, 0)),
                   pl.BlockSpec((tm, ROUTER_PAD), lambda i: (blk0 + i, 0))),
        out_shape=(jax.ShapeDtypeStruct((bn, l, D_MODEL), F32),
                   jax.ShapeDtypeStruct((n_rows, D_MODEL), F32),
                   jax.ShapeDtypeStruct((n_rows, ROUTER_PAD), F32)),
        input_output_aliases=aliases,
        compiler_params=_cparams(("arbitrary",)),
        name="mix_out",
    )(*args)


def _moe_body(be_ref, nv_ref, src_ref, srcn_ref, dst_ref, sw_ref, w1_ref, w3_ref, w2_ref, h_hbm, y_hbm,
              xbuf, ybuf, w1b, w3b, w2b, gsem, ssem, *, tb, nb):
    i = pl.program_id(0)
    slot = lax.rem(i, 2)
    other = 1 - slot
    nv = nv_ref[i]
    nv_prev = jnp.where(i > 0, nv_ref[jnp.maximum(i - 1, 0)], 0)
    nv_next = jnp.where(i + 1 < nb, nv_ref[jnp.minimum(i + 1, nb - 1)], 0)

    def gather_start(idx_ref, s):
        def one(r, carry):
            pltpu.make_async_copy(h_hbm.at[pl.ds(idx_ref[0, 0, r], 1)], xbuf.at[s, pl.ds(r, 1)],
                                  gsem.at[s]).start()
            return carry
        lax.fori_loop(0, tb, one, 0, unroll=8)

    def gather_wait(s):
        pltpu.make_async_copy(h_hbm.at[pl.ds(0, tb)], xbuf.at[s], gsem.at[s]).wait()

    def scatter_row(r, row, s):
        return pltpu.make_async_copy(ybuf.at[s, pl.ds(r, 1)], y_hbm.at[pl.ds(row, 1)], ssem.at[s])

    def scatter_start(s, n):
        @pl.when(n == tb)
        def _():
            def one(r, carry):
                scatter_row(r, dst_ref[0, 0, r], s).start()
                return carry
            lax.fori_loop(0, tb, one, 0, unroll=8)

        @pl.when(n < tb)
        def _():
            def one(r, carry):
                scatter_row(r, dst_ref[0, 0, r], s).start()
                return carry
            lax.fori_loop(0, n, one, 0)

    def scatter_wait(s, n):
        @pl.when(n == tb)
        def _():
            pltpu.make_async_copy(ybuf.at[s], y_hbm.at[pl.ds(0, tb)], ssem.at[s]).wait()

        @pl.when(n < tb)
        def _():
            def one(r, carry):
                scatter_row(r, 0, s).wait()
                return carry
            lax.fori_loop(0, n, one, 0)

    @pl.when((i == 0) & (nv > 0))
    def _():
        gather_start(src_ref, slot)

    @pl.when(nv_next > 0)
    def _():
        gather_start(srcn_ref, other)

    @pl.when(nv_prev > 0)
    def _():
        scatter_wait(other, nv_prev)

    @pl.when(nv > 0)
    def _():
        prev_e = be_ref[jnp.maximum(i - 1, 0)]

        @pl.when((i == 0) | (prev_e != be_ref[i]))
        def _():
            w1b[...] = w1_ref[0].astype(BF16)
            w3b[...] = w3_ref[0].astype(BF16)
            w2b[...] = w2_ref[0].astype(BF16)

        gather_wait(slot)
        xb = xbuf[slot].astype(BF16)
        a = jnp.dot(xb, w1b[...], preferred_element_type=F32)
        b = jnp.dot(xb, w3b[...], preferred_element_type=F32)
        hm = (a * (1.0 / (1.0 + jnp.exp(-a))) * b).astype(BF16)
        ybuf[slot] = jnp.dot(hm, w2b[...], preferred_element_type=F32) * sw_ref[...]
        scatter_start(slot, nv)

        @pl.when(i == nb - 1)
        def _():
            scatter_wait(slot, nv)


def _moe(h2, block_e, n_valid, src, dst, sw, w1, w3, w2, n_out_rows, tb):
    nb = block_e.shape[0]
    src3 = src.reshape(nb, 1, tb)
    grid_spec = pltpu.PrefetchScalarGridSpec(
        num_scalar_prefetch=2,
        grid=(nb,),
        in_specs=[
            pl.BlockSpec((1, 1, tb), lambda i, be, nv: (i, 0, 0), memory_space=pltpu.SMEM),
            pl.BlockSpec((1, 1, tb), lambda i, be, nv: (jnp.minimum(i + 1, nb - 1), 0, 0),
                         memory_space=pltpu.SMEM),
            pl.BlockSpec((1, 1, tb), lambda i, be, nv: (i, 0, 0), memory_space=pltpu.SMEM),
            pl.BlockSpec((tb, 1), lambda i, be, nv: (i, 0)),
            pl.BlockSpec((1, D_MODEL, D_EXPERT), lambda i, be, nv: (be[i], 0, 0)),
            pl.BlockSpec((1, D_MODEL, D_EXPERT), lambda i, be, nv: (be[i], 0, 0)),
            pl.BlockSpec((1, D_EXPERT, D_MODEL), lambda i, be, nv: (be[i], 0, 0)),
            pl.BlockSpec(memory_space=pl.ANY),
        ],
        out_specs=pl.BlockSpec(memory_space=pl.ANY),
        scratch_shapes=[pltpu.VMEM((2, tb, D_MODEL), F32), pltpu.VMEM((2, tb, D_MODEL), F32),
                        pltpu.VMEM((D_MODEL, D_EXPERT), BF16), pltpu.VMEM((D_MODEL, D_EXPERT), BF16),
                        pltpu.VMEM((D_EXPERT, D_MODEL), BF16),
                        pltpu.SemaphoreType.DMA((2,)), pltpu.SemaphoreType.DMA((2,))],
    )
    return pl.pallas_call(
        functools.partial(_moe_body, tb=tb, nb=nb),
        grid_spec=grid_spec,
        out_shape=jax.ShapeDtypeStruct((n_out_rows, D_MODEL), F32),
        compiler_params=_cparams(("arbitrary",)),
        name="moe_experts",
    )(block_e, n_valid, src3, src3, dst.reshape(nb, 1, tb), sw.reshape(nb * tb, 1), w1, w3, w2, h2)


def _route(lg):
    n = lg.shape[0]
    gl = lg[:, :N_GROUPS]
    el = lg[:, N_GROUPS:N_GROUPS + N_EXPERTS].reshape(n, N_GROUPS, EXPERTS_PER_GROUP)
    g_idx = jnp.argmax(gl, axis=-1)
    g_gate = jnp.take_along_axis(jax.nn.softmax(gl, axis=-1), g_idx[:, None], axis=1)[:, 0]
    e_sel = jnp.take_along_axis(el, g_idx[:, None, None], axis=1)[:, 0, :]
    top_l, top_j = lax.top_k(e_sel, TOP_K)
    weights = g_gate[:, None] * jax.nn.softmax(top_l, axis=-1)
    expert = g_idx[:, None] * EXPERTS_PER_GROUP + top_j
    return expert.astype(jnp.int32), weights


def _dispatch(expert, weights, tb):
    n = expert.shape[0]
    a = n * TOP_K
    flat_e = expert.reshape(a)
    flat_w = weights.reshape(a)
    order = jnp.argsort(flat_e).astype(jnp.int32)
    counts = jnp.sum((flat_e[:, None] == jnp.arange(N_EXPERTS)[None, :]).astype(jnp.int32), axis=0)
    starts = jnp.cumsum(counts) - counts
    pcounts = (counts + tb - 1) // tb * tb
    pends = jnp.cumsum(pcounts)
    pstarts = pends - pcounts
    nb = -(-a // tb) + N_EXPERTS
    n_used = (pends[-1] // tb).astype(jnp.int32)
    blk = jnp.arange(nb, dtype=jnp.int32)
    block_e = jnp.minimum(jnp.searchsorted(pends, blk * tb, side="right"), N_EXPERTS - 1).astype(jnp.int32)
    last_e = block_e[jnp.maximum(n_used - 1, 0)]
    block_e = jnp.where(blk < n_used, block_e, last_e)
    rank0 = blk * tb - pstarts[block_e]
    n_valid = jnp.where(blk < n_used, jnp.clip(counts[block_e] - rank0, 0, tb), 0).astype(jnp.int32)
    r = jnp.arange(tb, dtype=jnp.int32)[None, :]
    valid = r < n_valid[:, None]
    pos = jnp.clip((starts[block_e] + rank0)[:, None] + r, 0, a - 1)
    aidx = order[pos]
    tok = aidx // TOP_K
    src = jnp.where(valid, tok, 0).reshape(nb * tb)
    dst = jnp.where(valid, (aidx % TOP_K) * n + tok, 0).reshape(nb * tb)
    sw = jnp.where(valid, flat_w[aidx], 0.0).reshape(nb * tb)
    return block_e, n_valid, src, dst, sw


def _combine_body(x_ref, y0_ref, y1_ref, g2_ref, o_ref):
    o_ref[0] = x_ref[0] + g2_ref[0] * (y0_ref[0] + y1_ref[0])


def _combine(x, y3, g2, row0, tm):
    bn, l, _ = x.shape
    nt = l // tm
    t0 = row0 // tm
    rowspec = pl.BlockSpec((1, tm, D_MODEL), lambda b, t: (b, t, 0))
    return pl.pallas_call(
        _combine_body,
        grid=(bn, nt),
        in_specs=[rowspec,
                  pl.BlockSpec((1, tm, D_MODEL), lambda b, t: (0, t0 + b * nt + t, 0)),
                  pl.BlockSpec((1, tm, D_MODEL), lambda b, t: (1, t0 + b * nt + t, 0)),
                  pl.BlockSpec((1, 1, D_MODEL), lambda b, t: (b, 0, 0))],
        out_specs=rowspec,
        out_shape=jax.ShapeDtypeStruct(x.shape, F32),
        compiler_params=_cparams(("arbitrary", "arbitrary")),
        name="moe_combine",
    )(x, y3, y3, g2)


_ROPE_PERM = np.concatenate([np.arange(8, 16), np.arange(0, 8), np.arange(24, 32), np.arange(16, 24)])
_ROPE_SIGN = np.concatenate([-np.ones(8), np.ones(8), -np.ones(8), np.ones(8)]).astype(np.float32)


def _rope_tables_t(l):
    half = C_ROPE // 2
    inv = ROPE_BASE ** (-jnp.arange(0, half, 2, dtype=F32) / half)
    pos = jnp.arange(l)
    ang_r = (pos // GRID_W).astype(F32)[None, :] * inv[:, None]
    ang_c = (pos % GRID_W).astype(F32)[None, :] * inv[:, None]
    cos_t = jnp.concatenate([jnp.cos(ang_r)] * 2 + [jnp.cos(ang_c)] * 2, axis=0)
    sin_t = jnp.concatenate([jnp.sin(ang_r)] * 2 + [jnp.sin(ang_c)] * 2, axis=0)
    return cos_t, sin_t


def _col(v):
    return v.astype(F32)[:, None]


def _layer_weights(i, p):
    w_in = p["w_in"][i]
    kr0 = IN_A + IN_B + C_Q_LORA + C_KV_LORA
    kr_cols = w_in[:, kr0:kr0 + C_ROPE]
    w_aug = jnp.concatenate([w_in, kr_cols[:, _ROPE_PERM] * _ROPE_SIGN], axis=1)

    wq = p["c_w_q_up"][i].reshape(C_Q_LORA, C_HEADS, C_QK)
    wq_n = wq[:, :, :C_NOPE].reshape(C_Q_LORA, C_HEADS * C_NOPE)
    wq_r = wq[:, :, C_NOPE:]
    wq_rp = (wq_r[:, :, _ROPE_PERM] * _ROPE_SIGN).reshape(C_Q_LORA, C_HEADS * C_ROPE)
    wq_all = jnp.concatenate([wq_n, wq_r.reshape(C_Q_LORA, C_HEADS * C_ROPE), wq_rp], axis=1)

    wkv = p["c_w_kv_up"][i].reshape(C_KV_LORA, C_HEADS, C_NOPE + C_VDIM)
    wkv_all = jnp.concatenate([wkv[:, :, :C_NOPE].reshape(C_KV_LORA, C_HEADS * C_NOPE),
                               wkv[:, :, C_NOPE:].reshape(C_KV_LORA, C_HEADS * C_VDIM)], axis=1)

    cqn = p["c_q_norm"][i].astype(F32) * (C_QK ** -0.5)
    ckn = p["c_k_norm"][i].astype(F32)
    w_route = jnp.concatenate([p["moe_w_group"][i], p["moe_w_router"][i],
                               jnp.zeros((D_MODEL, ROUTER_PAD - N_GROUPS - N_EXPERTS), F32)], axis=1)
    return {
        "norm_mix": p["norm_mix"][i][None, :],
        "w_in_t": w_aug.T.astype(BF16),
        "avn": _col(p["a_v_norm"][i]),
        "wst": jnp.transpose(p["a_w_s"][i], (0, 2, 1)).astype(BF16),
        "bs": p["a_b_s"][i][:, None, :],
        "gq": _col(p["b_q_norm"][i] * (HEAD_DIM ** -0.5)),
        "gk": _col(p["b_k_norm"][i]),
        "cqa": _col(p["c_q_a_norm"][i]),
        "wq_t": wq_all.T.astype(BF16),
        "ckva": _col(p["c_kv_a_norm"][i]),
        "wkv_t": wkv_all.T.astype(BF16),
        "gcqn": _col(cqn[:C_NOPE]),
        "gcqr": _col(cqn[C_NOPE:]),
        "gcqrp": _col(cqn[C_NOPE:][_ROPE_PERM]),
        "gckn": _col(ckn[:C_NOPE]),
        "gckr": _col(ckn[C_NOPE:]),
        "gckrp": _col(ckn[C_NOPE:][_ROPE_PERM]),
        "w_out_t": p["w_out"][i].T.astype(BF16),
        "norm_ffn": p["norm_ffn"][i][None, :],
        "w_route": w_route,
    }


def _pick_tile(l, pref):
    t = min(l, pref)
    while l % t:
        t //= 2
    return t


def kernel(x, c, ctx, c_ctx, w_ada, b_ada, norm_mix, w_in, a_v_norm, a_w_s, a_b_s, b_q_norm, b_k_norm, b_rpb,
           c_q_a_norm, c_w_q_up, c_kv_a_norm, c_w_kv_up, c_q_norm, c_k_norm, w_out, norm_ffn,
           moe_w_group, moe_w_router, moe_w1, moe_w3, moe_w2):
    p = dict(norm_mix=norm_mix, w_in=w_in, a_v_norm=a_v_norm, a_w_s=a_w_s, a_b_s=a_b_s, b_q_norm=b_q_norm,
             b_k_norm=b_k_norm, c_q_a_norm=c_q_a_norm, c_w_q_up=c_w_q_up, c_kv_a_norm=c_kv_a_norm,
             c_w_kv_up=c_w_kv_up, c_q_norm=c_q_norm, c_k_norm=c_k_norm, w_out=w_out, norm_ffn=norm_ffn,
             moe_w_group=moe_w_group, moe_w_router=moe_w_router)
    bn, l, d = x.shape
    nc = ctx.shape[1]
    depth = w_ada.shape[0]
    rows = l // GRID_W
    tm = _pick_tile(l, 512)
    tq = _pick_tile(l, MLA_QUERY_TILE)

    cos_t, sin_t = _rope_tables_t(l)
    cos_x = jnp.ones((C_ROPE, nc), F32)
    sin_x = jnp.zeros((C_ROPE, nc), F32)
    cc8 = jnp.concatenate([c, c_ctx[None, :], jnp.zeros((8 - bn - 1, d), F32)], axis=0)

    xc = ctx
    for i in range(depth):
        need_ctx = i < depth - 1
        lw = _layer_weights(i, p)
        mod = _ada(cc8, w_ada[i], b_ada[i][None, :])
        mods = [mod[:bn, j * d:(j + 1) * d][:, None, :] for j in range(6)]
        modx = [jnp.broadcast_to(mod[bn, j * d:(j + 1) * d][None, None, :], (bn, 1, d)) for j in range(6)]
        sh1, s1, g1, sh2, s2, g2 = mods
        sh1x, s1x, g1x, sh2x, s2x, g2x = modx

        oa, qb, kb, vb, qc, kc, vc = _mixin(x, sh1, s1, lw, cos_t, sin_t, tm)
        oax, qbx, kbx, vbx, qcx, kcx, vcx = _mixin(xc, sh1x, s1x, lw, cos_x, sin_x, nc)

        table = _na_bias_table(b_rpb[i], rows)
        ob = _na_attention(qb, kb, vb, kbx, vbx, table)
        oc = _mla_attention(qc, kc, vc, kcx, vcx, tq)
        n_lat = bn * l
        n_tok = n_lat + (bn * nc if need_ctx else 0)
        x, h2_all, lg_all = _mixout(x, oa, ob, oc, g1, sh2, s2, lw, tm, n_tok)
        if need_ctx:
            obx = _flash(qbx, kbx[:, None], vbx,
                         pl.BlockSpec((1, 1, nc, LANE), lambda b, h, ii, j: (b, 0, j, h // 2)),
                         tq=nc, tk=nc, name="ctx_dense_attention")
            ocx = _flash(qcx, kcx, vcx.reshape(bn, C_WIDTH, nc),
                         pl.BlockSpec((1, 1, nc, QPAD), lambda b, h, ii, j: (b, h, j, 0)),
                         tq=nc, tk=nc, name="ctx_mla_attention")
            xc, h2_all, lg_all = _mixout(xc, oax, obx, ocx, g1x, sh2x, s2x, lw, nc, n_tok, row0=n_lat,
                                         bufs=(h2_all, lg_all))
        expert, weights = _route(lg_all)
        block_e, n_valid, src, dst, sw = _dispatch(expert, weights, MOE_BLOCK)
        y = _moe(h2_all, block_e, n_valid, src, dst, sw, moe_w1[i], moe_w3[i], moe_w2[i],
                 TOP_K * n_tok, MOE_BLOCK)
        y3 = y.reshape(TOP_K, n_tok, d)
        x = _combine(x, y3, g2, 0, tm)
        if need_ctx:
            xc = _combine(xc, y3, g2x, n_lat, nc)
    return x
```

```python
import functools

import numpy as np
import jax
import jax.numpy as jnp
from jax import lax
from jax.experimental import pallas as pl
from jax.experimental.pallas import tpu as pltpu

F32 = jnp.float32
BF16 = jnp.bfloat16

D_MODEL = 1024
GRID_W = 64
HEAD_DIM = 64
EPS = 1e-6
NEG_INF = -1e30

A_HEADS = 4
A_WIDTH = A_HEADS * HEAD_DIM
CHUNK = 128
B_HEADS = 6
B_WIDTH = B_HEADS * HEAD_DIM
NA_ROWS = 8
NA_COLS = 16
C_HEADS = 6
C_NOPE = 64
C_ROPE = 32
C_QK = C_NOPE + C_ROPE
C_VDIM = 64
C_Q_LORA = 384
C_KV_LORA = 256
C_WIDTH = C_HEADS * C_VDIM
ROPE_BASE = 10000.0
IN_A = 2 * A_WIDTH
IN_B = 3 * B_WIDTH
IN_C = C_Q_LORA + C_KV_LORA + C_ROPE
IN_AUG = IN_A + IN_B + IN_C + C_ROPE

N_GROUPS = 8
EXPERTS_PER_GROUP = 8
N_EXPERTS = N_GROUPS * EXPERTS_PER_GROUP
TOP_K = 2
D_EXPERT = D_MODEL // 2
ROUTER_PAD = 128

V7X_VMEM_LIMIT_BYTES = 56 * 1024 * 1024
LANE = 128
QPAD = 128

MLA_KEY_CHUNK = 256
MLA_QUERY_TILE = 1024
NA_QROWS = 8
NA_KBLK_ROWS = 4
MOE_BLOCK = 256

HIGHEST = lax.Precision.HIGHEST


def _cparams(sem):
    return pltpu.CompilerParams(dimension_semantics=sem, vmem_limit_bytes=V7X_VMEM_LIMIT_BYTES)


def _ada_body(c_ref, w_ref, b_ref, o_ref):
    cc = c_ref[...]
    s = cc * (1.0 / (1.0 + jnp.exp(-cc)))
    o_ref[...] = jnp.dot(s, w_ref[...], preferred_element_type=F32, precision=HIGHEST) + b_ref[...]


def _ada(cc8, w, b):
    n_out = w.shape[1]
    tn = 1024
    return pl.pallas_call(
        _ada_body,
        grid=(n_out // tn,),
        in_specs=[pl.BlockSpec((8, D_MODEL), lambda j: (0, 0)),
                  pl.BlockSpec((D_MODEL, tn), lambda j: (0, j)),
                  pl.BlockSpec((1, tn), lambda j: (0, j))],
        out_specs=pl.BlockSpec((8, tn), lambda j: (0, j)),
        out_shape=jax.ShapeDtypeStruct((8, n_out), F32),
        compiler_params=_cparams(("arbitrary",)),
        name="ada_mod",
    )(cc8, w, b)


def _rms_rows(v):
    return lax.rsqrt(jnp.mean(v * v, axis=0, keepdims=True) + EPS)


def _gelu_tanh(x):
    return 0.5 * x * (1.0 + jnp.tanh(0.7978845608028654 * (x + 0.044715 * (x * x * x))))


def _mixin_body(x_ref, sh_ref, sc_ref, gn_ref, wt_ref, avn_ref, wst_ref, bs_ref, gq_ref, gk_ref,
                cqa_ref, wq_ref, ckva_ref, wkv_ref, gcqn_ref, gcqr_ref, gcqrp_ref, gckn_ref, gckr_ref,
                gckrp_ref, cos_ref, sin_ref,
                oa_ref, qb_ref, kb_ref, vb_ref, qc_ref, kc_ref, vc_ref, *, tm):
    x = x_ref[0]
    h = x * lax.rsqrt(jnp.mean(x * x, axis=-1, keepdims=True) + EPS) * gn_ref[...]
    h = h * (1.0 + sc_ref[0]) + sh_ref[0]
    zt = lax.dot_general(wt_ref[...], h.astype(BF16), (((1,), (1,)), ((), ())),
                         preferred_element_type=F32)

    ga = _gelu_tanh(zt[0:IN_A])
    u = ga[0:A_WIDTH]
    v = ga[A_WIDTH:IN_A]
    vn = (v * _rms_rows(v) * avn_ref[...]).astype(BF16)
    for hd in range(A_HEADS):
        r0, r1 = hd * HEAD_DIM, (hd + 1) * HEAD_DIM
        for c in range(tm // CHUNK):
            c0, c1 = c * CHUNK, (c + 1) * CHUNK
            vm = jnp.dot(vn[r0:r1, c0:c1], wst_ref[hd], preferred_element_type=F32) + bs_ref[hd]
            oa_ref[0, r0:r1, c0:c1] = (u[r0:r1, c0:c1] * vm).astype(BF16)

    zb = zt[IN_A:IN_A + IN_B]
    zeros64 = jnp.zeros((HEAD_DIM, tm), F32)
    kn = []
    for hd in range(B_HEADS):
        q = zb[hd * HEAD_DIM:(hd + 1) * HEAD_DIM]
        k = zb[B_WIDTH + hd * HEAD_DIM:B_WIDTH + (hd + 1) * HEAD_DIM]
        qn = q * _rms_rows(q) * gq_ref[...]
        kn.append(k * _rms_rows(k) * gk_ref[...])
        pair = [qn, zeros64] if hd % 2 == 0 else [zeros64, qn]
        qb_ref[0, hd] = jnp.concatenate(pair, axis=0).astype(BF16)
    for p in range(B_HEADS // 2):
        kt = jnp.concatenate([kn[2 * p], kn[2 * p + 1]], axis=0)
        kb_ref[0, :, p * LANE:(p + 1) * LANE] = kt.T.astype(BF16)
    vb_ref[0] = zb[2 * B_WIDTH:3 * B_WIDTH].astype(BF16)

    zc = zt[IN_A + IN_B:IN_AUG]
    ql = zc[0:C_Q_LORA]
    kvl = zc[C_Q_LORA:C_Q_LORA + C_KV_LORA]
    kr = zc[C_Q_LORA + C_KV_LORA:C_Q_LORA + C_KV_LORA + C_ROPE]
    krp = zc[C_Q_LORA + C_KV_LORA + C_ROPE:C_Q_LORA + C_KV_LORA + 2 * C_ROPE]
    qln = (ql * _rms_rows(ql) * cqa_ref[...]).astype(BF16)
    kvln = (kvl * _rms_rows(kvl) * ckva_ref[...]).astype(BF16)
    qt = jnp.dot(wq_ref[...], qln, preferred_element_type=F32)
    kvt = jnp.dot(wkv_ref[...], kvln, preferred_element_type=F32)
    cos = cos_ref[...]
    sin = sin_ref[...]
    krn = _rms_rows(kr) * (gckr_ref[...] * kr * cos + gckrp_ref[...] * krp * sin)
    zeros32 = jnp.zeros((QPAD - C_QK, tm), F32)
    nq = C_HEADS * C_NOPE
    nr = C_HEADS * C_ROPE
    for hd in range(C_HEADS):
        qn = qt[hd * C_NOPE:(hd + 1) * C_NOPE]
        qn = qn * _rms_rows(qn) * gcqn_ref[...]
        qr = qt[nq + hd * C_ROPE:nq + (hd + 1) * C_ROPE]
        qrp = qt[nq + nr + hd * C_ROPE:nq + nr + (hd + 1) * C_ROPE]
        qrn = _rms_rows(qr) * (gcqr_ref[...] * qr * cos + gcqrp_ref[...] * qrp * sin)
        qc_ref[0, hd] = jnp.concatenate([qn, qrn, zeros32], axis=0).astype(BF16)
        kn_c = kvt[hd * C_NOPE:(hd + 1) * C_NOPE]
        kn_c = kn_c * _rms_rows(kn_c) * gckn_ref[...]
        kc_ref[0, hd] = jnp.concatenate([kn_c, krn, zeros32], axis=0).T.astype(BF16)
    vt = vc_ref.shape[3]
    v_rows = kvt[nq:nq + C_WIDTH].astype(BF16)
    for c in range(tm // vt):
        vc_ref[0, c] = v_rows[:, c * vt:(c + 1) * vt]


def _mixin(x, sh, sc, lw, cos_t, sin_t, tm):
    bn, l, _ = x.shape
    vt = min(MLA_KEY_CHUNK, tm)
    const2 = lambda b, t: (0, 0)
    const3 = lambda b, t: (0, 0, 0)
    in_specs = [
        pl.BlockSpec((1, tm, D_MODEL), lambda b, t: (b, t, 0)),
        pl.BlockSpec((1, 1, D_MODEL), lambda b, t: (b, 0, 0)),
        pl.BlockSpec((1, 1, D_MODEL), lambda b, t: (b, 0, 0)),
        pl.BlockSpec((1, D_MODEL), const2),
        pl.BlockSpec((IN_AUG, D_MODEL), const2),
        pl.BlockSpec((A_WIDTH, 1), const2),
        pl.BlockSpec((A_HEADS, CHUNK, CHUNK), const3),
        pl.BlockSpec((A_HEADS, 1, CHUNK), const3),
        pl.BlockSpec((HEAD_DIM, 1), const2),
        pl.BlockSpec((HEAD_DIM, 1), const2),
        pl.BlockSpec((C_Q_LORA, 1), const2),
        pl.BlockSpec((C_HEADS * (C_NOPE + 2 * C_ROPE), C_Q_LORA), const2),
        pl.BlockSpec((C_KV_LORA, 1), const2),
        pl.BlockSpec((C_HEADS * (C_NOPE + C_VDIM), C_KV_LORA), const2),
        pl.BlockSpec((C_NOPE, 1), const2),
        pl.BlockSpec((C_ROPE, 1), const2),
        pl.BlockSpec((C_ROPE, 1), const2),
        pl.BlockSpec((C_NOPE, 1), const2),
        pl.BlockSpec((C_ROPE, 1), const2),
        pl.BlockSpec((C_ROPE, 1), const2),
        pl.BlockSpec((C_ROPE, tm), lambda b, t: (0, t)),
        pl.BlockSpec((C_ROPE, tm), lambda b, t: (0, t)),
    ]
    out_shape = (
        jax.ShapeDtypeStruct((bn, A_WIDTH, l), BF16),
        jax.ShapeDtypeStruct((bn, B_HEADS, QPAD, l), BF16),
        jax.ShapeDtypeStruct((bn, l, B_WIDTH), BF16),
        jax.ShapeDtypeStruct((bn, B_WIDTH, l), BF16),
        jax.ShapeDtypeStruct((bn, C_HEADS, QPAD, l), BF16),
        jax.ShapeDtypeStruct((bn, C_HEADS, l, QPAD), BF16),
        jax.ShapeDtypeStruct((bn, l // vt, C_WIDTH, vt), BF16),
    )
    out_specs = (
        pl.BlockSpec((1, A_WIDTH, tm), lambda b, t: (b, 0, t)),
        pl.BlockSpec((1, B_HEADS, QPAD, tm), lambda b, t: (b, 0, 0, t)),
        pl.BlockSpec((1, tm, B_WIDTH), lambda b, t: (b, t, 0)),
        pl.BlockSpec((1, B_WIDTH, tm), lambda b, t: (b, 0, t)),
        pl.BlockSpec((1, C_HEADS, QPAD, tm), lambda b, t: (b, 0, 0, t)),
        pl.BlockSpec((1, C_HEADS, tm, QPAD), lambda b, t: (b, 0, t, 0)),
        pl.BlockSpec((1, tm // vt, C_WIDTH, vt), lambda b, t: (b, t, 0, 0)),
    )
    return pl.pallas_call(
        functools.partial(_mixin_body, tm=tm),
        grid=(bn, l // tm),
        in_specs=in_specs,
        out_specs=out_specs,
        out_shape=out_shape,
        compiler_params=_cparams(("arbitrary", "arbitrary")),
        name="mix_in",
    )(x, sh, sc, lw["norm_mix"], lw["w_in_t"], lw["avn"], lw["wst"], lw["bs"], lw["gq"], lw["gk"],
      lw["cqa"], lw["wq_t"], lw["ckva"], lw["wkv_t"], lw["gcqn"], lw["gcqr"], lw["gcqrp"], lw["gckn"],
      lw["gckr"], lw["gckrp"], cos_t, sin_t)


def _na_body(q_ref, k0_ref, k1_ref, k2_ref, k3_ref, v0_ref, v1_ref, v2_ref, v3_ref, kx_ref, vx_ref, t_ref,
             o_ref):
    q = q_ref[0, 0]
    k_refs = (k0_ref, k1_ref, k2_ref, k3_ref)
    v_refs = (v0_ref, v1_ref, v2_ref, v3_ref)
    s = jnp.concatenate([jnp.dot(kr[0], q, preferred_element_type=F32) for kr in k_refs], axis=0)
    s = s + t_ref[0, 0]
    sx = jnp.dot(kx_ref[0], q, preferred_element_type=F32)
    m = jnp.maximum(jnp.max(s, axis=0, keepdims=True), jnp.max(sx, axis=0, keepdims=True))
    p = jnp.exp(s - m)
    px = jnp.exp(sx - m)
    den = jnp.sum(p, axis=0, keepdims=True) + jnp.sum(px, axis=0, keepdims=True)
    pb = p.astype(BF16)
    nk = k0_ref.shape[1]
    o = jnp.dot(vx_ref[0], px.astype(BF16), preferred_element_type=F32)
    for j, vr in enumerate(v_refs):
        o = o + jnp.dot(vr[0], pb[j * nk:(j + 1) * nk], preferred_element_type=F32)
    o_ref[0] = (o / den).astype(BF16)


def _na_attention(qb, kb, vb, kxb, vxb, table):
    bn, _, _, l = qb.shape
    nq = NA_QROWS * GRID_W
    nk = NA_KBLK_ROWS * GRID_W
    nblk = l // nq
    nkb = l // nk
    nctx = kxb.shape[1]

    def kmap(j):
        return lambda b, h, i: (b, jnp.clip(2 * i - 1 + j, 0, nkb - 1), h // 2)

    def vmap_(j):
        return lambda b, h, i: (b, h, jnp.clip(2 * i - 1 + j, 0, nkb - 1))

    def tmap(b, h, i):
        return (h, jnp.where(i == 0, 0, jnp.where(i == nblk - 1, 2, 1)), 0, 0)

    in_specs = ([pl.BlockSpec((1, 1, QPAD, nq), lambda b, h, i: (b, h, 0, i))]
                + [pl.BlockSpec((1, nk, LANE), kmap(j)) for j in range(4)]
                + [pl.BlockSpec((1, HEAD_DIM, nk), vmap_(j)) for j in range(4)]
                + [pl.BlockSpec((1, nctx, LANE), lambda b, h, i: (b, 0, h // 2)),
                   pl.BlockSpec((1, HEAD_DIM, nctx), lambda b, h, i: (b, h, 0)),
                   pl.BlockSpec((1, 1, 4 * nk, nq), tmap)])
    return pl.pallas_call(
        _na_body,
        grid=(bn, B_HEADS, nblk),
        in_specs=in_specs,
        out_specs=pl.BlockSpec((1, HEAD_DIM, nq), lambda b, h, i: (b, h, i)),
        out_shape=jax.ShapeDtypeStruct((bn, B_WIDTH, l), BF16),
        compiler_params=_cparams(("arbitrary", "arbitrary", "arbitrary")),
        name="na_attention",
    )(qb, kb, kb, kb, kb, vb, vb, vb, vb, kxb, vxb, table)


def _na_bias_table(rpb, rows):
    nblk = rows // NA_QROWS
    qc = np.arange(GRID_W)
    kc = np.arange(GRID_W)
    c0 = np.clip(qc - NA_COLS // 2, 0, GRID_W - NA_COLS)
    valid_col = (kc[:, None] >= c0[None, :]) & (kc[:, None] < c0[None, :] + NA_COLS)
    dc = np.clip(kc[:, None] - qc[None, :], -(NA_COLS - 1), NA_COLS - 1) + NA_COLS - 1
    dc_onehot = (dc[None] == np.arange(2 * NA_COLS - 1)[:, None, None]).astype(np.float32)
    tabs = []
    for i in (0, 1, nblk - 1):
        kr = NA_KBLK_ROWS * (2 * i - 1) + np.arange(4 * NA_KBLK_ROWS)
        qr = NA_QROWS * i + np.arange(NA_QROWS)
        r0 = np.clip(qr - NA_ROWS // 2, 0, rows - NA_ROWS)
        valid_row = ((kr[:, None] >= r0[None, :]) & (kr[:, None] < r0[None, :] + NA_ROWS)
                     & (kr[:, None] >= 0) & (kr[:, None] < rows))
        dr = np.clip(kr[:, None] - qr[None, :] + NA_ROWS - 1, 0, 2 * NA_ROWS - 2)
        bias = jnp.einsum("hkqd,dcx->hkcqx", rpb[:, dr].astype(F32), dc_onehot, precision=HIGHEST)
        valid = valid_row[:, None, :, None] & valid_col[None, :, None, :]
        tabs.append(jnp.where(valid[None], bias.astype(F32), NEG_INF).reshape(
            rpb.shape[0], 4 * NA_KBLK_ROWS * GRID_W, NA_QROWS * GRID_W))
    return jnp.stack(tabs, axis=1)


def _flash_body(*refs, has_extra):
    if has_extra:
        q_ref, k_ref, v_ref, kx_ref, vx_ref, o_ref, m_sc, l_sc, acc_sc = refs
    else:
        q_ref, k_ref, v_ref, o_ref, m_sc, l_sc, acc_sc = refs
    kv = pl.program_id(3)
    nkv = pl.num_programs(3)

    @pl.when(kv == 0)
    def _():
        m_sc[...] = jnp.full(m_sc.shape, -jnp.inf, F32)
        l_sc[...] = jnp.zeros(l_sc.shape, F32)
        acc_sc[...] = jnp.zeros(acc_sc.shape, F32)

    def step(k, v):
        s = jnp.dot(k, q_ref[0, 0], preferred_element_type=F32)
        m_old = m_sc[...]
        m_new = jnp.maximum(m_old, jnp.max(s, axis=0, keepdims=True))
        alpha = jnp.exp(m_old - m_new)
        p = jnp.exp(s - m_new)
        l_sc[...] = alpha * l_sc[...] + jnp.sum(p, axis=0, keepdims=True)
        acc_sc[...] = alpha * acc_sc[...] + jnp.dot(v, p.astype(BF16), preferred_element_type=F32)
        m_sc[...] = m_new

    step(k_ref[0, 0], v_ref[0])

    @pl.when(kv == nkv - 1)
    def _():
        if has_extra:
            step(kx_ref[0, 0], vx_ref[0])
        o_ref[0] = (acc_sc[...] / l_sc[...]).astype(BF16)


def _flash(q, k, v, k_spec, kx=None, vx=None, kx_spec=None, *, tq, tk, name):
    bn, nh, _, lq = q.shape
    lk = v.shape[2]
    has_extra = kx is not None
    in_specs = [pl.BlockSpec((1, 1, QPAD, tq), lambda b, h, i, j: (b, h, 0, i)),
                k_spec,
                pl.BlockSpec((1, HEAD_DIM, tk), lambda b, h, i, j: (b, h, j))]
    args = [q, k, v]
    if has_extra:
        nx = vx.shape[2]
        in_specs += [kx_spec, pl.BlockSpec((1, HEAD_DIM, nx), lambda b, h, i, j: (b, h, 0))]
        args += [kx, vx]
    return pl.pallas_call(
        functools.partial(_flash_body, has_extra=has_extra),
        grid=(bn, nh, lq // tq, lk // tk),
        in_specs=in_specs,
        out_specs=pl.BlockSpec((1, HEAD_DIM, tq), lambda b, h, i, j: (b, h, i)),
        out_shape=jax.ShapeDtypeStruct((bn, nh * HEAD_DIM, lq), BF16),
        scratch_shapes=[pltpu.VMEM((1, tq), F32), pltpu.VMEM((1, tq), F32), pltpu.VMEM((HEAD_DIM, tq), F32)],
        compiler_params=_cparams(("arbitrary", "arbitrary", "arbitrary", "arbitrary")),
        name=name,
    )(*args)


ONES_ROWS = 16


def _mla_body(q_ref, k_ref, v_ref, kx_ref, vx_ref, o_ref, *, tk, nchunks):
    q = q_ref[0, 0]
    tq = q.shape[1]

    def scores(k):
        return jnp.dot(k, q, preferred_element_type=F32).astype(BF16)

    def absorb(s, v, carry):
        m, acc = carry
        m_new = jnp.maximum(m, jnp.max(s, axis=0, keepdims=True).astype(F32))
        alpha = jnp.exp(m - m_new)
        p = jnp.exp(s - m_new.astype(BF16))
        v_ext = jnp.concatenate([v, jnp.ones((ONES_ROWS, v.shape[1]), v.dtype)], axis=0)
        acc = alpha * acc + jnp.dot(v_ext, p, preferred_element_type=F32)
        return m_new, acc

    carry = (jnp.full((1, tq), -jnp.inf, F32), jnp.zeros((HEAD_DIM + ONES_ROWS, tq), F32))
    s_cur = scores(k_ref[0, 0, 0:tk, :])
    for j in range(nchunks):
        if j + 1 < nchunks:
            s_next = scores(k_ref[0, 0, (j + 1) * tk:(j + 2) * tk, :])
        else:
            s_next = scores(kx_ref[0, 0])
        carry = absorb(s_cur, v_ref[0, j], carry)
        s_cur = s_next
    _, acc = absorb(s_cur, vx_ref[0, 0], carry)
    o_ref[0] = (acc[0:HEAD_DIM] / acc[HEAD_DIM:HEAD_DIM + 1]).astype(BF16)


def _mla_attention(q, k, v, kx, vx, tq):
    bn, nh, _, l = q.shape
    nchunks, tk = v.shape[1], v.shape[3]
    nc = kx.shape[2]
    return pl.pallas_call(
        functools.partial(_mla_body, tk=tk, nchunks=nchunks),
        grid=(bn, nh, l // tq),
        in_specs=[pl.BlockSpec((1, 1, QPAD, tq), lambda b, h, i: (b, h, 0, i)),
                  pl.BlockSpec((1, 1, l, QPAD), lambda b, h, i: (b, h, 0, 0)),
                  pl.BlockSpec((1, nchunks, HEAD_DIM, tk), lambda b, h, i: (b, 0, h, 0)),
                  pl.BlockSpec((1, 1, nc, QPAD), lambda b, h, i: (b, h, 0, 0)),
                  pl.BlockSpec((1, 1, HEAD_DIM, nc), lambda b, h, i: (b, 0, h, 0))],
        out_specs=pl.BlockSpec((1, HEAD_DIM, tq), lambda b, h, i: (b, h, i)),
        out_shape=jax.ShapeDtypeStruct((bn, nh * HEAD_DIM, l), BF16),
        compiler_params=_cparams(("arbitrary", "arbitrary", "arbitrary")),
        name="mla_attention",
    )(q, k, v, kx, vx)


TOKEN_ROWS = 8


def _first_argmax_rows(v, row_id):
    vmax = jnp.max(v, axis=0, keepdims=True)
    idx = jnp.min(jnp.where(v == vmax, row_id, float(v.shape[0])), axis=0, keepdims=True)
    return vmax, idx


def _mixout_body(x_ref, oa_ref, ob_ref, oc_ref, wt_ref, g1_ref, sh2_ref, sc2_ref, gn2_ref, wr_ref, tri_ref,
                 *rest, tm, steps, has_base):
    if has_base:
        cnt_in_ref = rest[0]
    xo_ref, h2_ref, ei_ref, wf_ref, cnt_ref, base_sc = rest[-6:]
    i = pl.program_id(0)

    @pl.when(i == 0)
    def _():
        if has_base:
            base_sc[...] = cnt_in_ref[:, 0:1]
        else:
            base_sc[...] = jnp.zeros(base_sc.shape, F32)

    ot = jnp.concatenate([oa_ref[0], ob_ref[0], oc_ref[0]], axis=0)
    out_t = jnp.dot(wt_ref[...], ot, preferred_element_type=F32)
    xn = x_ref[0] + g1_ref[0] * out_t.T
    xo_ref[0] = xn
    h2 = xn * lax.rsqrt(jnp.mean(xn * xn, axis=-1, keepdims=True) + EPS) * gn2_ref[...]
    h2 = h2 * (1.0 + sc2_ref[0]) + sh2_ref[0]
    for s in range(TOKEN_ROWS):
        h2_ref[pl.ds(s, tm, stride=TOKEN_ROWS), :] = h2[:, s * LANE:(s + 1) * LANE]

    lt = jnp.dot(h2, wr_ref[...], preferred_element_type=F32, precision=HIGHEST).T
    gl = lt[0:N_GROUPS]
    rid = lax.broadcasted_iota(jnp.int32, (N_GROUPS, tm), 0).astype(F32)
    gmax, g_idx = _first_argmax_rows(gl, rid)
    g_gate = 1.0 / jnp.sum(jnp.exp(gl - gmax), axis=0, keepdims=True)
    e_sel = jnp.zeros((EXPERTS_PER_GROUP, tm), F32)
    for g in range(N_GROUPS):
        lo = N_GROUPS + g * EXPERTS_PER_GROUP
        e_sel = jnp.where(g_idx == float(g), lt[lo:lo + EXPERTS_PER_GROUP], e_sel)
    v1, j1 = _first_argmax_rows(e_sel, rid)
    v2, j2 = _first_argmax_rows(jnp.where(rid == j1, -jnp.inf, e_sel), rid)
    t21 = jnp.exp(v2 - v1)
    w1 = g_gate / (1.0 + t21)
    w2 = g_gate * t21 / (1.0 + t21)
    e1 = g_idx * float(EXPERTS_PER_GROUP) + j1
    e2 = g_idx * float(EXPERTS_PER_GROUP) + j2

    eid = lax.broadcasted_iota(jnp.int32, (N_EXPERTS, tm), 0).astype(F32)
    oh1 = (eid == e1).astype(F32)
    oh2 = (eid == e2).astype(F32)
    tri = tri_ref[...]
    cum1 = jnp.dot(oh1.astype(BF16), tri, preferred_element_type=F32)
    cum2 = jnp.dot(oh2.astype(BF16), tri, preferred_element_type=F32)
    tot1 = jnp.sum(oh1, axis=1, keepdims=True)
    tot2 = jnp.sum(oh2, axis=1, keepdims=True)
    base = base_sc[...]
    r1 = jnp.sum(oh1 * (base + cum1), axis=0, keepdims=True)
    r2 = jnp.sum(oh2 * (base + tot1 + cum2), axis=0, keepdims=True)
    live = jnp.where(i < steps, 1.0, 0.0)
    base_new = base + live * (tot1 + tot2)
    base_sc[...] = base_new
    cnt_ref[...] = jnp.broadcast_to(base_new, cnt_ref.shape)

    zeros4 = jnp.zeros((4, tm), F32)
    ei_ref[...] = jnp.concatenate([e1, e2, r1, r2, zeros4], axis=0).astype(jnp.int32)
    wpad = jnp.concatenate([w1, w2, jnp.zeros((LANE - 2, tm), F32)], axis=0)
    wf_ref[...] = wpad.T


def _mixout(x, oa, ob, oc, g1, sh2, sc2, lw, tm, n_rows, row0=0, bufs=None, cnt_in=None):
    bn, l, _ = x.shape
    nt = l // tm
    steps = bn * nt
    extra = 0
    if bufs is None and n_rows > bn * l:
        assert n_rows - bn * l == tm, "spare rows must be exactly one tile"
        extra = 1
    blk0 = row0 // tm

    def bt(i):
        ii = jnp.minimum(i, steps - 1)
        return ii // nt, ii % nt

    const2 = lambda i: (0, 0)
    modspec = pl.BlockSpec((1, 1, D_MODEL), lambda i: (bt(i)[0], 0, 0))
    rowspec = pl.BlockSpec((1, tm, D_MODEL), lambda i: (bt(i)[0], bt(i)[1], 0))
    tri = jnp.asarray(np.triu(np.ones((tm, tm), np.float32), k=1), BF16)
    in_specs = [rowspec,
                pl.BlockSpec((1, A_WIDTH, tm), lambda i: (bt(i)[0], 0, bt(i)[1])),
                pl.BlockSpec((1, B_WIDTH, tm), lambda i: (bt(i)[0], 0, bt(i)[1])),
                pl.BlockSpec((1, C_WIDTH, tm), lambda i: (bt(i)[0], 0, bt(i)[1])),
                pl.BlockSpec((D_MODEL, D_MODEL), const2),
                modspec, modspec, modspec,
                pl.BlockSpec((1, D_MODEL), const2),
                pl.BlockSpec((D_MODEL, ROUTER_PAD), const2),
                pl.BlockSpec((tm, tm), const2)]
    args = [x, oa, ob, oc, lw["w_out_t"], g1, sh2, sc2, lw["norm_ffn"], lw["w_route"], tri]
    aliases = {}
    if bufs is not None:
        in_specs.append(pl.BlockSpec((N_EXPERTS, LANE), const2))
        args.append(cnt_in)
        aliases = {len(args): 1, len(args) + 1: 2, len(args) + 2: 3}
        in_specs += [pl.BlockSpec(memory_space=pl.ANY)] * 3
        args += list(bufs)
    return pl.pallas_call(
        functools.partial(_mixout_body, tm=tm, steps=steps, has_base=bufs is not None),
        grid=(steps + extra,),
        in_specs=in_specs,
        out_specs=(rowspec,
                   pl.BlockSpec((tm * TOKEN_ROWS, LANE), lambda i: (blk0 + i, 0)),
                   pl.BlockSpec((8, tm), lambda i: (0, blk0 + i)),
                   pl.BlockSpec((tm, LANE), lambda i: (blk0 + i, 0)),
                   pl.BlockSpec((N_EXPERTS, LANE), const2)),
        out_shape=(jax.ShapeDtypeStruct((bn, l, D_MODEL), F32),
                   jax.ShapeDtypeStruct((n_rows * TOKEN_ROWS, LANE), F32),
                   jax.ShapeDtypeStruct((8, n_rows), jnp.int32),
                   jax.ShapeDtypeStruct((n_rows, LANE), F32),
                   jax.ShapeDtypeStruct((N_EXPERTS, LANE), F32)),
        scratch_shapes=[pltpu.VMEM((N_EXPERTS, 1), F32)],
        input_output_aliases=aliases,
        compiler_params=_cparams(("arbitrary",)),
        name="mix_out",
    )(*args)


def _tile_rows(idx):
    return pl.ds(pl.multiple_of(idx * TOKEN_ROWS, TOKEN_ROWS), TOKEN_ROWS)


def _slot(ps_ref, ei_ref, k, r):
    return ps_ref[ei_ref[k, r]] + ei_ref[TOP_K + k, r]


def _dispatch_body(ps_ref, ei_ref, h_ref, xs_in_ref, xs_ref, sem, *, td):
    del xs_in_ref

    def one(r, carry):
        for k in range(TOP_K):
            pltpu.make_async_copy(h_ref.at[_tile_rows(r)], xs_ref.at[_tile_rows(_slot(ps_ref, ei_ref, k, r))],
                                  sem).start()
        return carry
    lax.fori_loop(0, td, one, 0, unroll=8)
    for k in range(TOP_K):
        pltpu.make_async_copy(h_ref, xs_ref.at[pl.ds(0, td * TOKEN_ROWS)], sem).wait()


def _dispatch(pstarts, ei, h2t, n_slots, td):
    n_tok = ei.shape[1]
    xs0 = jnp.zeros((n_slots * TOKEN_ROWS, LANE), F32)
    grid_spec = pltpu.PrefetchScalarGridSpec(
        num_scalar_prefetch=1,
        grid=(n_tok // td,),
        in_specs=[pl.BlockSpec((8, td), lambda i, ps: (0, i), memory_space=pltpu.SMEM),
                  pl.BlockSpec((td * TOKEN_ROWS, LANE), lambda i, ps: (i, 0)),
                  pl.BlockSpec(memory_space=pl.ANY)],
        out_specs=pl.BlockSpec(memory_space=pl.ANY),
        scratch_shapes=[pltpu.SemaphoreType.DMA(())],
    )
    return pl.pallas_call(
        functools.partial(_dispatch_body, td=td),
        grid_spec=grid_spec,
        out_shape=jax.ShapeDtypeStruct(xs0.shape, F32),
        input_output_aliases={3: 0},
        compiler_params=_cparams(("arbitrary",)),
        name="moe_dispatch",
    )(pstarts, ei, h2t, xs0)


def _experts_body(be_ref, bi_ref, used_ref, x_ref, w1_ref, w3_ref, w2_ref, y_ref, w1b, w3b, w2b, *, tb):
    i = pl.program_id(0)
    used = used_ref[i] > 0

    @pl.when(used)
    def _():
        prev_e = be_ref[jnp.maximum(i - 1, 0)]

        @pl.when((i == 0) | (prev_e != be_ref[i]))
        def _():
            w1b[...] = w1_ref[0, 0].astype(BF16)
            w3b[...] = w3_ref[0, 0].astype(BF16)
            w2b[...] = w2_ref[0, 0].astype(BF16)

        xb = jnp.concatenate([x_ref[pl.ds(s, tb, stride=TOKEN_ROWS), :] for s in range(TOKEN_ROWS)],
                             axis=1).astype(BF16)
        a = jnp.dot(xb, w1b[...], preferred_element_type=F32)
        b = jnp.dot(xb, w3b[...], preferred_element_type=F32)
        hm = (a * (1.0 / (1.0 + jnp.exp(-a))) * b).astype(BF16)
        y = jnp.dot(hm, w2b[...], preferred_element_type=F32)
        for s in range(TOKEN_ROWS):
            y_ref[pl.ds(s, tb, stride=TOKEN_ROWS), :] = y[:, s * LANE:(s + 1) * LANE]

    @pl.when(jnp.logical_not(used))
    def _():
        y_ref[...] = jnp.zeros(y_ref.shape, F32)


def _experts(xs, block_e, block_i, used, w1, w3, w2, layer, tb):
    nb = block_e.shape[0]
    wspec_up = pl.BlockSpec((1, 1, D_MODEL, D_EXPERT), lambda i, be, bi, us: (layer, be[i], 0, 0))
    grid_spec = pltpu.PrefetchScalarGridSpec(
        num_scalar_prefetch=3,
        grid=(nb,),
        in_specs=[pl.BlockSpec((tb * TOKEN_ROWS, LANE), lambda i, be, bi, us: (bi[i], 0)),
                  wspec_up, wspec_up,
                  pl.BlockSpec((1, 1, D_EXPERT, D_MODEL), lambda i, be, bi, us: (layer, be[i], 0, 0))],
        out_specs=pl.BlockSpec((tb * TOKEN_ROWS, LANE), lambda i, be, bi, us: (i, 0)),
        scratch_shapes=[pltpu.VMEM((D_MODEL, D_EXPERT), BF16), pltpu.VMEM((D_MODEL, D_EXPERT), BF16),
                        pltpu.VMEM((D_EXPERT, D_MODEL), BF16)],
    )
    return pl.pallas_call(
        functools.partial(_experts_body, tb=tb),
        grid_spec=grid_spec,
        out_shape=jax.ShapeDtypeStruct(xs.shape, F32),
        compiler_params=_cparams(("arbitrary",)),
        name="moe_experts",
    )(block_e, block_i, used, xs, w1, w3, w2)


def _combine_body(ps_ref, ei_ref, wf_ref, x_ref, g2_ref, ys_ref, o_ref, buf0, buf1, sem, *, tm):
    bufs = (buf0, buf1)

    def one(r, carry):
        for k in range(TOP_K):
            pltpu.make_async_copy(ys_ref.at[_tile_rows(_slot(ps_ref, ei_ref, k, r))], bufs[k].at[_tile_rows(r)],
                                  sem).start()
        return carry
    lax.fori_loop(0, tm, one, 0, unroll=8)
    for k in range(TOP_K):
        pltpu.make_async_copy(ys_ref.at[pl.ds(0, tm * TOKEN_ROWS)], bufs[k], sem).wait()
    wf = wf_ref[...]
    y = None
    for k in range(TOP_K):
        yk = jnp.concatenate([bufs[k][pl.ds(s, tm, stride=TOKEN_ROWS), :] for s in range(TOKEN_ROWS)], axis=1)
        yk = wf[:, k:k + 1] * yk
        y = yk if y is None else y + yk
    o_ref[0] = x_ref[0] + g2_ref[0] * y


def _combine(x, pstarts, ei, wf, ys, g2, row0, tm):
    bn, l, _ = x.shape
    nt = l // tm
    t0 = row0 // tm
    rowspec = pl.BlockSpec((1, tm, D_MODEL), lambda b, t, ps: (b, t, 0))
    grid_spec = pltpu.PrefetchScalarGridSpec(
        num_scalar_prefetch=1,
        grid=(bn, nt),
        in_specs=[pl.BlockSpec((8, tm), lambda b, t, ps: (0, t0 + b * nt + t), memory_space=pltpu.SMEM),
                  pl.BlockSpec((tm, LANE), lambda b, t, ps: (t0 + b * nt + t, 0)),
                  rowspec,
                  pl.BlockSpec((1, 1, D_MODEL), lambda b, t, ps: (b, 0, 0)),
                  pl.BlockSpec(memory_space=pl.ANY)],
        out_specs=rowspec,
        scratch_shapes=[pltpu.VMEM((tm * TOKEN_ROWS, LANE), F32), pltpu.VMEM((tm * TOKEN_ROWS, LANE), F32),
                        pltpu.SemaphoreType.DMA(())],
    )
    return pl.pallas_call(
        functools.partial(_combine_body, tm=tm),
        grid_spec=grid_spec,
        out_shape=jax.ShapeDtypeStruct(x.shape, F32),
        compiler_params=_cparams(("arbitrary", "arbitrary")),
        name="moe_combine",
    )(pstarts, ei, wf, x, g2, ys)


def _expert_blocks(counts, n_assign, tb):
    pcounts = (counts + tb - 1) // tb * tb
    pends = jnp.cumsum(pcounts)
    pstarts = (pends - pcounts).astype(jnp.int32)
    nb = -(-n_assign // tb) + N_EXPERTS
    n_used = pends[-1] // tb
    blk = jnp.arange(nb, dtype=jnp.int32)
    used = blk < n_used
    last = jnp.maximum(n_used - 1, 0).astype(jnp.int32)
    block_i = jnp.where(used, blk, last)
    block_e = jnp.minimum(jnp.searchsorted(pends, block_i * tb, side="right"), N_EXPERTS - 1).astype(jnp.int32)
    return pstarts, block_e, block_i, used.astype(jnp.int32), nb


_ROPE_PERM = np.concatenate([np.arange(8, 16), np.arange(0, 8), np.arange(24, 32), np.arange(16, 24)])
_ROPE_SIGN = np.concatenate([-np.ones(8), np.ones(8), -np.ones(8), np.ones(8)]).astype(np.float32)


def _rope_tables_t(l):
    half = C_ROPE // 2
    inv = ROPE_BASE ** (-jnp.arange(0, half, 2, dtype=F32) / half)
    pos = jnp.arange(l)
    ang_r = (pos // GRID_W).astype(F32)[None, :] * inv[:, None]
    ang_c = (pos % GRID_W).astype(F32)[None, :] * inv[:, None]
    cos_t = jnp.concatenate([jnp.cos(ang_r)] * 2 + [jnp.cos(ang_c)] * 2, axis=0)
    sin_t = jnp.concatenate([jnp.sin(ang_r)] * 2 + [jnp.sin(ang_c)] * 2, axis=0)
    return cos_t, sin_t


def _col(v):
    return v.astype(F32)[:, None]


def _layer_weights(i, p):
    w_in = p["w_in"][i]
    kr0 = IN_A + IN_B + C_Q_LORA + C_KV_LORA
    kr_cols = w_in[:, kr0:kr0 + C_ROPE]
    w_aug = jnp.concatenate([w_in, kr_cols[:, _ROPE_PERM] * _ROPE_SIGN], axis=1)

    wq = p["c_w_q_up"][i].reshape(C_Q_LORA, C_HEADS, C_QK)
    wq_n = wq[:, :, :C_NOPE].reshape(C_Q_LORA, C_HEADS * C_NOPE)
    wq_r = wq[:, :, C_NOPE:]
    wq_rp = (wq_r[:, :, _ROPE_PERM] * _ROPE_SIGN).reshape(C_Q_LORA, C_HEADS * C_ROPE)
    wq_all = jnp.concatenate([wq_n, wq_r.reshape(C_Q_LORA, C_HEADS * C_ROPE), wq_rp], axis=1)

    wkv = p["c_w_kv_up"][i].reshape(C_KV_LORA, C_HEADS, C_NOPE + C_VDIM)
    wkv_all = jnp.concatenate([wkv[:, :, :C_NOPE].reshape(C_KV_LORA, C_HEADS * C_NOPE),
                               wkv[:, :, C_NOPE:].reshape(C_KV_LORA, C_HEADS * C_VDIM)], axis=1)

    cqn = p["c_q_norm"][i].astype(F32) * (C_QK ** -0.5)
    ckn = p["c_k_norm"][i].astype(F32)
    w_route = jnp.concatenate([p["moe_w_group"][i], p["moe_w_router"][i],
                               jnp.zeros((D_MODEL, ROUTER_PAD - N_GROUPS - N_EXPERTS), F32)], axis=1)
    return {
        "norm_mix": p["norm_mix"][i][None, :],
        "w_in_t": w_aug.T.astype(BF16),
        "avn": _col(p["a_v_norm"][i]),
        "wst": jnp.transpose(p["a_w_s"][i], (0, 2, 1)).astype(BF16),
        "bs": p["a_b_s"][i][:, None, :],
        "gq": _col(p["b_q_norm"][i] * (HEAD_DIM ** -0.5)),
        "gk": _col(p["b_k_norm"][i]),
        "cqa": _col(p["c_q_a_norm"][i]),
        "wq_t": wq_all.T.astype(BF16),
        "ckva": _col(p["c_kv_a_norm"][i]),
        "wkv_t": wkv_all.T.astype(BF16),
        "gcqn": _col(cqn[:C_NOPE]),
        "gcqr": _col(cqn[C_NOPE:]),
        "gcqrp": _col(cqn[C_NOPE:][_ROPE_PERM]),
        "gckn": _col(ckn[:C_NOPE]),
        "gckr": _col(ckn[C_NOPE:]),
        "gckrp": _col(ckn[C_NOPE:][_ROPE_PERM]),
        "w_out_t": p["w_out"][i].T.astype(BF16),
        "norm_ffn": p["norm_ffn"][i][None, :],
        "w_route": w_route,
    }


def _pick_tile(l, pref):
    t = min(l, pref)
    while l % t:
        t //= 2
    return t


def kernel(x, c, ctx, c_ctx, w_ada, b_ada, norm_mix, w_in, a_v_norm, a_w_s, a_b_s, b_q_norm, b_k_norm, b_rpb,
           c_q_a_norm, c_w_q_up, c_kv_a_norm, c_w_kv_up, c_q_norm, c_k_norm, w_out, norm_ffn,
           moe_w_group, moe_w_router, moe_w1, moe_w3, moe_w2):
    p = dict(norm_mix=norm_mix, w_in=w_in, a_v_norm=a_v_norm, a_w_s=a_w_s, a_b_s=a_b_s, b_q_norm=b_q_norm,
             b_k_norm=b_k_norm, c_q_a_norm=c_q_a_norm, c_w_q_up=c_w_q_up, c_kv_a_norm=c_kv_a_norm,
             c_w_kv_up=c_w_kv_up, c_q_norm=c_q_norm, c_k_norm=c_k_norm, w_out=w_out, norm_ffn=norm_ffn,
             moe_w_group=moe_w_group, moe_w_router=moe_w_router)
    bn, l, d = x.shape
    nc = ctx.shape[1]
    depth = w_ada.shape[0]
    rows = l // GRID_W
    tm = _pick_tile(l, 512)
    tq = _pick_tile(l, MLA_QUERY_TILE)

    cos_t, sin_t = _rope_tables_t(l)
    cos_x = jnp.ones((C_ROPE, nc), F32)
    sin_x = jnp.zeros((C_ROPE, nc), F32)
    cc8 = jnp.concatenate([c, c_ctx[None, :], jnp.zeros((8 - bn - 1, d), F32)], axis=0)

    xc = ctx
    for i in range(depth):
        need_ctx = i < depth - 1
        lw = _layer_weights(i, p)
        mod = _ada(cc8, w_ada[i], b_ada[i][None, :])
        mods = [mod[:bn, j * d:(j + 1) * d][:, None, :] for j in range(6)]
        modx = [jnp.broadcast_to(mod[bn, j * d:(j + 1) * d][None, None, :], (bn, 1, d)) for j in range(6)]
        sh1, s1, g1, sh2, s2, g2 = mods
        sh1x, s1x, g1x, sh2x, s2x, g2x = modx

        oa, qb, kb, vb, qc, kc, vc = _mixin(x, sh1, s1, lw, cos_t, sin_t, tm)
        oax, qbx, kbx, vbx, qcx, kcx, vcx = _mixin(xc, sh1x, s1x, lw, cos_x, sin_x, nc)

        table = _na_bias_table(b_rpb[i], rows)
        ob = _na_attention(qb, kb, vb, kbx, vbx, table)
        oc = _mla_attention(qc, kc, vc, kcx, vcx, tq)
        n_lat = bn * l
        n_tok = n_lat + (bn * nc if need_ctx else 0)
        x, h2t, ei, wf, cnt = _mixout(x, oa, ob, oc, g1, sh2, s2, lw, tm, n_tok)
        if need_ctx:
            obx = _flash(qbx, kbx[:, None], vbx,
                         pl.BlockSpec((1, 1, nc, LANE), lambda b, h, ii, j: (b, 0, j, h // 2)),
                         tq=nc, tk=nc, name="ctx_dense_attention")
            ocx = _flash(qcx, kcx, vcx.reshape(bn, C_WIDTH, nc),
                         pl.BlockSpec((1, 1, nc, QPAD), lambda b, h, ii, j: (b, h, j, 0)),
                         tq=nc, tk=nc, name="ctx_mla_attention")
            xc, h2t, ei, wf, cnt = _mixout(xc, oax, obx, ocx, g1x, sh2x, s2x, lw, nc, n_tok, row0=n_lat,
                                           bufs=(h2t, ei, wf), cnt_in=cnt)

        counts = cnt[:, 0].astype(jnp.int32)
        pstarts, block_e, block_i, used, nb = _expert_blocks(counts, TOP_K * n_tok, MOE_BLOCK)
        xs = _dispatch(pstarts, ei, h2t, nb * MOE_BLOCK, tm)
        ys = _experts(xs, block_e, block_i, used, moe_w1, moe_w3, moe_w2, i, MOE_BLOCK)
        x = _combine(x, pstarts, ei, wf, ys, g2, 0, tm)
        if need_ctx:
            xc = _combine(xc, pstarts, ei, wf, ys, g2x, n_lat, nc)
    return x
```

```python
import functools

import numpy as np
import jax
import jax.numpy as jnp
from jax import lax
from jax.experimental import pallas as pl
from jax.experimental.pallas import tpu as pltpu

F32 = jnp.float32
BF16 = jnp.bfloat16

D_MODEL = 1024
GRID_W = 64
HEAD_DIM = 64
EPS = 1e-6
NEG_INF = -1e30

A_HEADS = 4
A_WIDTH = A_HEADS * HEAD_DIM
CHUNK = 128
B_HEADS = 6
B_WIDTH = B_HEADS * HEAD_DIM
NA_ROWS = 8
NA_COLS = 16
C_HEADS = 6
C_NOPE = 64
C_ROPE = 32
C_QK = C_NOPE + C_ROPE
C_VDIM = 64
C_Q_LORA = 384
C_KV_LORA = 256
C_WIDTH = C_HEADS * C_VDIM
ROPE_BASE = 10000.0
IN_A = 2 * A_WIDTH
IN_B = 3 * B_WIDTH
IN_C = C_Q_LORA + C_KV_LORA + C_ROPE
IN_AUG = IN_A + IN_B + IN_C + C_ROPE

N_GROUPS = 8
EXPERTS_PER_GROUP = 8
N_EXPERTS = N_GROUPS * EXPERTS_PER_GROUP
TOP_K = 2
D_EXPERT = D_MODEL // 2
ROUTER_PAD = 128

V7X_VMEM_LIMIT_BYTES = 56 * 1024 * 1024
LANE = 128
QPAD = 128

MLA_KEY_CHUNK = 256
MLA_QUERY_TILE = 1024
NA_QROWS = 8
NA_KBLK_ROWS = 4
MOE_BLOCK = 256

HIGHEST = lax.Precision.HIGHEST
LOG2E = 1.4426950408889634


def _cparams(sem):
    return pltpu.CompilerParams(dimension_semantics=sem, vmem_limit_bytes=V7X_VMEM_LIMIT_BYTES)


def _ada_body(c_ref, w_ref, b_ref, o_ref):
    cc = c_ref[...]
    s = cc * (1.0 / (1.0 + jnp.exp(-cc)))
    o_ref[...] = jnp.dot(s, w_ref[...], preferred_element_type=F32, precision=HIGHEST) + b_ref[...]


def _ada(cc8, w, b):
    n_out = w.shape[1]
    tn = 1024
    return pl.pallas_call(
        _ada_body,
        grid=(n_out // tn,),
        in_specs=[pl.BlockSpec((8, D_MODEL), lambda j: (0, 0)),
                  pl.BlockSpec((D_MODEL, tn), lambda j: (0, j)),
                  pl.BlockSpec((1, tn), lambda j: (0, j))],
        out_specs=pl.BlockSpec((8, tn), lambda j: (0, j)),
        out_shape=jax.ShapeDtypeStruct((8, n_out), F32),
        compiler_params=_cparams(("arbitrary",)),
        name="ada_mod",
    )(cc8, w, b)


def _rms_rows(v):
    return lax.rsqrt(jnp.mean(v * v, axis=0, keepdims=True) + EPS)


def _gelu_tanh(x):
    return 0.5 * x * (1.0 + jnp.tanh(0.7978845608028654 * (x + 0.044715 * (x * x * x))))


def _mixin_body(x_ref, sh_ref, sc_ref, gn_ref, wt_ref, avn_ref, wst_ref, bs_ref, gq_ref, gk_ref,
                cqa_ref, wq_ref, ckva_ref, wkv_ref, gcqn_ref, gcqr_ref, gcqrp_ref, gckn_ref, gckr_ref,
                gckrp_ref, cos_ref, sin_ref,
                oa_ref, qb_ref, kb_ref, vb_ref, qc_ref, kc_ref, vc_ref, *, tm):
    x = x_ref[0]
    h = x * lax.rsqrt(jnp.mean(x * x, axis=-1, keepdims=True) + EPS) * gn_ref[...]
    h = h * (1.0 + sc_ref[0]) + sh_ref[0]
    zt = lax.dot_general(wt_ref[...], h.astype(BF16), (((1,), (1,)), ((), ())),
                         preferred_element_type=F32)

    ga = _gelu_tanh(zt[0:IN_A])
    u = ga[0:A_WIDTH]
    v = ga[A_WIDTH:IN_A]
    vn = (v * _rms_rows(v) * avn_ref[...]).astype(BF16)
    for hd in range(A_HEADS):
        r0, r1 = hd * HEAD_DIM, (hd + 1) * HEAD_DIM
        for c in range(tm // CHUNK):
            c0, c1 = c * CHUNK, (c + 1) * CHUNK
            vm = jnp.dot(vn[r0:r1, c0:c1], wst_ref[hd], preferred_element_type=F32) + bs_ref[hd]
            oa_ref[0, r0:r1, c0:c1] = (u[r0:r1, c0:c1] * vm).astype(BF16)

    zb = zt[IN_A:IN_A + IN_B]
    zeros64 = jnp.zeros((HEAD_DIM, tm), F32)
    kn = []
    for hd in range(B_HEADS):
        q = zb[hd * HEAD_DIM:(hd + 1) * HEAD_DIM]
        k = zb[B_WIDTH + hd * HEAD_DIM:B_WIDTH + (hd + 1) * HEAD_DIM]
        qn = q * _rms_rows(q) * gq_ref[...]
        kn.append(k * _rms_rows(k) * gk_ref[...])
        pair = [qn, zeros64] if hd % 2 == 0 else [zeros64, qn]
        qb_ref[0, hd] = jnp.concatenate(pair, axis=0).astype(BF16)
    for p in range(B_HEADS // 2):
        kt = jnp.concatenate([kn[2 * p], kn[2 * p + 1]], axis=0)
        kb_ref[0, :, p * LANE:(p + 1) * LANE] = kt.T.astype(BF16)
    vb_ref[0] = zb[2 * B_WIDTH:3 * B_WIDTH].astype(BF16)

    zc = zt[IN_A + IN_B:IN_AUG]
    ql = zc[0:C_Q_LORA]
    kvl = zc[C_Q_LORA:C_Q_LORA + C_KV_LORA]
    kr = zc[C_Q_LORA + C_KV_LORA:C_Q_LORA + C_KV_LORA + C_ROPE]
    krp = zc[C_Q_LORA + C_KV_LORA + C_ROPE:C_Q_LORA + C_KV_LORA + 2 * C_ROPE]
    qln = (ql * _rms_rows(ql) * cqa_ref[...]).astype(BF16)
    kvln = (kvl * _rms_rows(kvl) * ckva_ref[...]).astype(BF16)
    qt = jnp.dot(wq_ref[...], qln, preferred_element_type=F32)
    kvt = jnp.dot(wkv_ref[...], kvln, preferred_element_type=F32)
    cos = cos_ref[...]
    sin = sin_ref[...]
    krn = _rms_rows(kr) * (gckr_ref[...] * kr * cos + gckrp_ref[...] * krp * sin)
    zeros32 = jnp.zeros((QPAD - C_QK, tm), F32)
    nq = C_HEADS * C_NOPE
    nr = C_HEADS * C_ROPE
    for hd in range(C_HEADS):
        qn = qt[hd * C_NOPE:(hd + 1) * C_NOPE]
        qn = qn * _rms_rows(qn) * gcqn_ref[...]
        qr = qt[nq + hd * C_ROPE:nq + (hd + 1) * C_ROPE]
        qrp = qt[nq + nr + hd * C_ROPE:nq + nr + (hd + 1) * C_ROPE]
        qrn = _rms_rows(qr) * (gcqr_ref[...] * qr * cos + gcqrp_ref[...] * qrp * sin)
        qc_ref[0, hd] = jnp.concatenate([qn, qrn, zeros32], axis=0).astype(BF16)
        kn_c = kvt[hd * C_NOPE:(hd + 1) * C_NOPE]
        kn_c = kn_c * _rms_rows(kn_c) * gckn_ref[...]
        kc_ref[0, hd] = jnp.concatenate([kn_c, krn, zeros32], axis=0).T.astype(BF16)
    vt = vc_ref.shape[3]
    v_rows = kvt[nq:nq + C_WIDTH].astype(BF16)
    for c in range(tm // vt):
        vc_ref[0, c] = v_rows[:, c * vt:(c + 1) * vt]


def _mixin(x, sh, sc, lw, cos_t, sin_t, tm):
    bn, l, _ = x.shape
    vt = min(MLA_KEY_CHUNK, tm)
    const2 = lambda b, t: (0, 0)
    const3 = lambda b, t: (0, 0, 0)
    in_specs = [
        pl.BlockSpec((1, tm, D_MODEL), lambda b, t: (b, t, 0)),
        pl.BlockSpec((1, 1, D_MODEL), lambda b, t: (b, 0, 0)),
        pl.BlockSpec((1, 1, D_MODEL), lambda b, t: (b, 0, 0)),
        pl.BlockSpec((1, D_MODEL), const2),
        pl.BlockSpec((IN_AUG, D_MODEL), const2),
        pl.BlockSpec((A_WIDTH, 1), const2),
        pl.BlockSpec((A_HEADS, CHUNK, CHUNK), const3),
        pl.BlockSpec((A_HEADS, 1, CHUNK), const3),
        pl.BlockSpec((HEAD_DIM, 1), const2),
        pl.BlockSpec((HEAD_DIM, 1), const2),
        pl.BlockSpec((C_Q_LORA, 1), const2),
        pl.BlockSpec((C_HEADS * (C_NOPE + 2 * C_ROPE), C_Q_LORA), const2),
        pl.BlockSpec((C_KV_LORA, 1), const2),
        pl.BlockSpec((C_HEADS * (C_NOPE + C_VDIM), C_KV_LORA), const2),
        pl.BlockSpec((C_NOPE, 1), const2),
        pl.BlockSpec((C_ROPE, 1), const2),
        pl.BlockSpec((C_ROPE, 1), const2),
        pl.BlockSpec((C_NOPE, 1), const2),
        pl.BlockSpec((C_ROPE, 1), const2),
        pl.BlockSpec((C_ROPE, 1), const2),
        pl.BlockSpec((C_ROPE, tm), lambda b, t: (0, t)),
        pl.BlockSpec((C_ROPE, tm), lambda b, t: (0, t)),
    ]
    out_shape = (
        jax.ShapeDtypeStruct((bn, A_WIDTH, l), BF16),
        jax.ShapeDtypeStruct((bn, B_HEADS, QPAD, l), BF16),
        jax.ShapeDtypeStruct((bn, l, B_WIDTH), BF16),
        jax.ShapeDtypeStruct((bn, B_WIDTH, l), BF16),
        jax.ShapeDtypeStruct((bn, C_HEADS, QPAD, l), BF16),
        jax.ShapeDtypeStruct((bn, C_HEADS, l, QPAD), BF16),
        jax.ShapeDtypeStruct((bn, l // vt, C_WIDTH, vt), BF16),
    )
    out_specs = (
        pl.BlockSpec((1, A_WIDTH, tm), lambda b, t: (b, 0, t)),
        pl.BlockSpec((1, B_HEADS, QPAD, tm), lambda b, t: (b, 0, 0, t)),
        pl.BlockSpec((1, tm, B_WIDTH), lambda b, t: (b, t, 0)),
        pl.BlockSpec((1, B_WIDTH, tm), lambda b, t: (b, 0, t)),
        pl.BlockSpec((1, C_HEADS, QPAD, tm), lambda b, t: (b, 0, 0, t)),
        pl.BlockSpec((1, C_HEADS, tm, QPAD), lambda b, t: (b, 0, t, 0)),
        pl.BlockSpec((1, tm // vt, C_WIDTH, vt), lambda b, t: (b, t, 0, 0)),
    )
    return pl.pallas_call(
        functools.partial(_mixin_body, tm=tm),
        grid=(bn, l // tm),
        in_specs=in_specs,
        out_specs=out_specs,
        out_shape=out_shape,
        compiler_params=_cparams(("arbitrary", "arbitrary")),
        name="mix_in",
    )(x, sh, sc, lw["norm_mix"], lw["w_in_t"], lw["avn"], lw["wst"], lw["bs"], lw["gq"], lw["gk"],
      lw["cqa"], lw["wq_t"], lw["ckva"], lw["wkv_t"], lw["gcqn"], lw["gcqr"], lw["gcqrp"], lw["gckn"],
      lw["gckr"], lw["gckrp"], cos_t, sin_t)


def _na_body(q_ref, k0_ref, k1_ref, k2_ref, k3_ref, v0_ref, v1_ref, v2_ref, v3_ref, kx_ref, vx_ref, t_ref,
             o_ref):
    k_refs = (k0_ref, k1_ref, k2_ref, k3_ref)
    v_refs = (v0_ref, v1_ref, v2_ref, v3_ref)
    nk = k0_ref.shape[1]
    for hh in range(2):
        q = q_ref[0, hh]
        rows = slice(hh * HEAD_DIM, (hh + 1) * HEAD_DIM)
        s = jnp.concatenate([jnp.dot(kr[0], q, preferred_element_type=F32) for kr in k_refs], axis=0)
        s = (s + t_ref[hh, 0]).astype(BF16)
        sx = jnp.dot(kx_ref[0], q, preferred_element_type=F32).astype(BF16)
        m = jnp.maximum(jnp.max(s, axis=0, keepdims=True), jnp.max(sx, axis=0, keepdims=True))
        p = jnp.exp2(s - m)
        px = jnp.exp2(sx - m)
        den = (jnp.sum(p.astype(F32), axis=0, keepdims=True) + jnp.sum(px.astype(F32), axis=0, keepdims=True))
        o = jnp.dot(vx_ref[0, rows], px, preferred_element_type=F32)
        for j, vr in enumerate(v_refs):
            o = o + jnp.dot(vr[0, rows], p[j * nk:(j + 1) * nk], preferred_element_type=F32)
        o_ref[0, rows] = (o / den).astype(BF16)


def _na_attention(qb, kb, vb, kxb, vxb, table):
    bn, _, _, l = qb.shape
    nq = NA_QROWS * GRID_W
    nk = NA_KBLK_ROWS * GRID_W
    nblk = l // nq
    nkb = l // nk
    nctx = kxb.shape[1]

    def kmap(j):
        return lambda b, h, i: (b, jnp.clip(2 * i - 1 + j, 0, nkb - 1), h)

    def vmap_(j):
        return lambda b, h, i: (b, h, jnp.clip(2 * i - 1 + j, 0, nkb - 1))

    def tmap(b, h, i):
        return (h, jnp.where(i == 0, 0, jnp.where(i == nblk - 1, 2, 1)), 0, 0)

    in_specs = ([pl.BlockSpec((1, 2, QPAD, nq), lambda b, h, i: (b, h, 0, i))]
                + [pl.BlockSpec((1, nk, LANE), kmap(j)) for j in range(4)]
                + [pl.BlockSpec((1, 2 * HEAD_DIM, nk), vmap_(j)) for j in range(4)]
                + [pl.BlockSpec((1, nctx, LANE), lambda b, h, i: (b, 0, h)),
                   pl.BlockSpec((1, 2 * HEAD_DIM, nctx), lambda b, h, i: (b, h, 0)),
                   pl.BlockSpec((2, 1, 4 * nk, nq), tmap)])
    return pl.pallas_call(
        _na_body,
        grid=(bn, B_HEADS // 2, nblk),
        in_specs=in_specs,
        out_specs=pl.BlockSpec((1, 2 * HEAD_DIM, nq), lambda b, h, i: (b, h, i)),
        out_shape=jax.ShapeDtypeStruct((bn, B_WIDTH, l), BF16),
        compiler_params=_cparams(("arbitrary", "arbitrary", "arbitrary")),
        name="na_attention",
    )(qb, kb, kb, kb, kb, vb, vb, vb, vb, kxb, vxb, table)


def _na_bias_table(rpb, rows):
    nblk = rows // NA_QROWS
    qc = np.arange(GRID_W)
    kc = np.arange(GRID_W)
    c0 = np.clip(qc - NA_COLS // 2, 0, GRID_W - NA_COLS)
    valid_col = (kc[:, None] >= c0[None, :]) & (kc[:, None] < c0[None, :] + NA_COLS)
    dc = np.clip(kc[:, None] - qc[None, :], -(NA_COLS - 1), NA_COLS - 1) + NA_COLS - 1
    dc_onehot = (dc[None] == np.arange(2 * NA_COLS - 1)[:, None, None]).astype(np.float32)
    tabs = []
    for i in (0, 1, nblk - 1):
        kr = NA_KBLK_ROWS * (2 * i - 1) + np.arange(4 * NA_KBLK_ROWS)
        qr = NA_QROWS * i + np.arange(NA_QROWS)
        r0 = np.clip(qr - NA_ROWS // 2, 0, rows - NA_ROWS)
        valid_row = ((kr[:, None] >= r0[None, :]) & (kr[:, None] < r0[None, :] + NA_ROWS)
                     & (kr[:, None] >= 0) & (kr[:, None] < rows))
        dr = np.clip(kr[:, None] - qr[None, :] + NA_ROWS - 1, 0, 2 * NA_ROWS - 2)
        bias = jnp.einsum("hkqd,dcx->hkcqx", rpb[:, dr].astype(F32), dc_onehot, precision=HIGHEST)
        valid = valid_row[:, None, :, None] & valid_col[None, :, None, :]
        tabs.append(jnp.where(valid[None], bias.astype(F32) * LOG2E, NEG_INF).reshape(
            rpb.shape[0], 4 * NA_KBLK_ROWS * GRID_W, NA_QROWS * GRID_W))
    return jnp.stack(tabs, axis=1)


def _flash_body(*refs, has_extra):
    if has_extra:
        q_ref, k_ref, v_ref, kx_ref, vx_ref, o_ref, m_sc, l_sc, acc_sc = refs
    else:
        q_ref, k_ref, v_ref, o_ref, m_sc, l_sc, acc_sc = refs
    kv = pl.program_id(3)
    nkv = pl.num_programs(3)

    @pl.when(kv == 0)
    def _():
        m_sc[...] = jnp.full(m_sc.shape, -jnp.inf, F32)
        l_sc[...] = jnp.zeros(l_sc.shape, F32)
        acc_sc[...] = jnp.zeros(acc_sc.shape, F32)

    def step(k, v):
        s = jnp.dot(k, q_ref[0, 0], preferred_element_type=F32)
        m_old = m_sc[...]
        m_new = jnp.maximum(m_old, jnp.max(s, axis=0, keepdims=True))
        alpha = jnp.exp2(m_old - m_new)
        p = jnp.exp2(s - m_new)
        l_sc[...] = alpha * l_sc[...] + jnp.sum(p, axis=0, keepdims=True)
        acc_sc[...] = alpha * acc_sc[...] + jnp.dot(v, p.astype(BF16), preferred_element_type=F32)
        m_sc[...] = m_new

    step(k_ref[0, 0], v_ref[0])

    @pl.when(kv == nkv - 1)
    def _():
        if has_extra:
            step(kx_ref[0, 0], vx_ref[0])
        o_ref[0] = (acc_sc[...] / l_sc[...]).astype(BF16)


def _flash(q, k, v, k_spec, kx=None, vx=None, kx_spec=None, *, tq, tk, name):
    bn, nh, _, lq = q.shape
    lk = v.shape[2]
    has_extra = kx is not None
    in_specs = [pl.BlockSpec((1, 1, QPAD, tq), lambda b, h, i, j: (b, h, 0, i)),
                k_spec,
                pl.BlockSpec((1, HEAD_DIM, tk), lambda b, h, i, j: (b, h, j))]
    args = [q, k, v]
    if has_extra:
        nx = vx.shape[2]
        in_specs += [kx_spec, pl.BlockSpec((1, HEAD_DIM, nx), lambda b, h, i, j: (b, h, 0))]
        args += [kx, vx]
    return pl.pallas_call(
        functools.partial(_flash_body, has_extra=has_extra),
        grid=(bn, nh, lq // tq, lk // tk),
        in_specs=in_specs,
        out_specs=pl.BlockSpec((1, HEAD_DIM, tq), lambda b, h, i, j: (b, h, i)),
        out_shape=jax.ShapeDtypeStruct((bn, nh * HEAD_DIM, lq), BF16),
        scratch_shapes=[pltpu.VMEM((1, tq), F32), pltpu.VMEM((1, tq), F32), pltpu.VMEM((HEAD_DIM, tq), F32)],
        compiler_params=_cparams(("arbitrary", "arbitrary", "arbitrary", "arbitrary")),
        name=name,
    )(*args)


def _mla_body(q_ref, k_ref, v_ref, kx_ref, vx_ref, o_ref, *, tk, nchunks):
    q = q_ref[0, 0]
    tq = q.shape[1]

    def scores(k):
        return jnp.dot(k, q, preferred_element_type=F32).astype(BF16)

    def absorb(s, v, carry):
        m, den, acc = carry
        m_new = jnp.maximum(m, jnp.max(s, axis=0, keepdims=True).astype(F32))
        alpha = jnp.exp2(m - m_new)
        p = jnp.exp2(s - m_new.astype(BF16))
        den = alpha * den + jnp.sum(p.astype(F32), axis=0, keepdims=True)
        acc = alpha * acc + jnp.dot(v, p, preferred_element_type=F32)
        return m_new, den, acc

    carry = (jnp.full((1, tq), -jnp.inf, F32), jnp.zeros((1, tq), F32), jnp.zeros((HEAD_DIM, tq), F32))
    s_cur = scores(k_ref[0, 0, 0:tk, :])
    for j in range(nchunks):
        if j + 1 < nchunks:
            s_next = scores(k_ref[0, 0, (j + 1) * tk:(j + 2) * tk, :])
        else:
            s_next = scores(kx_ref[0, 0])
        carry = absorb(s_cur, v_ref[0, j], carry)
        s_cur = s_next
    _, den, acc = absorb(s_cur, vx_ref[0, 0], carry)
    o_ref[0] = (acc / den).astype(BF16)


def _mla_attention(q, k, v, kx, vx, tq):
    bn, nh, _, l = q.shape
    nchunks, tk = v.shape[1], v.shape[3]
    nc = kx.shape[2]
    return pl.pallas_call(
        functools.partial(_mla_body, tk=tk, nchunks=nchunks),
        grid=(bn, nh, l // tq),
        in_specs=[pl.BlockSpec((1, 1, QPAD, tq), lambda b, h, i: (b, h, 0, i)),
                  pl.BlockSpec((1, 1, l, QPAD), lambda b, h, i: (b, h, 0, 0)),
                  pl.BlockSpec((1, nchunks, HEAD_DIM, tk), lambda b, h, i: (b, 0, h, 0)),
                  pl.BlockSpec((1, 1, nc, QPAD), lambda b, h, i: (b, h, 0, 0)),
                  pl.BlockSpec((1, 1, HEAD_DIM, nc), lambda b, h, i: (b, 0, h, 0))],
        out_specs=pl.BlockSpec((1, HEAD_DIM, tq), lambda b, h, i: (b, h, i)),
        out_shape=jax.ShapeDtypeStruct((bn, nh * HEAD_DIM, l), BF16),
        compiler_params=_cparams(("arbitrary", "arbitrary", "arbitrary")),
        name="mla_attention",
    )(q, k, v, kx, vx)


TOKEN_ROWS = 8


def _first_argmax_rows(v, row_id):
    vmax = jnp.max(v, axis=0, keepdims=True)
    idx = jnp.min(jnp.where(v == vmax, row_id, float(v.shape[0])), axis=0, keepdims=True)
    return vmax, idx


def _mixout_body(x_ref, oa_ref, ob_ref, oc_ref, wt_ref, g1_ref, sh2_ref, sc2_ref, gn2_ref, wr_ref, tri_ref,
                 *rest, tm, steps, has_base):
    if has_base:
        cnt_in_ref = rest[0]
    xo_ref, h2_ref, ei_ref, wf_ref, cnt_ref, base_sc = rest[-6:]
    i = pl.program_id(0)

    @pl.when(i == 0)
    def _():
        if has_base:
            base_sc[...] = cnt_in_ref[:, 0:1]
        else:
            base_sc[...] = jnp.zeros(base_sc.shape, F32)

    ot = jnp.concatenate([oa_ref[0], ob_ref[0], oc_ref[0]], axis=0)
    out_t = jnp.dot(wt_ref[...], ot, preferred_element_type=F32)
    xn = x_ref[0] + g1_ref[0] * out_t.T
    xo_ref[0] = xn
    h2 = xn * lax.rsqrt(jnp.mean(xn * xn, axis=-1, keepdims=True) + EPS) * gn2_ref[...]
    h2 = h2 * (1.0 + sc2_ref[0]) + sh2_ref[0]
    for s in range(TOKEN_ROWS):
        h2_ref[pl.ds(s, tm, stride=TOKEN_ROWS), :] = h2[:, s * LANE:(s + 1) * LANE]

    lt = jnp.dot(h2, wr_ref[...], preferred_element_type=F32, precision=HIGHEST).T
    gl = lt[0:N_GROUPS]
    rid = lax.broadcasted_iota(jnp.int32, (N_GROUPS, tm), 0).astype(F32)
    gmax, g_idx = _first_argmax_rows(gl, rid)
    g_gate = 1.0 / jnp.sum(jnp.exp(gl - gmax), axis=0, keepdims=True)
    e_sel = jnp.zeros((EXPERTS_PER_GROUP, tm), F32)
    for g in range(N_GROUPS):
        lo = N_GROUPS + g * EXPERTS_PER_GROUP
        e_sel = jnp.where(g_idx == float(g), lt[lo:lo + EXPERTS_PER_GROUP], e_sel)
    v1, j1 = _first_argmax_rows(e_sel, rid)
    v2, j2 = _first_argmax_rows(jnp.where(rid == j1, -jnp.inf, e_sel), rid)
    t21 = jnp.exp(v2 - v1)
    w1 = g_gate / (1.0 + t21)
    w2 = g_gate * t21 / (1.0 + t21)
    e1 = g_idx * float(EXPERTS_PER_GROUP) + j1
    e2 = g_idx * float(EXPERTS_PER_GROUP) + j2

    eid = lax.broadcasted_iota(jnp.int32, (N_EXPERTS, tm), 0).astype(F32)
    oh1 = (eid == e1).astype(F32)
    oh2 = (eid == e2).astype(F32)
    tri = tri_ref[...]
    cum1 = jnp.dot(oh1.astype(BF16), tri, preferred_element_type=F32)
    cum2 = jnp.dot(oh2.astype(BF16), tri, preferred_element_type=F32)
    tot1 = jnp.sum(oh1, axis=1, keepdims=True)
    tot2 = jnp.sum(oh2, axis=1, keepdims=True)
    base = base_sc[...]
    r1 = jnp.sum(oh1 * (base + cum1), axis=0, keepdims=True)
    r2 = jnp.sum(oh2 * (base + tot1 + cum2), axis=0, keepdims=True)
    live = jnp.where(i < steps, 1.0, 0.0)
    base_new = base + live * (tot1 + tot2)
    base_sc[...] = base_new
    cnt_ref[...] = jnp.broadcast_to(base_new, cnt_ref.shape)

    zeros4 = jnp.zeros((4, tm), F32)
    ei_ref[...] = jnp.concatenate([e1, e2, r1, r2, zeros4], axis=0).astype(jnp.int32)
    wpad = jnp.concatenate([w1, w2, jnp.zeros((LANE - 2, tm), F32)], axis=0)
    wf_ref[...] = wpad.T


def _mixout(x, oa, ob, oc, g1, sh2, sc2, lw, tm, n_rows, row0=0, bufs=None, cnt_in=None):
    bn, l, _ = x.shape
    nt = l // tm
    steps = bn * nt
    extra = 0
    if bufs is None and n_rows > bn * l:
        assert n_rows - bn * l == tm, "spare rows must be exactly one tile"
        extra = 1
    blk0 = row0 // tm

    def bt(i):
        ii = jnp.minimum(i, steps - 1)
        return ii // nt, ii % nt

    const2 = lambda i: (0, 0)
    modspec = pl.BlockSpec((1, 1, D_MODEL), lambda i: (bt(i)[0], 0, 0))
    rowspec = pl.BlockSpec((1, tm, D_MODEL), lambda i: (bt(i)[0], bt(i)[1], 0))
    tri = jnp.asarray(np.triu(np.ones((tm, tm), np.float32), k=1), BF16)
    in_specs = [rowspec,
                pl.BlockSpec((1, A_WIDTH, tm), lambda i: (bt(i)[0], 0, bt(i)[1])),
                pl.BlockSpec((1, B_WIDTH, tm), lambda i: (bt(i)[0], 0, bt(i)[1])),
                pl.BlockSpec((1, C_WIDTH, tm), lambda i: (bt(i)[0], 0, bt(i)[1])),
                pl.BlockSpec((D_MODEL, D_MODEL), const2),
                modspec, modspec, modspec,
                pl.BlockSpec((1, D_MODEL), const2),
                pl.BlockSpec((D_MODEL, ROUTER_PAD), const2),
                pl.BlockSpec((tm, tm), const2)]
    args = [x, oa, ob, oc, lw["w_out_t"], g1, sh2, sc2, lw["norm_ffn"], lw["w_route"], tri]
    aliases = {}
    if bufs is not None:
        in_specs.append(pl.BlockSpec((N_EXPERTS, LANE), const2))
        args.append(cnt_in)
        aliases = {len(args): 1, len(args) + 1: 2, len(args) + 2: 3}
        in_specs += [pl.BlockSpec(memory_space=pl.ANY)] * 3
        args += list(bufs)
    return pl.pallas_call(
        functools.partial(_mixout_body, tm=tm, steps=steps, has_base=bufs is not None),
        grid=(steps + extra,),
        in_specs=in_specs,
        out_specs=(rowspec,
                   pl.BlockSpec((tm * TOKEN_ROWS, LANE), lambda i: (blk0 + i, 0)),
                   pl.BlockSpec((8, tm), lambda i: (0, blk0 + i)),
                   pl.BlockSpec((tm, LANE), lambda i: (blk0 + i, 0)),
                   pl.BlockSpec((N_EXPERTS, LANE), const2)),
        out_shape=(jax.ShapeDtypeStruct((bn, l, D_MODEL), F32),
                   jax.ShapeDtypeStruct((n_rows * TOKEN_ROWS, LANE), F32),
                   jax.ShapeDtypeStruct((8, n_rows), jnp.int32),
                   jax.ShapeDtypeStruct((n_rows, LANE), F32),
                   jax.ShapeDtypeStruct((N_EXPERTS, LANE), F32)),
        scratch_shapes=[pltpu.VMEM((N_EXPERTS, 1), F32)],
        input_output_aliases=aliases,
        compiler_params=_cparams(("arbitrary",)),
        name="mix_out",
    )(*args)


def _tile_rows(idx):
    return pl.ds(pl.multiple_of(idx * TOKEN_ROWS, TOKEN_ROWS), TOKEN_ROWS)


def _slot(ps_ref, ei_ref, k, r):
    return ps_ref[ei_ref[k, r]] + ei_ref[TOP_K + k, r]


def _dispatch_body(ps_ref, ei_ref, h_ref, xs_in_ref, xs_ref, sem, *, td):
    del xs_in_ref

    def one(r, carry):
        for k in range(TOP_K):
            pltpu.make_async_copy(h_ref.at[_tile_rows(r)], xs_ref.at[_tile_rows(_slot(ps_ref, ei_ref, k, r))],
                                  sem).start()
        return carry
    lax.fori_loop(0, td, one, 0, unroll=8)
    for k in range(TOP_K):
        pltpu.make_async_copy(h_ref, xs_ref.at[pl.ds(0, td * TOKEN_ROWS)], sem).wait()


def _dispatch(pstarts, ei, h2t, n_slots, td):
    n_tok = ei.shape[1]
    xs0 = jnp.zeros((n_slots * TOKEN_ROWS, LANE), F32)
    grid_spec = pltpu.PrefetchScalarGridSpec(
        num_scalar_prefetch=1,
        grid=(n_tok // td,),
        in_specs=[pl.BlockSpec((8, td), lambda i, ps: (0, i), memory_space=pltpu.SMEM),
                  pl.BlockSpec((td * TOKEN_ROWS, LANE), lambda i, ps: (i, 0)),
                  pl.BlockSpec(memory_space=pl.ANY)],
        out_specs=pl.BlockSpec(memory_space=pl.ANY),
        scratch_shapes=[pltpu.SemaphoreType.DMA(())],
    )
    return pl.pallas_call(
        functools.partial(_dispatch_body, td=td),
        grid_spec=grid_spec,
        out_shape=jax.ShapeDtypeStruct(xs0.shape, F32),
        input_output_aliases={3: 0},
        compiler_params=_cparams(("arbitrary",)),
        name="moe_dispatch",
    )(pstarts, ei, h2t, xs0)


def _experts_body(be_ref, bi_ref, used_ref, x_ref, w1_ref, w3_ref, w2_ref, y_ref, w1b, w3b, w2b, *, tb):
    i = pl.program_id(0)
    used = used_ref[i] > 0

    @pl.when(used)
    def _():
        prev_e = be_ref[jnp.maximum(i - 1, 0)]

        @pl.when((i == 0) | (prev_e != be_ref[i]))
        def _():
            w1b[...] = w1_ref[0, 0].astype(BF16)
            w3b[...] = w3_ref[0, 0].astype(BF16)
            w2b[...] = w2_ref[0, 0].astype(BF16)

        xb = jnp.concatenate([x_ref[pl.ds(s, tb, stride=TOKEN_ROWS), :] for s in range(TOKEN_ROWS)],
                             axis=1).astype(BF16)
        a = jnp.dot(xb, w1b[...], preferred_element_type=F32)
        b = jnp.dot(xb, w3b[...], preferred_element_type=F32)
        hm = (a * (1.0 / (1.0 + jnp.exp(-a))) * b).astype(BF16)
        y = jnp.dot(hm, w2b[...], preferred_element_type=F32)
        for s in range(TOKEN_ROWS):
            y_ref[pl.ds(s, tb, stride=TOKEN_ROWS), :] = y[:, s * LANE:(s + 1) * LANE]

    @pl.when(jnp.logical_not(used))
    def _():
        y_ref[...] = jnp.zeros(y_ref.shape, F32)


def _experts(xs, block_e, block_i, used, w1, w3, w2, layer, tb):
    nb = block_e.shape[0]
    wspec_up = pl.BlockSpec((1, 1, D_MODEL, D_EXPERT), lambda i, be, bi, us: (layer, be[i], 0, 0))
    grid_spec = pltpu.PrefetchScalarGridSpec(
        num_scalar_prefetch=3,
        grid=(nb,),
        in_specs=[pl.BlockSpec((tb * TOKEN_ROWS, LANE), lambda i, be, bi, us: (bi[i], 0)),
                  wspec_up, wspec_up,
                  pl.BlockSpec((1, 1, D_EXPERT, D_MODEL), lambda i, be, bi, us: (layer, be[i], 0, 0))],
        out_specs=pl.BlockSpec((tb * TOKEN_ROWS, LANE), lambda i, be, bi, us: (i, 0)),
        scratch_shapes=[pltpu.VMEM((D_MODEL, D_EXPERT), BF16), pltpu.VMEM((D_MODEL, D_EXPERT), BF16),
                        pltpu.VMEM((D_EXPERT, D_MODEL), BF16)],
    )
    return pl.pallas_call(
        functools.partial(_experts_body, tb=tb),
        grid_spec=grid_spec,
        out_shape=jax.ShapeDtypeStruct(xs.shape, F32),
        compiler_params=_cparams(("arbitrary",)),
        name="moe_experts",
    )(block_e, block_i, used, xs, w1, w3, w2)


def _combine_body(ps_ref, ei_ref, wf_ref, x_ref, g2_ref, ys_ref, o_ref, buf0, buf1, sem, *, tm):
    bufs = (buf0, buf1)

    def one(r, carry):
        for k in range(TOP_K):
            pltpu.make_async_copy(ys_ref.at[_tile_rows(_slot(ps_ref, ei_ref, k, r))], bufs[k].at[_tile_rows(r)],
                                  sem).start()
        return carry
    lax.fori_loop(0, tm, one, 0, unroll=8)
    for k in range(TOP_K):
        pltpu.make_async_copy(ys_ref.at[pl.ds(0, tm * TOKEN_ROWS)], bufs[k], sem).wait()
    wf = wf_ref[...]
    y = None
    for k in range(TOP_K):
        yk = jnp.concatenate([bufs[k][pl.ds(s, tm, stride=TOKEN_ROWS), :] for s in range(TOKEN_ROWS)], axis=1)
        yk = wf[:, k:k + 1] * yk
        y = yk if y is None else y + yk
    o_ref[0] = x_ref[0] + g2_ref[0] * y


def _combine(x, pstarts, ei, wf, ys, g2, row0, tm):
    bn, l, _ = x.shape
    nt = l // tm
    t0 = row0 // tm
    rowspec = pl.BlockSpec((1, tm, D_MODEL), lambda b, t, ps: (b, t, 0))
    grid_spec = pltpu.PrefetchScalarGridSpec(
        num_scalar_prefetch=1,
        grid=(bn, nt),
        in_specs=[pl.BlockSpec((8, tm), lambda b, t, ps: (0, t0 + b * nt + t), memory_space=pltpu.SMEM),
                  pl.BlockSpec((tm, LANE), lambda b, t, ps: (t0 + b * nt + t, 0)),
                  rowspec,
                  pl.BlockSpec((1, 1, D_MODEL), lambda b, t, ps: (b, 0, 0)),
                  pl.BlockSpec(memory_space=pl.ANY)],
        out_specs=rowspec,
        scratch_shapes=[pltpu.VMEM((tm * TOKEN_ROWS, LANE), F32), pltpu.VMEM((tm * TOKEN_ROWS, LANE), F32),
                        pltpu.SemaphoreType.DMA(())],
    )
    return pl.pallas_call(
        functools.partial(_combine_body, tm=tm),
        grid_spec=grid_spec,
        out_shape=jax.ShapeDtypeStruct(x.shape, F32),
        compiler_params=_cparams(("arbitrary", "arbitrary")),
        name="moe_combine",
    )(pstarts, ei, wf, x, g2, ys)


def _expert_blocks(counts, n_assign, tb):
    pcounts = (counts + tb - 1) // tb * tb
    pends = jnp.cumsum(pcounts)
    pstarts = (pends - pcounts).astype(jnp.int32)
    nb = -(-n_assign // tb) + N_EXPERTS
    n_used = pends[-1] // tb
    blk = jnp.arange(nb, dtype=jnp.int32)
    used = blk < n_used
    last = jnp.maximum(n_used - 1, 0).astype(jnp.int32)
    block_i = jnp.where(used, blk, last)
    block_e = jnp.sum((pends[None, :] <= (block_i * tb)[:, None]).astype(jnp.int32), axis=1)
    block_e = jnp.minimum(block_e, N_EXPERTS - 1)
    return pstarts, block_e, block_i, used.astype(jnp.int32), nb


_ROPE_PERM = np.concatenate([np.arange(8, 16), np.arange(0, 8), np.arange(24, 32), np.arange(16, 24)])
_ROPE_SIGN = np.concatenate([-np.ones(8), np.ones(8), -np.ones(8), np.ones(8)]).astype(np.float32)


def _rope_tables_t(l):
    half = C_ROPE // 2
    inv = ROPE_BASE ** (-jnp.arange(0, half, 2, dtype=F32) / half)
    pos = jnp.arange(l)
    ang_r = (pos // GRID_W).astype(F32)[None, :] * inv[:, None]
    ang_c = (pos % GRID_W).astype(F32)[None, :] * inv[:, None]
    cos_t = jnp.concatenate([jnp.cos(ang_r)] * 2 + [jnp.cos(ang_c)] * 2, axis=0)
    sin_t = jnp.concatenate([jnp.sin(ang_r)] * 2 + [jnp.sin(ang_c)] * 2, axis=0)
    return cos_t, sin_t


def _col(v):
    return v.astype(F32)[:, None]


def _layer_weights(i, p):
    w_in = p["w_in"][i]
    kr0 = IN_A + IN_B + C_Q_LORA + C_KV_LORA
    kr_cols = w_in[:, kr0:kr0 + C_ROPE]
    w_aug = jnp.concatenate([w_in, kr_cols[:, _ROPE_PERM] * _ROPE_SIGN], axis=1)

    wq = p["c_w_q_up"][i].reshape(C_Q_LORA, C_HEADS, C_QK)
    wq_n = wq[:, :, :C_NOPE].reshape(C_Q_LORA, C_HEADS * C_NOPE)
    wq_r = wq[:, :, C_NOPE:]
    wq_rp = (wq_r[:, :, _ROPE_PERM] * _ROPE_SIGN).reshape(C_Q_LORA, C_HEADS * C_ROPE)
    wq_all = jnp.concatenate([wq_n, wq_r.reshape(C_Q_LORA, C_HEADS * C_ROPE), wq_rp], axis=1)

    wkv = p["c_w_kv_up"][i].reshape(C_KV_LORA, C_HEADS, C_NOPE + C_VDIM)
    wkv_all = jnp.concatenate([wkv[:, :, :C_NOPE].reshape(C_KV_LORA, C_HEADS * C_NOPE),
                               wkv[:, :, C_NOPE:].reshape(C_KV_LORA, C_HEADS * C_VDIM)], axis=1)

    cqn = p["c_q_norm"][i].astype(F32) * (C_QK ** -0.5 * LOG2E)
    ckn = p["c_k_norm"][i].astype(F32)
    w_route = jnp.concatenate([p["moe_w_group"][i], p["moe_w_router"][i],
                               jnp.zeros((D_MODEL, ROUTER_PAD - N_GROUPS - N_EXPERTS), F32)], axis=1)
    return {
        "norm_mix": p["norm_mix"][i][None, :],
        "w_in_t": w_aug.T.astype(BF16),
        "avn": _col(p["a_v_norm"][i]),
        "wst": jnp.transpose(p["a_w_s"][i], (0, 2, 1)).astype(BF16),
        "bs": p["a_b_s"][i][:, None, :],
        "gq": _col(p["b_q_norm"][i] * (HEAD_DIM ** -0.5 * LOG2E)),
        "gk": _col(p["b_k_norm"][i]),
        "cqa": _col(p["c_q_a_norm"][i]),
        "wq_t": wq_all.T.astype(BF16),
        "ckva": _col(p["c_kv_a_norm"][i]),
        "wkv_t": wkv_all.T.astype(BF16),
        "gcqn": _col(cqn[:C_NOPE]),
        "gcqr": _col(cqn[C_NOPE:]),
        "gcqrp": _col(cqn[C_NOPE:][_ROPE_PERM]),
        "gckn": _col(ckn[:C_NOPE]),
        "gckr": _col(ckn[C_NOPE:]),
        "gckrp": _col(ckn[C_NOPE:][_ROPE_PERM]),
        "w_out_t": p["w_out"][i].T.astype(BF16),
        "norm_ffn": p["norm_ffn"][i][None, :],
        "w_route": w_route,
    }


def _pick_tile(l, pref):
    t = min(l, pref)
    while l % t:
        t //= 2
    return t


def kernel(x, c, ctx, c_ctx, w_ada, b_ada, norm_mix, w_in, a_v_norm, a_w_s, a_b_s, b_q_norm, b_k_norm, b_rpb,
           c_q_a_norm, c_w_q_up, c_kv_a_norm, c_w_kv_up, c_q_norm, c_k_norm, w_out, norm_ffn,
           moe_w_group, moe_w_router, moe_w1, moe_w3, moe_w2):
    p = dict(norm_mix=norm_mix, w_in=w_in, a_v_norm=a_v_norm, a_w_s=a_w_s, a_b_s=a_b_s, b_q_norm=b_q_norm,
             b_k_norm=b_k_norm, c_q_a_norm=c_q_a_norm, c_w_q_up=c_w_q_up, c_kv_a_norm=c_kv_a_norm,
             c_w_kv_up=c_w_kv_up, c_q_norm=c_q_norm, c_k_norm=c_k_norm, w_out=w_out, norm_ffn=norm_ffn,
             moe_w_group=moe_w_group, moe_w_router=moe_w_router)
    bn, l, d = x.shape
    nc = ctx.shape[1]
    depth = w_ada.shape[0]
    rows = l // GRID_W
    tm = _pick_tile(l, 512)
    tq = _pick_tile(l, MLA_QUERY_TILE)

    cos_t, sin_t = _rope_tables_t(l)
    cos_x = jnp.ones((C_ROPE, nc), F32)
    sin_x = jnp.zeros((C_ROPE, nc), F32)
    cc8 = jnp.concatenate([c, c_ctx[None, :], jnp.zeros((8 - bn - 1, d), F32)], axis=0)

    xc = ctx
    for i in range(depth):
        need_ctx = i < depth - 1
        lw = _layer_weights(i, p)
        mod = _ada(cc8, w_ada[i], b_ada[i][None, :])
        mods = [mod[:bn, j * d:(j + 1) * d][:, None, :] for j in range(6)]
        modx = [jnp.broadcast_to(mod[bn, j * d:(j + 1) * d][None, None, :], (bn, 1, d)) for j in range(6)]
        sh1, s1, g1, sh2, s2, g2 = mods
        sh1x, s1x, g1x, sh2x, s2x, g2x = modx

        oa, qb, kb, vb, qc, kc, vc = _mixin(x, sh1, s1, lw, cos_t, sin_t, tm)
        oax, qbx, kbx, vbx, qcx, kcx, vcx = _mixin(xc, sh1x, s1x, lw, cos_x, sin_x, nc)

        table = _na_bias_table(b_rpb[i], rows)
        ob = _na_attention(qb, kb, vb, kbx, vbx, table)
        oc = _mla_attention(qc, kc, vc, kcx, vcx, tq)
        n_lat = bn * l
        n_tok = n_lat + (bn * nc if need_ctx else 0)
        x, h2t, ei, wf, cnt = _mixout(x, oa, ob, oc, g1, sh2, s2, lw, tm, n_tok)
        if need_ctx:
            obx = _flash(qbx, kbx[:, None], vbx,
                         pl.BlockSpec((1, 1, nc, LANE), lambda b, h, ii, j: (b, 0, j, h // 2)),
                         tq=nc, tk=nc, name="ctx_dense_attention")
            ocx = _flash(qcx, kcx, vcx.reshape(bn, C_WIDTH, nc),
                         pl.BlockSpec((1, 1, nc, QPAD), lambda b, h, ii, j: (b, h, j, 0)),
                         tq=nc, tk=nc, name="ctx_mla_attention")
            xc, h2t, ei, wf, cnt = _mixout(xc, oax, obx, ocx, g1x, sh2x, s2x, lw, nc, n_tok, row0=n_lat,
                                           bufs=(h2t, ei, wf), cnt_in=cnt)

        counts = cnt[:, 0].astype(jnp.int32)
        pstarts, block_e, block_i, used, nb = _expert_blocks(counts, TOP_K * n_tok, MOE_BLOCK)
        xs = _dispatch(pstarts, ei, h2t, nb * MOE_BLOCK, tm)
        ys = _experts(xs, block_e, block_i, used, moe_w1, moe_w3, moe_w2, i, MOE_BLOCK)
        x = _combine(x, pstarts, ei, wf, ys, g2, 0, tm)
        if need_ctx:
            xc = _combine(xc, pstarts, ei, wf, ys, g2x, n_lat, nc)
    return x
```

```python
import functools

import numpy as np
import jax
import jax.numpy as jnp
from jax import lax
from jax.experimental import pallas as pl
from jax.experimental.pallas import tpu as pltpu

F32 = jnp.float32
BF16 = jnp.bfloat16

D_MODEL = 1024
GRID_W = 64
HEAD_DIM = 64
EPS = 1e-6
NEG_INF = -1e30

A_HEADS = 4
A_WIDTH = A_HEADS * HEAD_DIM
CHUNK = 128
B_HEADS = 6
B_WIDTH = B_HEADS * HEAD_DIM
NA_ROWS = 8
NA_COLS = 16
C_HEADS = 6
C_NOPE = 64
C_ROPE = 32
C_QK = C_NOPE + C_ROPE
C_VDIM = 64
C_Q_LORA = 384
C_KV_LORA = 256
C_WIDTH = C_HEADS * C_VDIM
ROPE_BASE = 10000.0
IN_A = 2 * A_WIDTH
IN_B = 3 * B_WIDTH
IN_C = C_Q_LORA + C_KV_LORA + C_ROPE
IN_AUG = IN_A + IN_B + IN_C + C_ROPE

N_GROUPS = 8
EXPERTS_PER_GROUP = 8
N_EXPERTS = N_GROUPS * EXPERTS_PER_GROUP
TOP_K = 2
D_EXPERT = D_MODEL // 2
ROUTER_PAD = 128

V7X_VMEM_LIMIT_BYTES = 56 * 1024 * 1024
LANE = 128
QPAD = 128

MLA_KEY_CHUNK = 256
MLA_QUERY_TILE = 1024
NA_QROWS = 8
NA_KBLK_ROWS = 4
MOE_BLOCK = 256

HIGHEST = lax.Precision.HIGHEST
LOG2E = 1.4426950408889634


def _cparams(sem):
    return pltpu.CompilerParams(dimension_semantics=sem, vmem_limit_bytes=V7X_VMEM_LIMIT_BYTES)


def _ada_body(c_ref, w_ref, b_ref, o_ref):
    cc = c_ref[...]
    s = cc * (1.0 / (1.0 + jnp.exp(-cc)))
    o_ref[...] = jnp.dot(s, w_ref[...], preferred_element_type=F32, precision=HIGHEST) + b_ref[...]


def _ada(cc8, w, b):
    n_out = w.shape[1]
    tn = 1024
    return pl.pallas_call(
        _ada_body,
        grid=(n_out // tn,),
        in_specs=[pl.BlockSpec((8, D_MODEL), lambda j: (0, 0)),
                  pl.BlockSpec((D_MODEL, tn), lambda j: (0, j)),
                  pl.BlockSpec((1, tn), lambda j: (0, j))],
        out_specs=pl.BlockSpec((8, tn), lambda j: (0, j)),
        out_shape=jax.ShapeDtypeStruct((8, n_out), F32),
        compiler_params=_cparams(("arbitrary",)),
        name="ada_mod",
    )(cc8, w, b)


def _rms_rows(v):
    return lax.rsqrt(jnp.mean(v * v, axis=0, keepdims=True) + EPS)


def _gelu_tanh(x):
    return 0.5 * x * (1.0 + jnp.tanh(0.7978845608028654 * (x + 0.044715 * (x * x * x))))


def _mixin_body(x_ref, sh_ref, sc_ref, gn_ref, wt_ref, avn_ref, wst_ref, bs_ref, gq_ref, gk_ref,
                cqa_ref, wq_ref, ckva_ref, wkv_ref, gcqn_ref, gcqr_ref, gcqrp_ref, gckn_ref, gckr_ref,
                gckrp_ref, cos_ref, sin_ref,
                oa_ref, qb_ref, kb_ref, vb_ref, qc_ref, kc_ref, vc_ref, *, tm):
    x = x_ref[0]
    h = x * lax.rsqrt(jnp.mean(x * x, axis=-1, keepdims=True) + EPS) * gn_ref[...]
    h = h * (1.0 + sc_ref[0]) + sh_ref[0]
    zt = lax.dot_general(wt_ref[...], h.astype(BF16), (((1,), (1,)), ((), ())),
                         preferred_element_type=F32)

    ga = _gelu_tanh(zt[0:IN_A])
    u = ga[0:A_WIDTH]
    v = ga[A_WIDTH:IN_A]
    vn = (v * _rms_rows(v) * avn_ref[...]).astype(BF16)
    for hd in range(A_HEADS):
        r0, r1 = hd * HEAD_DIM, (hd + 1) * HEAD_DIM
        for c in range(tm // CHUNK):
            c0, c1 = c * CHUNK, (c + 1) * CHUNK
            vm = jnp.dot(vn[r0:r1, c0:c1], wst_ref[hd], preferred_element_type=F32) + bs_ref[hd]
            oa_ref[0, r0:r1, c0:c1] = (u[r0:r1, c0:c1] * vm).astype(BF16)

    zb = zt[IN_A:IN_A + IN_B]
    zeros64 = jnp.zeros((HEAD_DIM, tm), F32)
    kn = []
    for hd in range(B_HEADS):
        q = zb[hd * HEAD_DIM:(hd + 1) * HEAD_DIM]
        k = zb[B_WIDTH + hd * HEAD_DIM:B_WIDTH + (hd + 1) * HEAD_DIM]
        qn = q * _rms_rows(q) * gq_ref[...]
        kn.append(k * _rms_rows(k) * gk_ref[...])
        pair = [qn, zeros64] if hd % 2 == 0 else [zeros64, qn]
        qb_ref[0, hd] = jnp.concatenate(pair, axis=0).astype(BF16)
    for p in range(B_HEADS // 2):
        kt = jnp.concatenate([kn[2 * p], kn[2 * p + 1]], axis=0)
        kb_ref[0, :, p * LANE:(p + 1) * LANE] = kt.T.astype(BF16)
    vb_ref[0] = zb[2 * B_WIDTH:3 * B_WIDTH].astype(BF16)

    zc = zt[IN_A + IN_B:IN_AUG]
    ql = zc[0:C_Q_LORA]
    kvl = zc[C_Q_LORA:C_Q_LORA + C_KV_LORA]
    kr = zc[C_Q_LORA + C_KV_LORA:C_Q_LORA + C_KV_LORA + C_ROPE]
    krp = zc[C_Q_LORA + C_KV_LORA + C_ROPE:C_Q_LORA + C_KV_LORA + 2 * C_ROPE]
    qln = (ql * _rms_rows(ql) * cqa_ref[...]).astype(BF16)
    kvln = (kvl * _rms_rows(kvl) * ckva_ref[...]).astype(BF16)
    qt = jnp.dot(wq_ref[...], qln, preferred_element_type=F32)
    kvt = jnp.dot(wkv_ref[...], kvln, preferred_element_type=F32)
    cos = cos_ref[...]
    sin = sin_ref[...]
    krn = _rms_rows(kr) * (gckr_ref[...] * kr * cos + gckrp_ref[...] * krp * sin)
    zeros32 = jnp.zeros((QPAD - C_QK, tm), F32)
    nq = C_HEADS * C_NOPE
    nr = C_HEADS * C_ROPE
    for hd in range(C_HEADS):
        qn = qt[hd * C_NOPE:(hd + 1) * C_NOPE]
        qn = qn * _rms_rows(qn) * gcqn_ref[...]
        qr = qt[nq + hd * C_ROPE:nq + (hd + 1) * C_ROPE]
        qrp = qt[nq + nr + hd * C_ROPE:nq + nr + (hd + 1) * C_ROPE]
        qrn = _rms_rows(qr) * (gcqr_ref[...] * qr * cos + gcqrp_ref[...] * qrp * sin)
        qc_ref[0, hd] = jnp.concatenate([qn, qrn, zeros32], axis=0).astype(BF16)
        kn_c = kvt[hd * C_NOPE:(hd + 1) * C_NOPE]
        kn_c = kn_c * _rms_rows(kn_c) * gckn_ref[...]
        kc_ref[0, hd] = jnp.concatenate([kn_c, krn, zeros32], axis=0).T.astype(BF16)
    vt = vc_ref.shape[3]
    v_rows = kvt[nq:nq + C_WIDTH].astype(BF16)
    for c in range(tm // vt):
        vc_ref[0, c] = v_rows[:, c * vt:(c + 1) * vt]


def _mixin(x, sh, sc, lw, cos_t, sin_t, tm):
    bn, l, _ = x.shape
    vt = min(MLA_KEY_CHUNK, tm)
    const2 = lambda b, t: (0, 0)
    const3 = lambda b, t: (0, 0, 0)
    in_specs = [
        pl.BlockSpec((1, tm, D_MODEL), lambda b, t: (b, t, 0)),
        pl.BlockSpec((1, 1, D_MODEL), lambda b, t: (b, 0, 0)),
        pl.BlockSpec((1, 1, D_MODEL), lambda b, t: (b, 0, 0)),
        pl.BlockSpec((1, D_MODEL), const2),
        pl.BlockSpec((IN_AUG, D_MODEL), const2),
        pl.BlockSpec((A_WIDTH, 1), const2),
        pl.BlockSpec((A_HEADS, CHUNK, CHUNK), const3),
        pl.BlockSpec((A_HEADS, 1, CHUNK), const3),
        pl.BlockSpec((HEAD_DIM, 1), const2),
        pl.BlockSpec((HEAD_DIM, 1), const2),
        pl.BlockSpec((C_Q_LORA, 1), const2),
        pl.BlockSpec((C_HEADS * (C_NOPE + 2 * C_ROPE), C_Q_LORA), const2),
        pl.BlockSpec((C_KV_LORA, 1), const2),
        pl.BlockSpec((C_HEADS * (C_NOPE + C_VDIM), C_KV_LORA), const2),
        pl.BlockSpec((C_NOPE, 1), const2),
        pl.BlockSpec((C_ROPE, 1), const2),
        pl.BlockSpec((C_ROPE, 1), const2),
        pl.BlockSpec((C_NOPE, 1), const2),
        pl.BlockSpec((C_ROPE, 1), const2),
        pl.BlockSpec((C_ROPE, 1), const2),
        pl.BlockSpec((C_ROPE, tm), lambda b, t: (0, t)),
        pl.BlockSpec((C_ROPE, tm), lambda b, t: (0, t)),
    ]
    out_shape = (
        jax.ShapeDtypeStruct((bn, A_WIDTH, l), BF16),
        jax.ShapeDtypeStruct((bn, B_HEADS, QPAD, l), BF16),
        jax.ShapeDtypeStruct((bn, l, B_WIDTH), BF16),
        jax.ShapeDtypeStruct((bn, B_WIDTH, l), BF16),
        jax.ShapeDtypeStruct((bn, C_HEADS, QPAD, l), BF16),
        jax.ShapeDtypeStruct((bn, C_HEADS, l, QPAD), BF16),
        jax.ShapeDtypeStruct((bn, l // vt, C_WIDTH, vt), BF16),
    )
    out_specs = (
        pl.BlockSpec((1, A_WIDTH, tm), lambda b, t: (b, 0, t)),
        pl.BlockSpec((1, B_HEADS, QPAD, tm), lambda b, t: (b, 0, 0, t)),
        pl.BlockSpec((1, tm, B_WIDTH), lambda b, t: (b, t, 0)),
        pl.BlockSpec((1, B_WIDTH, tm), lambda b, t: (b, 0, t)),
        pl.BlockSpec((1, C_HEADS, QPAD, tm), lambda b, t: (b, 0, 0, t)),
        pl.BlockSpec((1, C_HEADS, tm, QPAD), lambda b, t: (b, 0, t, 0)),
        pl.BlockSpec((1, tm // vt, C_WIDTH, vt), lambda b, t: (b, t, 0, 0)),
    )
    return pl.pallas_call(
        functools.partial(_mixin_body, tm=tm),
        grid=(bn, l // tm),
        in_specs=in_specs,
        out_specs=out_specs,
        out_shape=out_shape,
        compiler_params=_cparams(("arbitrary", "arbitrary")),
        name="mix_in",
    )(x, sh, sc, lw["norm_mix"], lw["w_in_t"], lw["avn"], lw["wst"], lw["bs"], lw["gq"], lw["gk"],
      lw["cqa"], lw["wq_t"], lw["ckva"], lw["wkv_t"], lw["gcqn"], lw["gcqr"], lw["gcqrp"], lw["gckn"],
      lw["gckr"], lw["gckrp"], cos_t, sin_t)


DEN_ROWS = 16


def _na_body(q_ref, k0_ref, k1_ref, k2_ref, k3_ref, v0_ref, v1_ref, v2_ref, v3_ref, kx_ref, vx_ref, t_ref,
             o_ref):
    k_refs = (k0_ref, k1_ref, k2_ref, k3_ref)
    v_refs = (v0_ref, v1_ref, v2_ref, v3_ref)
    nk = k0_ref.shape[1]
    for hh in range(2):
        q = q_ref[0, hh]
        rows = slice(hh * HEAD_DIM, (hh + 1) * HEAD_DIM)
        s = jnp.concatenate([jnp.dot(kr[0], q, preferred_element_type=F32) for kr in k_refs], axis=0)
        s = (s + t_ref[hh, 0]).astype(BF16)
        sx = jnp.dot(kx_ref[0], q, preferred_element_type=F32).astype(BF16)
        m = jnp.maximum(jnp.max(s, axis=0, keepdims=True), jnp.max(sx, axis=0, keepdims=True))
        p = jnp.exp2(s - m)
        px = jnp.exp2(sx - m)
        o = jnp.dot(jnp.concatenate([vx_ref[0, rows], jnp.ones((DEN_ROWS, px.shape[0]), BF16)], axis=0), px,
                    preferred_element_type=F32)
        ones = jnp.ones((DEN_ROWS, nk), BF16)
        for j, vr in enumerate(v_refs):
            o = o + jnp.dot(jnp.concatenate([vr[0, rows], ones], axis=0), p[j * nk:(j + 1) * nk],
                            preferred_element_type=F32)
        o_ref[0, rows] = (o[0:HEAD_DIM] / o[HEAD_DIM:HEAD_DIM + 1]).astype(BF16)


def _na_attention(qb, kb, vb, kxb, vxb, table):
    bn, _, _, l = qb.shape
    nq = NA_QROWS * GRID_W
    nk = NA_KBLK_ROWS * GRID_W
    nblk = l // nq
    nkb = l // nk
    nctx = kxb.shape[1]

    def kmap(j):
        return lambda b, h, i: (b, jnp.clip(2 * i - 1 + j, 0, nkb - 1), h)

    def vmap_(j):
        return lambda b, h, i: (b, h, jnp.clip(2 * i - 1 + j, 0, nkb - 1))

    def tmap(b, h, i):
        return (h, jnp.where(i == 0, 0, jnp.where(i == nblk - 1, 2, 1)), 0, 0)

    in_specs = ([pl.BlockSpec((1, 2, QPAD, nq), lambda b, h, i: (b, h, 0, i))]
                + [pl.BlockSpec((1, nk, LANE), kmap(j)) for j in range(4)]
                + [pl.BlockSpec((1, 2 * HEAD_DIM, nk), vmap_(j)) for j in range(4)]
                + [pl.BlockSpec((1, nctx, LANE), lambda b, h, i: (b, 0, h)),
                   pl.BlockSpec((1, 2 * HEAD_DIM, nctx), lambda b, h, i: (b, h, 0)),
                   pl.BlockSpec((2, 1, 4 * nk, nq), tmap)])
    return pl.pallas_call(
        _na_body,
        grid=(bn, B_HEADS // 2, nblk),
        in_specs=in_specs,
        out_specs=pl.BlockSpec((1, 2 * HEAD_DIM, nq), lambda b, h, i: (b, h, i)),
        out_shape=jax.ShapeDtypeStruct((bn, B_WIDTH, l), BF16),
        compiler_params=_cparams(("arbitrary", "arbitrary", "arbitrary")),
        name="na_attention",
    )(qb, kb, kb, kb, kb, vb, vb, vb, vb, kxb, vxb, table)


def _na_bias_table(rpb, rows):
    nblk = rows // NA_QROWS
    qc = np.arange(GRID_W)
    kc = np.arange(GRID_W)
    c0 = np.clip(qc - NA_COLS // 2, 0, GRID_W - NA_COLS)
    valid_col = (kc[:, None] >= c0[None, :]) & (kc[:, None] < c0[None, :] + NA_COLS)
    dc = np.clip(kc[:, None] - qc[None, :], -(NA_COLS - 1), NA_COLS - 1) + NA_COLS - 1
    dc_onehot = (dc[None] == np.arange(2 * NA_COLS - 1)[:, None, None]).astype(np.float32)
    tabs = []
    for i in (0, 1, nblk - 1):
        kr = NA_KBLK_ROWS * (2 * i - 1) + np.arange(4 * NA_KBLK_ROWS)
        qr = NA_QROWS * i + np.arange(NA_QROWS)
        r0 = np.clip(qr - NA_ROWS // 2, 0, rows - NA_ROWS)
        valid_row = ((kr[:, None] >= r0[None, :]) & (kr[:, None] < r0[None, :] + NA_ROWS)
                     & (kr[:, None] >= 0) & (kr[:, None] < rows))
        dr = np.clip(kr[:, None] - qr[None, :] + NA_ROWS - 1, 0, 2 * NA_ROWS - 2)
        bias = jnp.einsum("hkqd,dcx->hkcqx", rpb[:, dr].astype(F32), dc_onehot, precision=HIGHEST)
        valid = valid_row[:, None, :, None] & valid_col[None, :, None, :]
        tabs.append(jnp.where(valid[None], bias.astype(F32) * LOG2E, NEG_INF).reshape(
            rpb.shape[0], 4 * NA_KBLK_ROWS * GRID_W, NA_QROWS * GRID_W))
    return jnp.stack(tabs, axis=1)


def _flash_body(*refs, has_extra):
    if has_extra:
        q_ref, k_ref, v_ref, kx_ref, vx_ref, o_ref, m_sc, l_sc, acc_sc = refs
    else:
        q_ref, k_ref, v_ref, o_ref, m_sc, l_sc, acc_sc = refs
    kv = pl.program_id(3)
    nkv = pl.num_programs(3)

    @pl.when(kv == 0)
    def _():
        m_sc[...] = jnp.full(m_sc.shape, -jnp.inf, F32)
        l_sc[...] = jnp.zeros(l_sc.shape, F32)
        acc_sc[...] = jnp.zeros(acc_sc.shape, F32)

    def step(k, v):
        s = jnp.dot(k, q_ref[0, 0], preferred_element_type=F32)
        m_old = m_sc[...]
        m_new = jnp.maximum(m_old, jnp.max(s, axis=0, keepdims=True))
        alpha = jnp.exp2(m_old - m_new)
        p = jnp.exp2(s - m_new)
        l_sc[...] = alpha * l_sc[...] + jnp.sum(p, axis=0, keepdims=True)
        acc_sc[...] = alpha * acc_sc[...] + jnp.dot(v, p.astype(BF16), preferred_element_type=F32)
        m_sc[...] = m_new

    step(k_ref[0, 0], v_ref[0])

    @pl.when(kv == nkv - 1)
    def _():
        if has_extra:
            step(kx_ref[0, 0], vx_ref[0])
        o_ref[0] = (acc_sc[...] / l_sc[...]).astype(BF16)


def _flash(q, k, v, k_spec, kx=None, vx=None, kx_spec=None, *, tq, tk, name):
    bn, nh, _, lq = q.shape
    lk = v.shape[2]
    has_extra = kx is not None
    in_specs = [pl.BlockSpec((1, 1, QPAD, tq), lambda b, h, i, j: (b, h, 0, i)),
                k_spec,
                pl.BlockSpec((1, HEAD_DIM, tk), lambda b, h, i, j: (b, h, j))]
    args = [q, k, v]
    if has_extra:
        nx = vx.shape[2]
        in_specs += [kx_spec, pl.BlockSpec((1, HEAD_DIM, nx), lambda b, h, i, j: (b, h, 0))]
        args += [kx, vx]
    return pl.pallas_call(
        functools.partial(_flash_body, has_extra=has_extra),
        grid=(bn, nh, lq // tq, lk // tk),
        in_specs=in_specs,
        out_specs=pl.BlockSpec((1, HEAD_DIM, tq), lambda b, h, i, j: (b, h, i)),
        out_shape=jax.ShapeDtypeStruct((bn, nh * HEAD_DIM, lq), BF16),
        scratch_shapes=[pltpu.VMEM((1, tq), F32), pltpu.VMEM((1, tq), F32), pltpu.VMEM((HEAD_DIM, tq), F32)],
        compiler_params=_cparams(("arbitrary", "arbitrary", "arbitrary", "arbitrary")),
        name=name,
    )(*args)


def _mla_body(q_ref, k_ref, v_ref, kx_ref, vx_ref, o_ref, *, tk, nchunks):
    q = q_ref[0, 0]
    tq = q.shape[1]

    def scores(k):
        return jnp.dot(k, q, preferred_element_type=F32).astype(BF16)

    def absorb(s, v, carry):
        m, den, acc = carry
        m_new = jnp.maximum(m, jnp.max(s, axis=0, keepdims=True).astype(F32))
        alpha = jnp.exp2(m - m_new)
        p = jnp.exp2(s - m_new.astype(BF16))
        den = alpha * den + jnp.sum(p.astype(F32), axis=0, keepdims=True)
        acc = alpha * acc + jnp.dot(v, p, preferred_element_type=F32)
        return m_new, den, acc

    carry = (jnp.full((1, tq), -jnp.inf, F32), jnp.zeros((1, tq), F32), jnp.zeros((HEAD_DIM, tq), F32))
    s_cur = scores(k_ref[0, 0, 0:tk, :])
    for j in range(nchunks):
        if j + 1 < nchunks:
            s_next = scores(k_ref[0, 0, (j + 1) * tk:(j + 2) * tk, :])
        else:
            s_next = scores(kx_ref[0, 0])
        carry = absorb(s_cur, v_ref[0, j], carry)
        s_cur = s_next
    _, den, acc = absorb(s_cur, vx_ref[0, 0], carry)
    o_ref[0] = (acc / den).astype(BF16)


def _mla_attention(q, k, v, kx, vx, tq):
    bn, nh, _, l = q.shape
    nchunks, tk = v.shape[1], v.shape[3]
    nc = kx.shape[2]
    return pl.pallas_call(
        functools.partial(_mla_body, tk=tk, nchunks=nchunks),
        grid=(bn, nh, l // tq),
        in_specs=[pl.BlockSpec((1, 1, QPAD, tq), lambda b, h, i: (b, h, 0, i)),
                  pl.BlockSpec((1, 1, l, QPAD), lambda b, h, i: (b, h, 0, 0)),
                  pl.BlockSpec((1, nchunks, HEAD_DIM, tk), lambda b, h, i: (b, 0, h, 0)),
                  pl.BlockSpec((1, 1, nc, QPAD), lambda b, h, i: (b, h, 0, 0)),
                  pl.BlockSpec((1, 1, HEAD_DIM, nc), lambda b, h, i: (b, 0, h, 0))],
        out_specs=pl.BlockSpec((1, HEAD_DIM, tq), lambda b, h, i: (b, h, i)),
        out_shape=jax.ShapeDtypeStruct((bn, nh * HEAD_DIM, l), BF16),
        compiler_params=_cparams(("arbitrary", "arbitrary", "arbitrary")),
        name="mla_attention",
    )(q, k, v, kx, vx)


TOKEN_ROWS = 8


def _first_argmax_rows(v, row_id):
    vmax = jnp.max(v, axis=0, keepdims=True)
    idx = jnp.min(jnp.where(v == vmax, row_id, float(v.shape[0])), axis=0, keepdims=True)
    return vmax, idx


def _mixout_body(x_ref, oa_ref, ob_ref, oc_ref, wt_ref, g1_ref, sh2_ref, sc2_ref, gn2_ref, wr_ref, tri_ref,
                 *rest, tm, steps, has_base):
    if has_base:
        cnt_in_ref = rest[0]
    xo_ref, h2_ref, ei_ref, wf_ref, cnt_ref, base_sc = rest[-6:]
    i = pl.program_id(0)

    @pl.when(i == 0)
    def _():
        if has_base:
            base_sc[...] = cnt_in_ref[:, 0:1]
        else:
            base_sc[...] = jnp.zeros(base_sc.shape, F32)

    ot = jnp.concatenate([oa_ref[0], ob_ref[0], oc_ref[0]], axis=0)
    out_t = jnp.dot(wt_ref[...], ot, preferred_element_type=F32)
    xn = x_ref[0] + g1_ref[0] * out_t.T
    xo_ref[0] = xn
    h2 = xn * lax.rsqrt(jnp.mean(xn * xn, axis=-1, keepdims=True) + EPS) * gn2_ref[...]
    h2 = h2 * (1.0 + sc2_ref[0]) + sh2_ref[0]
    for s in range(TOKEN_ROWS):
        h2_ref[pl.ds(s, tm, stride=TOKEN_ROWS), :] = h2[:, s * LANE:(s + 1) * LANE]

    h_hi = h2.astype(BF16)
    h_lo = (h2 - h_hi.astype(F32)).astype(BF16)
    lg = (jnp.dot(h_hi, wr_ref[0], preferred_element_type=F32) + jnp.dot(h_lo, wr_ref[0], preferred_element_type=F32)
          + jnp.dot(h_hi, wr_ref[1], preferred_element_type=F32))
    lt = lg.T
    gl = lt[0:N_GROUPS]
    rid = lax.broadcasted_iota(jnp.int32, (N_GROUPS, tm), 0).astype(F32)
    gmax, g_idx = _first_argmax_rows(gl, rid)
    g_gate = 1.0 / jnp.sum(jnp.exp(gl - gmax), axis=0, keepdims=True)
    e_sel = jnp.zeros((EXPERTS_PER_GROUP, tm), F32)
    for g in range(N_GROUPS):
        lo = N_GROUPS + g * EXPERTS_PER_GROUP
        e_sel = jnp.where(g_idx == float(g), lt[lo:lo + EXPERTS_PER_GROUP], e_sel)
    v1, j1 = _first_argmax_rows(e_sel, rid)
    v2, j2 = _first_argmax_rows(jnp.where(rid == j1, -jnp.inf, e_sel), rid)
    t21 = jnp.exp(v2 - v1)
    w1 = g_gate / (1.0 + t21)
    w2 = g_gate * t21 / (1.0 + t21)
    e1 = g_idx * float(EXPERTS_PER_GROUP) + j1
    e2 = g_idx * float(EXPERTS_PER_GROUP) + j2

    eid = lax.broadcasted_iota(jnp.int32, (N_EXPERTS, tm), 0).astype(F32)
    oh1 = (eid == e1).astype(F32)
    oh2 = (eid == e2).astype(F32)
    tri = tri_ref[...]
    cum1 = jnp.dot(oh1.astype(BF16), tri, preferred_element_type=F32)
    cum2 = jnp.dot(oh2.astype(BF16), tri, preferred_element_type=F32)
    tot1 = jnp.sum(oh1, axis=1, keepdims=True)
    tot2 = jnp.sum(oh2, axis=1, keepdims=True)
    base = base_sc[...]
    r1 = jnp.sum(oh1 * (base + cum1), axis=0, keepdims=True)
    r2 = jnp.sum(oh2 * (base + tot1 + cum2), axis=0, keepdims=True)
    live = jnp.where(i < steps, 1.0, 0.0)
    base_new = base + live * (tot1 + tot2)
    base_sc[...] = base_new
    cnt_ref[...] = jnp.broadcast_to(base_new, cnt_ref.shape)

    zeros4 = jnp.zeros((4, tm), F32)
    ei_ref[...] = jnp.concatenate([e1, e2, r1, r2, zeros4], axis=0).astype(jnp.int32)
    wpad = jnp.concatenate([w1, w2, jnp.zeros((LANE - 2, tm), F32)], axis=0)
    wf_ref[...] = wpad.T


def _mixout(x, oa, ob, oc, g1, sh2, sc2, lw, tm, n_rows, row0=0, bufs=None, cnt_in=None):
    bn, l, _ = x.shape
    nt = l // tm
    steps = bn * nt
    extra = 0
    if bufs is None and n_rows > bn * l:
        assert n_rows - bn * l == tm, "spare rows must be exactly one tile"
        extra = 1
    blk0 = row0 // tm

    def bt(i):
        ii = jnp.minimum(i, steps - 1)
        return ii // nt, ii % nt

    const2 = lambda i: (0, 0)
    modspec = pl.BlockSpec((1, 1, D_MODEL), lambda i: (bt(i)[0], 0, 0))
    rowspec = pl.BlockSpec((1, tm, D_MODEL), lambda i: (bt(i)[0], bt(i)[1], 0))
    tri = jnp.asarray(np.triu(np.ones((tm, tm), np.float32), k=1), BF16)
    in_specs = [rowspec,
                pl.BlockSpec((1, A_WIDTH, tm), lambda i: (bt(i)[0], 0, bt(i)[1])),
                pl.BlockSpec((1, B_WIDTH, tm), lambda i: (bt(i)[0], 0, bt(i)[1])),
                pl.BlockSpec((1, C_WIDTH, tm), lambda i: (bt(i)[0], 0, bt(i)[1])),
                pl.BlockSpec((D_MODEL, D_MODEL), const2),
                modspec, modspec, modspec,
                pl.BlockSpec((1, D_MODEL), const2),
                pl.BlockSpec((2, D_MODEL, ROUTER_PAD), lambda i: (0, 0, 0)),
                pl.BlockSpec((tm, tm), const2)]
    args = [x, oa, ob, oc, lw["w_out_t"], g1, sh2, sc2, lw["norm_ffn"], lw["w_route"], tri]
    aliases = {}
    if bufs is not None:
        in_specs.append(pl.BlockSpec((N_EXPERTS, LANE), const2))
        args.append(cnt_in)
        aliases = {len(args): 1, len(args) + 1: 2, len(args) + 2: 3}
        in_specs += [pl.BlockSpec(memory_space=pl.ANY)] * 3
        args += list(bufs)
    return pl.pallas_call(
        functools.partial(_mixout_body, tm=tm, steps=steps, has_base=bufs is not None),
        grid=(steps + extra,),
        in_specs=in_specs,
        out_specs=(rowspec,
                   pl.BlockSpec((tm * TOKEN_ROWS, LANE), lambda i: (blk0 + i, 0)),
                   pl.BlockSpec((8, tm), lambda i: (0, blk0 + i)),
                   pl.BlockSpec((tm, LANE), lambda i: (blk0 + i, 0)),
                   pl.BlockSpec((N_EXPERTS, LANE), const2)),
        out_shape=(jax.ShapeDtypeStruct((bn, l, D_MODEL), F32),
                   jax.ShapeDtypeStruct((n_rows * TOKEN_ROWS, LANE), F32),
                   jax.ShapeDtypeStruct((8, n_rows), jnp.int32),
                   jax.ShapeDtypeStruct((n_rows, LANE), F32),
                   jax.ShapeDtypeStruct((N_EXPERTS, LANE), F32)),
        scratch_shapes=[pltpu.VMEM((N_EXPERTS, 1), F32)],
        input_output_aliases=aliases,
        compiler_params=_cparams(("arbitrary",)),
        name="mix_out",
    )(*args)


def _tile_rows(idx):
    return pl.ds(pl.multiple_of(idx * TOKEN_ROWS, TOKEN_ROWS), TOKEN_ROWS)


def _slot_rows(sl_ref, k, r):
    return pl.ds(pl.multiple_of(sl_ref[k, r], TOKEN_ROWS), TOKEN_ROWS)


def _dispatch_body(sl_ref, h_ref, xs_in_ref, xs_ref, sem, *, td):
    del xs_in_ref

    def one(r, carry):
        for k in range(TOP_K):
            pltpu.make_async_copy(h_ref.at[_tile_rows(r)], xs_ref.at[_slot_rows(sl_ref, k, r)], sem).start()
        return carry
    lax.fori_loop(0, td, one, 0, unroll=8)
    for k in range(TOP_K):
        pltpu.make_async_copy(h_ref, xs_ref.at[pl.ds(0, td * TOKEN_ROWS)], sem).wait()


def _dispatch(slots, h2t, n_slots, td):
    n_tok = slots.shape[1]
    xs0 = jnp.zeros((n_slots * TOKEN_ROWS, LANE), F32)
    return pl.pallas_call(
        functools.partial(_dispatch_body, td=td),
        grid=(n_tok // td,),
        in_specs=[pl.BlockSpec((8, td), lambda i: (0, i), memory_space=pltpu.SMEM),
                  pl.BlockSpec((td * TOKEN_ROWS, LANE), lambda i: (i, 0)),
                  pl.BlockSpec(memory_space=pl.ANY)],
        out_specs=pl.BlockSpec(memory_space=pl.ANY),
        out_shape=jax.ShapeDtypeStruct(xs0.shape, F32),
        scratch_shapes=[pltpu.SemaphoreType.DMA(())],
        input_output_aliases={2: 0},
        compiler_params=_cparams(("arbitrary",)),
        name="moe_dispatch",
    )(slots, h2t, xs0)


def _experts_body(be_ref, bi_ref, used_ref, x_ref, w1_ref, w3_ref, w2_ref, y_ref, w1b, w3b, w2b, *, tb):
    i = pl.program_id(0)
    used = used_ref[i] > 0

    @pl.when(used)
    def _():
        prev_e = be_ref[jnp.maximum(i - 1, 0)]

        @pl.when((i == 0) | (prev_e != be_ref[i]))
        def _():
            w1b[...] = w1_ref[0, 0].astype(BF16)
            w3b[...] = w3_ref[0, 0].astype(BF16)
            w2b[...] = w2_ref[0, 0].astype(BF16)

        xb = jnp.concatenate([x_ref[pl.ds(s, tb, stride=TOKEN_ROWS), :] for s in range(TOKEN_ROWS)],
                             axis=1).astype(BF16)
        a = jnp.dot(xb, w1b[...], preferred_element_type=F32)
        b = jnp.dot(xb, w3b[...], preferred_element_type=F32)
        hm = (a * (1.0 / (1.0 + jnp.exp(-a))) * b).astype(BF16)
        y = jnp.dot(hm, w2b[...], preferred_element_type=F32)
        for s in range(TOKEN_ROWS):
            y_ref[pl.ds(s, tb, stride=TOKEN_ROWS), :] = y[:, s * LANE:(s + 1) * LANE]

    @pl.when(jnp.logical_not(used))
    def _():
        y_ref[...] = jnp.zeros(y_ref.shape, F32)


def _experts(xs, block_e, block_i, used, w1, w3, w2, layer, tb):
    nb = block_e.shape[0]
    wspec_up = pl.BlockSpec((1, 1, D_MODEL, D_EXPERT), lambda i, be, bi, us: (layer, be[i], 0, 0))
    grid_spec = pltpu.PrefetchScalarGridSpec(
        num_scalar_prefetch=3,
        grid=(nb,),
        in_specs=[pl.BlockSpec((tb * TOKEN_ROWS, LANE), lambda i, be, bi, us: (bi[i], 0)),
                  wspec_up, wspec_up,
                  pl.BlockSpec((1, 1, D_EXPERT, D_MODEL), lambda i, be, bi, us: (layer, be[i], 0, 0))],
        out_specs=pl.BlockSpec((tb * TOKEN_ROWS, LANE), lambda i, be, bi, us: (i, 0)),
        scratch_shapes=[pltpu.VMEM((D_MODEL, D_EXPERT), BF16), pltpu.VMEM((D_MODEL, D_EXPERT), BF16),
                        pltpu.VMEM((D_EXPERT, D_MODEL), BF16)],
    )
    return pl.pallas_call(
        functools.partial(_experts_body, tb=tb),
        grid_spec=grid_spec,
        out_shape=jax.ShapeDtypeStruct(xs.shape, F32),
        compiler_params=_cparams(("arbitrary",)),
        name="moe_experts",
    )(block_e, block_i, used, xs, w1, w3, w2)


def _combine_body(sl_ref, wf_ref, x_ref, g2_ref, ys_ref, o_ref, buf0, buf1, sem, *, tm):
    bufs = (buf0, buf1)

    def one(r, carry):
        for k in range(TOP_K):
            pltpu.make_async_copy(ys_ref.at[_slot_rows(sl_ref, k, r)], bufs[k].at[_tile_rows(r)], sem).start()
        return carry
    lax.fori_loop(0, tm, one, 0, unroll=8)
    for k in range(TOP_K):
        pltpu.make_async_copy(ys_ref.at[pl.ds(0, tm * TOKEN_ROWS)], bufs[k], sem).wait()
    wf = wf_ref[...]
    y = None
    for k in range(TOP_K):
        yk = jnp.concatenate([bufs[k][pl.ds(s, tm, stride=TOKEN_ROWS), :] for s in range(TOKEN_ROWS)], axis=1)
        yk = wf[:, k:k + 1] * yk
        y = yk if y is None else y + yk
    o_ref[0] = x_ref[0] + g2_ref[0] * y


def _combine(x, slots, wf, ys, g2, row0, tm):
    bn, l, _ = x.shape
    nt = l // tm
    t0 = row0 // tm
    rowspec = pl.BlockSpec((1, tm, D_MODEL), lambda b, t: (b, t, 0))
    return pl.pallas_call(
        functools.partial(_combine_body, tm=tm),
        grid=(bn, nt),
        in_specs=[pl.BlockSpec((8, tm), lambda b, t: (0, t0 + b * nt + t), memory_space=pltpu.SMEM),
                  pl.BlockSpec((tm, LANE), lambda b, t: (t0 + b * nt + t, 0)),
                  rowspec,
                  pl.BlockSpec((1, 1, D_MODEL), lambda b, t: (b, 0, 0)),
                  pl.BlockSpec(memory_space=pl.ANY)],
        out_specs=rowspec,
        out_shape=jax.ShapeDtypeStruct(x.shape, F32),
        scratch_shapes=[pltpu.VMEM((tm * TOKEN_ROWS, LANE), F32), pltpu.VMEM((tm * TOKEN_ROWS, LANE), F32),
                        pltpu.SemaphoreType.DMA(())],
        compiler_params=_cparams(("arbitrary", "arbitrary")),
        name="moe_combine",
    )(slots, wf, x, g2, ys)


def _expert_blocks(counts, n_assign, tb):
    pcounts = (counts + tb - 1) // tb * tb
    pends = jnp.cumsum(pcounts)
    pstarts = (pends - pcounts).astype(jnp.int32)
    nb = -(-n_assign // tb) + N_EXPERTS
    n_used = pends[-1] // tb
    blk = jnp.arange(nb, dtype=jnp.int32)
    used = blk < n_used
    last = jnp.maximum(n_used - 1, 0).astype(jnp.int32)
    block_i = jnp.where(used, blk, last)
    block_e = jnp.sum((pends[None, :] <= (block_i * tb)[:, None]).astype(jnp.int32), axis=1)
    block_e = jnp.minimum(block_e, N_EXPERTS - 1)
    return pstarts, block_e, block_i, used.astype(jnp.int32), nb


_ROPE_PERM = np.concatenate([np.arange(8, 16), np.arange(0, 8), np.arange(24, 32), np.arange(16, 24)])
_ROPE_SIGN = np.concatenate([-np.ones(8), np.ones(8), -np.ones(8), np.ones(8)]).astype(np.float32)


def _rope_tables_t(l):
    half = C_ROPE // 2
    inv = ROPE_BASE ** (-jnp.arange(0, half, 2, dtype=F32) / half)
    pos = jnp.arange(l)
    ang_r = (pos // GRID_W).astype(F32)[None, :] * inv[:, None]
    ang_c = (pos % GRID_W).astype(F32)[None, :] * inv[:, None]
    cos_t = jnp.concatenate([jnp.cos(ang_r)] * 2 + [jnp.cos(ang_c)] * 2, axis=0)
    sin_t = jnp.concatenate([jnp.sin(ang_r)] * 2 + [jnp.sin(ang_c)] * 2, axis=0)
    return cos_t, sin_t


def _col(v):
    return v.astype(F32)[:, None]


def _layer_weights(i, p):
    w_in = p["w_in"][i]
    kr0 = IN_A + IN_B + C_Q_LORA + C_KV_LORA
    kr_cols = w_in[:, kr0:kr0 + C_ROPE]
    w_aug = jnp.concatenate([w_in, kr_cols[:, _ROPE_PERM] * _ROPE_SIGN], axis=1)

    wq = p["c_w_q_up"][i].reshape(C_Q_LORA, C_HEADS, C_QK)
    wq_n = wq[:, :, :C_NOPE].reshape(C_Q_LORA, C_HEADS * C_NOPE)
    wq_r = wq[:, :, C_NOPE:]
    wq_rp = (wq_r[:, :, _ROPE_PERM] * _ROPE_SIGN).reshape(C_Q_LORA, C_HEADS * C_ROPE)
    wq_all = jnp.concatenate([wq_n, wq_r.reshape(C_Q_LORA, C_HEADS * C_ROPE), wq_rp], axis=1)

    wkv = p["c_w_kv_up"][i].reshape(C_KV_LORA, C_HEADS, C_NOPE + C_VDIM)
    wkv_all = jnp.concatenate([wkv[:, :, :C_NOPE].reshape(C_KV_LORA, C_HEADS * C_NOPE),
                               wkv[:, :, C_NOPE:].reshape(C_KV_LORA, C_HEADS * C_VDIM)], axis=1)

    cqn = p["c_q_norm"][i].astype(F32) * (C_QK ** -0.5 * LOG2E)
    ckn = p["c_k_norm"][i].astype(F32)
    w_route = jnp.concatenate([p["moe_w_group"][i], p["moe_w_router"][i],
                               jnp.zeros((D_MODEL, ROUTER_PAD - N_GROUPS - N_EXPERTS), F32)], axis=1)
    return {
        "norm_mix": p["norm_mix"][i][None, :],
        "w_in_t": w_aug.T.astype(BF16),
        "avn": _col(p["a_v_norm"][i]),
        "wst": jnp.transpose(p["a_w_s"][i], (0, 2, 1)).astype(BF16),
        "bs": p["a_b_s"][i][:, None, :],
        "gq": _col(p["b_q_norm"][i] * (HEAD_DIM ** -0.5 * LOG2E)),
        "gk": _col(p["b_k_norm"][i]),
        "cqa": _col(p["c_q_a_norm"][i]),
        "wq_t": wq_all.T.astype(BF16),
        "ckva": _col(p["c_kv_a_norm"][i]),
        "wkv_t": wkv_all.T.astype(BF16),
        "gcqn": _col(cqn[:C_NOPE]),
        "gcqr": _col(cqn[C_NOPE:]),
        "gcqrp": _col(cqn[C_NOPE:][_ROPE_PERM]),
        "gckn": _col(ckn[:C_NOPE]),
        "gckr": _col(ckn[C_NOPE:]),
        "gckrp": _col(ckn[C_NOPE:][_ROPE_PERM]),
        "w_out_t": p["w_out"][i].T.astype(BF16),
        "norm_ffn": p["norm_ffn"][i][None, :],
        "w_route": jnp.stack([w_route.astype(BF16), (w_route - w_route.astype(BF16).astype(F32)).astype(BF16)]),
    }


def _pick_tile(l, pref):
    t = min(l, pref)
    while l % t:
        t //= 2
    return t


def kernel(x, c, ctx, c_ctx, w_ada, b_ada, norm_mix, w_in, a_v_norm, a_w_s, a_b_s, b_q_norm, b_k_norm, b_rpb,
           c_q_a_norm, c_w_q_up, c_kv_a_norm, c_w_kv_up, c_q_norm, c_k_norm, w_out, norm_ffn,
           moe_w_group, moe_w_router, moe_w1, moe_w3, moe_w2):
    p = dict(norm_mix=norm_mix, w_in=w_in, a_v_norm=a_v_norm, a_w_s=a_w_s, a_b_s=a_b_s, b_q_norm=b_q_norm,
             b_k_norm=b_k_norm, c_q_a_norm=c_q_a_norm, c_w_q_up=c_w_q_up, c_kv_a_norm=c_kv_a_norm,
             c_w_kv_up=c_w_kv_up, c_q_norm=c_q_norm, c_k_norm=c_k_norm, w_out=w_out, norm_ffn=norm_ffn,
             moe_w_group=moe_w_group, moe_w_router=moe_w_router)
    bn, l, d = x.shape
    nc = ctx.shape[1]
    depth = w_ada.shape[0]
    rows = l // GRID_W
    tm = _pick_tile(l, 512)
    tq = _pick_tile(l, MLA_QUERY_TILE)

    cos_t, sin_t = _rope_tables_t(l)
    cos_x = jnp.ones((C_ROPE, nc), F32)
    sin_x = jnp.zeros((C_ROPE, nc), F32)
    cc8 = jnp.concatenate([c, c_ctx[None, :], jnp.zeros((8 - bn - 1, d), F32)], axis=0)

    xc = ctx
    for i in range(depth):
        need_ctx = i < depth - 1
        lw = _layer_weights(i, p)
        mod = _ada(cc8, w_ada[i], b_ada[i][None, :])
        mods = [mod[:bn, j * d:(j + 1) * d][:, None, :] for j in range(6)]
        modx = [jnp.broadcast_to(mod[bn, j * d:(j + 1) * d][None, None, :], (bn, 1, d)) for j in range(6)]
        sh1, s1, g1, sh2, s2, g2 = mods
        sh1x, s1x, g1x, sh2x, s2x, g2x = modx

        oa, qb, kb, vb, qc, kc, vc = _mixin(x, sh1, s1, lw, cos_t, sin_t, tm)
        oax, qbx, kbx, vbx, qcx, kcx, vcx = _mixin(xc, sh1x, s1x, lw, cos_x, sin_x, nc)

        table = _na_bias_table(b_rpb[i], rows)
        ob = _na_attention(qb, kb, vb, kbx, vbx, table)
        oc = _mla_attention(qc, kc, vc, kcx, vcx, tq)
        n_lat = bn * l
        n_tok = n_lat + (bn * nc if need_ctx else 0)
        x, h2t, ei, wf, cnt = _mixout(x, oa, ob, oc, g1, sh2, s2, lw, tm, n_tok)
        if need_ctx:
            obx = _flash(qbx, kbx[:, None], vbx,
                         pl.BlockSpec((1, 1, nc, LANE), lambda b, h, ii, j: (b, 0, j, h // 2)),
                         tq=nc, tk=nc, name="ctx_dense_attention")
            ocx = _flash(qcx, kcx, vcx.reshape(bn, C_WIDTH, nc),
                         pl.BlockSpec((1, 1, nc, QPAD), lambda b, h, ii, j: (b, h, j, 0)),
                         tq=nc, tk=nc, name="ctx_mla_attention")
            xc, h2t, ei, wf, cnt = _mixout(xc, oax, obx, ocx, g1x, sh2x, s2x, lw, nc, n_tok, row0=n_lat,
                                           bufs=(h2t, ei, wf), cnt_in=cnt)

        counts = cnt[:, 0].astype(jnp.int32)
        pstarts, block_e, block_i, used, nb = _expert_blocks(counts, TOP_K * n_tok, MOE_BLOCK)
        slots = jnp.pad((pstarts[ei[:TOP_K]] + ei[TOP_K:2 * TOP_K]) * TOKEN_ROWS, ((0, 8 - TOP_K), (0, 0)))
        xs = _dispatch(slots, h2t, nb * MOE_BLOCK, tm)
        ys = _experts(xs, block_e, block_i, used, moe_w1, moe_w3, moe_w2, i, MOE_BLOCK)
        x = _combine(x, slots, wf, ys, g2, 0, tm)
        if need_ctx:
            xc = _combine(xc, slots, wf, ys, g2x, n_lat, nc)
    return x
```

```python
import functools

import numpy as np
import jax
import jax.numpy as jnp
from jax import lax
from jax.experimental import pallas as pl
from jax.experimental.pallas import tpu as pltpu

F32 = jnp.float32
BF16 = jnp.bfloat16

D_MODEL = 1024
GRID_W = 64
HEAD_DIM = 64
EPS = 1e-6
NEG_INF = -1e30

A_HEADS = 4
A_WIDTH = A_HEADS * HEAD_DIM
CHUNK = 128
B_HEADS = 6
B_WIDTH = B_HEADS * HEAD_DIM
NA_ROWS = 8
NA_COLS = 16
C_HEADS = 6
C_NOPE = 64
C_ROPE = 32
C_QK = C_NOPE + C_ROPE
C_VDIM = 64
C_Q_LORA = 384
C_KV_LORA = 256
C_WIDTH = C_HEADS * C_VDIM
ROPE_BASE = 10000.0
IN_A = 2 * A_WIDTH
IN_B = 3 * B_WIDTH
IN_C = C_Q_LORA + C_KV_LORA + C_ROPE
IN_AUG = IN_A + IN_B + IN_C + C_ROPE

N_GROUPS = 8
EXPERTS_PER_GROUP = 8
N_EXPERTS = N_GROUPS * EXPERTS_PER_GROUP
TOP_K = 2
D_EXPERT = D_MODEL // 2
ROUTER_PAD = 128

V7X_VMEM_LIMIT_BYTES = 56 * 1024 * 1024
LANE = 128
QPAD = 128

MLA_KEY_CHUNK = 256
MLA_QUERY_TILE = 1024
NA_QROWS = 8
NA_KBLK_ROWS = 4
MOE_BLOCK = 256

HIGHEST = lax.Precision.HIGHEST
LOG2E = 1.4426950408889634


def _cparams(sem):
    return pltpu.CompilerParams(dimension_semantics=sem, vmem_limit_bytes=V7X_VMEM_LIMIT_BYTES)


def _ada_body(c_ref, w_ref, b_ref, o_ref):
    cc = c_ref[...]
    s = cc * (1.0 / (1.0 + jnp.exp(-cc)))
    o_ref[...] = jnp.dot(s, w_ref[...], preferred_element_type=F32, precision=HIGHEST) + b_ref[...]


def _ada(cc8, w, b):
    n_out = w.shape[1]
    tn = 1024
    return pl.pallas_call(
        _ada_body,
        grid=(n_out // tn,),
        in_specs=[pl.BlockSpec((8, D_MODEL), lambda j: (0, 0)),
                  pl.BlockSpec((D_MODEL, tn), lambda j: (0, j)),
                  pl.BlockSpec((1, tn), lambda j: (0, j))],
        out_specs=pl.BlockSpec((8, tn), lambda j: (0, j)),
        out_shape=jax.ShapeDtypeStruct((8, n_out), F32),
        compiler_params=_cparams(("arbitrary",)),
        name="ada_mod",
    )(cc8, w, b)


def _rms_rows(v):
    return lax.rsqrt(jnp.mean(v * v, axis=0, keepdims=True) + EPS)


def _gelu_tanh(x):
    return 0.5 * x * (1.0 + jnp.tanh(0.7978845608028654 * (x + 0.044715 * (x * x * x))))


def _mixin_body(x_ref, sh_ref, sc_ref, gn_ref, wt_ref, avn_ref, wst_ref, bs_ref, gq_ref, gk_ref,
                cqa_ref, wq_ref, ckva_ref, wkv_ref, gcqn_ref, gcqr_ref, gcqrp_ref, gckn_ref, gckr_ref,
                gckrp_ref, cos_ref, sin_ref,
                oa_ref, qb_ref, kb_ref, vb_ref, qc_ref, kc_ref, vc_ref, *, tm):
    x = x_ref[0]
    h = x * lax.rsqrt(jnp.mean(x * x, axis=-1, keepdims=True) + EPS) * gn_ref[...]
    h = h * (1.0 + sc_ref[0]) + sh_ref[0]
    zt = lax.dot_general(wt_ref[...], h.astype(BF16), (((1,), (1,)), ((), ())),
                         preferred_element_type=F32)

    ga = _gelu_tanh(zt[0:IN_A])
    u = ga[0:A_WIDTH]
    v = ga[A_WIDTH:IN_A]
    vn = (v * _rms_rows(v) * avn_ref[...]).astype(BF16)
    for hd in range(A_HEADS):
        r0, r1 = hd * HEAD_DIM, (hd + 1) * HEAD_DIM
        for c in range(tm // CHUNK):
            c0, c1 = c * CHUNK, (c + 1) * CHUNK
            vm = jnp.dot(vn[r0:r1, c0:c1], wst_ref[hd], preferred_element_type=F32) + bs_ref[hd]
            oa_ref[0, r0:r1, c0:c1] = (u[r0:r1, c0:c1] * vm).astype(BF16)

    zb = zt[IN_A:IN_A + IN_B]
    zeros64 = jnp.zeros((HEAD_DIM, tm), F32)
    kn = []
    for hd in range(B_HEADS):
        q = zb[hd * HEAD_DIM:(hd + 1) * HEAD_DIM]
        k = zb[B_WIDTH + hd * HEAD_DIM:B_WIDTH + (hd + 1) * HEAD_DIM]
        qn = q * _rms_rows(q) * gq_ref[...]
        kn.append(k * _rms_rows(k) * gk_ref[...])
        pair = [qn, zeros64] if hd % 2 == 0 else [zeros64, qn]
        qb_ref[0, hd] = jnp.concatenate(pair, axis=0).astype(BF16)
    for p in range(B_HEADS // 2):
        kt = jnp.concatenate([kn[2 * p], kn[2 * p + 1]], axis=0)
        kb_ref[0, :, p * LANE:(p + 1) * LANE] = kt.T.astype(BF16)
    vb_ref[0] = zb[2 * B_WIDTH:3 * B_WIDTH].astype(BF16)

    zc = zt[IN_A + IN_B:IN_AUG]
    ql = zc[0:C_Q_LORA]
    kvl = zc[C_Q_LORA:C_Q_LORA + C_KV_LORA]
    kr = zc[C_Q_LORA + C_KV_LORA:C_Q_LORA + C_KV_LORA + C_ROPE]
    krp = zc[C_Q_LORA + C_KV_LORA + C_ROPE:C_Q_LORA + C_KV_LORA + 2 * C_ROPE]
    qln = (ql * _rms_rows(ql) * cqa_ref[...]).astype(BF16)
    kvln = (kvl * _rms_rows(kvl) * ckva_ref[...]).astype(BF16)
    qt = jnp.dot(wq_ref[...], qln, preferred_element_type=F32)
    kvt = jnp.dot(wkv_ref[...], kvln, preferred_element_type=F32)
    cos = cos_ref[...]
    sin = sin_ref[...]
    krn = _rms_rows(kr) * (gckr_ref[...] * kr * cos + gckrp_ref[...] * krp * sin)
    zeros32 = jnp.zeros((QPAD - C_QK, tm), F32)
    nq = C_HEADS * C_NOPE
    nr = C_HEADS * C_ROPE
    for hd in range(C_HEADS):
        qn = qt[hd * C_NOPE:(hd + 1) * C_NOPE]
        qn = qn * _rms_rows(qn) * gcqn_ref[...]
        qr = qt[nq + hd * C_ROPE:nq + (hd + 1) * C_ROPE]
        qrp = qt[nq + nr + hd * C_ROPE:nq + nr + (hd + 1) * C_ROPE]
        qrn = _rms_rows(qr) * (gcqr_ref[...] * qr * cos + gcqrp_ref[...] * qrp * sin)
        qc_ref[0, hd] = jnp.concatenate([qn, qrn, zeros32], axis=0).astype(BF16)
        kn_c = kvt[hd * C_NOPE:(hd + 1) * C_NOPE]
        kn_c = kn_c * _rms_rows(kn_c) * gckn_ref[...]
        kc_ref[0, hd] = jnp.concatenate([kn_c, krn, zeros32], axis=0).T.astype(BF16)
    vt = vc_ref.shape[3]
    v_rows = kvt[nq:nq + C_WIDTH].astype(BF16)
    for c in range(tm // vt):
        vc_ref[0, c] = v_rows[:, c * vt:(c + 1) * vt]


def _mixin(x, sh, sc, lw, cos_t, sin_t, tm):
    bn, l, _ = x.shape
    vt = min(MLA_KEY_CHUNK, tm)
    const2 = lambda b, t: (0, 0)
    const3 = lambda b, t: (0, 0, 0)
    in_specs = [
        pl.BlockSpec((1, tm, D_MODEL), lambda b, t: (b, t, 0)),
        pl.BlockSpec((1, 1, D_MODEL), lambda b, t: (b, 0, 0)),
        pl.BlockSpec((1, 1, D_MODEL), lambda b, t: (b, 0, 0)),
        pl.BlockSpec((1, D_MODEL), const2),
        pl.BlockSpec((IN_AUG, D_MODEL), const2),
        pl.BlockSpec((A_WIDTH, 1), const2),
        pl.BlockSpec((A_HEADS, CHUNK, CHUNK), const3),
        pl.BlockSpec((A_HEADS, 1, CHUNK), const3),
        pl.BlockSpec((HEAD_DIM, 1), const2),
        pl.BlockSpec((HEAD_DIM, 1), const2),
        pl.BlockSpec((C_Q_LORA, 1), const2),
        pl.BlockSpec((C_HEADS * (C_NOPE + 2 * C_ROPE), C_Q_LORA), const2),
        pl.BlockSpec((C_KV_LORA, 1), const2),
        pl.BlockSpec((C_HEADS * (C_NOPE + C_VDIM), C_KV_LORA), const2),
        pl.BlockSpec((C_NOPE, 1), const2),
        pl.BlockSpec((C_ROPE, 1), const2),
        pl.BlockSpec((C_ROPE, 1), const2),
        pl.BlockSpec((C_NOPE, 1), const2),
        pl.BlockSpec((C_ROPE, 1), const2),
        pl.BlockSpec((C_ROPE, 1), const2),
        pl.BlockSpec((C_ROPE, tm), lambda b, t: (0, t)),
        pl.BlockSpec((C_ROPE, tm), lambda b, t: (0, t)),
    ]
    out_shape = (
        jax.ShapeDtypeStruct((bn, A_WIDTH, l), BF16),
        jax.ShapeDtypeStruct((bn, B_HEADS, QPAD, l), BF16),
        jax.ShapeDtypeStruct((bn, l, B_WIDTH), BF16),
        jax.ShapeDtypeStruct((bn, B_WIDTH, l), BF16),
        jax.ShapeDtypeStruct((bn, C_HEADS, QPAD, l), BF16),
        jax.ShapeDtypeStruct((bn, C_HEADS, l, QPAD), BF16),
        jax.ShapeDtypeStruct((bn, l // vt, C_WIDTH, vt), BF16),
    )
    out_specs = (
        pl.BlockSpec((1, A_WIDTH, tm), lambda b, t: (b, 0, t)),
        pl.BlockSpec((1, B_HEADS, QPAD, tm), lambda b, t: (b, 0, 0, t)),
        pl.BlockSpec((1, tm, B_WIDTH), lambda b, t: (b, t, 0)),
        pl.BlockSpec((1, B_WIDTH, tm), lambda b, t: (b, 0, t)),
        pl.BlockSpec((1, C_HEADS, QPAD, tm), lambda b, t: (b, 0, 0, t)),
        pl.BlockSpec((1, C_HEADS, tm, QPAD), lambda b, t: (b, 0, t, 0)),
        pl.BlockSpec((1, tm // vt, C_WIDTH, vt), lambda b, t: (b, t, 0, 0)),
    )
    return pl.pallas_call(
        functools.partial(_mixin_body, tm=tm),
        grid=(bn, l // tm),
        in_specs=in_specs,
        out_specs=out_specs,
        out_shape=out_shape,
        compiler_params=_cparams(("arbitrary", "arbitrary")),
        name="mix_in",
    )(x, sh, sc, lw["norm_mix"], lw["w_in_t"], lw["avn"], lw["wst"], lw["bs"], lw["gq"], lw["gk"],
      lw["cqa"], lw["wq_t"], lw["ckva"], lw["wkv_t"], lw["gcqn"], lw["gcqr"], lw["gcqrp"], lw["gckn"],
      lw["gckr"], lw["gckrp"], cos_t, sin_t)


DEN_ROWS = 16


def _na_body(q_ref, k0_ref, k1_ref, k2_ref, k3_ref, v0_ref, v1_ref, v2_ref, v3_ref, kx_ref, vx_ref, t_ref,
             o_ref):
    k_refs = (k0_ref, k1_ref, k2_ref, k3_ref)
    v_refs = (v0_ref, v1_ref, v2_ref, v3_ref)
    nk = k0_ref.shape[1]
    for hh in range(2):
        q = q_ref[0, hh]
        rows = slice(hh * HEAD_DIM, (hh + 1) * HEAD_DIM)
        s = jnp.concatenate([jnp.dot(kr[0], q, preferred_element_type=F32) for kr in k_refs], axis=0)
        s = (s + t_ref[hh, 0]).astype(BF16)
        sx = jnp.dot(kx_ref[0], q, preferred_element_type=F32).astype(BF16)
        m = jnp.maximum(jnp.max(s, axis=0, keepdims=True), jnp.max(sx, axis=0, keepdims=True))
        p = jnp.exp2(s - m)
        px = jnp.exp2(sx - m)
        o = jnp.dot(jnp.concatenate([vx_ref[0, rows], jnp.ones((DEN_ROWS, px.shape[0]), BF16)], axis=0), px,
                    preferred_element_type=F32)
        ones = jnp.ones((DEN_ROWS, nk), BF16)
        for j, vr in enumerate(v_refs):
            o = o + jnp.dot(jnp.concatenate([vr[0, rows], ones], axis=0), p[j * nk:(j + 1) * nk],
                            preferred_element_type=F32)
        o_ref[0, rows] = (o[0:HEAD_DIM] / o[HEAD_DIM:HEAD_DIM + 1]).astype(BF16)


def _na_attention(qb, kb, vb, kxb, vxb, table):
    bn, _, _, l = qb.shape
    nq = NA_QROWS * GRID_W
    nk = NA_KBLK_ROWS * GRID_W
    nblk = l // nq
    nkb = l // nk
    nctx = kxb.shape[1]

    def kmap(j):
        return lambda b, h, i: (b, jnp.clip(2 * i - 1 + j, 0, nkb - 1), h)

    def vmap_(j):
        return lambda b, h, i: (b, h, jnp.clip(2 * i - 1 + j, 0, nkb - 1))

    def tmap(b, h, i):
        return (h, jnp.where(i == 0, 0, jnp.where(i == nblk - 1, 2, 1)), 0, 0)

    in_specs = ([pl.BlockSpec((1, 2, QPAD, nq), lambda b, h, i: (b, h, 0, i))]
                + [pl.BlockSpec((1, nk, LANE), kmap(j)) for j in range(4)]
                + [pl.BlockSpec((1, 2 * HEAD_DIM, nk), vmap_(j)) for j in range(4)]
                + [pl.BlockSpec((1, nctx, LANE), lambda b, h, i: (b, 0, h)),
                   pl.BlockSpec((1, 2 * HEAD_DIM, nctx), lambda b, h, i: (b, h, 0)),
                   pl.BlockSpec((2, 1, 4 * nk, nq), tmap)])
    return pl.pallas_call(
        _na_body,
        grid=(bn, B_HEADS // 2, nblk),
        in_specs=in_specs,
        out_specs=pl.BlockSpec((1, 2 * HEAD_DIM, nq), lambda b, h, i: (b, h, i)),
        out_shape=jax.ShapeDtypeStruct((bn, B_WIDTH, l), BF16),
        compiler_params=_cparams(("arbitrary", "arbitrary", "arbitrary")),
        name="na_attention",
    )(qb, kb, kb, kb, kb, vb, vb, vb, vb, kxb, vxb, table)


def _na_bias_table(rpb, rows):
    nblk = rows // NA_QROWS
    qc = np.arange(GRID_W)
    kc = np.arange(GRID_W)
    c0 = np.clip(qc - NA_COLS // 2, 0, GRID_W - NA_COLS)
    valid_col = (kc[:, None] >= c0[None, :]) & (kc[:, None] < c0[None, :] + NA_COLS)
    dc = np.clip(kc[:, None] - qc[None, :], -(NA_COLS - 1), NA_COLS - 1) + NA_COLS - 1
    dc_onehot = (dc[None] == np.arange(2 * NA_COLS - 1)[:, None, None]).astype(np.float32)
    tabs = []
    for i in (0, 1, nblk - 1):
        kr = NA_KBLK_ROWS * (2 * i - 1) + np.arange(4 * NA_KBLK_ROWS)
        qr = NA_QROWS * i + np.arange(NA_QROWS)
        r0 = np.clip(qr - NA_ROWS // 2, 0, rows - NA_ROWS)
        valid_row = ((kr[:, None] >= r0[None, :]) & (kr[:, None] < r0[None, :] + NA_ROWS)
                     & (kr[:, None] >= 0) & (kr[:, None] < rows))
        dr = np.clip(kr[:, None] - qr[None, :] + NA_ROWS - 1, 0, 2 * NA_ROWS - 2)
        bias = jnp.einsum("hkqd,dcx->hkcqx", rpb[:, dr].astype(F32), dc_onehot, precision=HIGHEST)
        valid = valid_row[:, None, :, None] & valid_col[None, :, None, :]
        tabs.append(jnp.where(valid[None], bias.astype(F32) * LOG2E, NEG_INF).reshape(
            rpb.shape[0], 4 * NA_KBLK_ROWS * GRID_W, NA_QROWS * GRID_W))
    return jnp.stack(tabs, axis=1)


def _flash_body(*refs, has_extra):
    if has_extra:
        q_ref, k_ref, v_ref, kx_ref, vx_ref, o_ref, m_sc, l_sc, acc_sc = refs
    else:
        q_ref, k_ref, v_ref, o_ref, m_sc, l_sc, acc_sc = refs
    kv = pl.program_id(3)
    nkv = pl.num_programs(3)

    @pl.when(kv == 0)
    def _():
        m_sc[...] = jnp.full(m_sc.shape, -jnp.inf, F32)
        l_sc[...] = jnp.zeros(l_sc.shape, F32)
        acc_sc[...] = jnp.zeros(acc_sc.shape, F32)

    def step(k, v):
        s = jnp.dot(k, q_ref[0, 0], preferred_element_type=F32)
        m_old = m_sc[...]
        m_new = jnp.maximum(m_old, jnp.max(s, axis=0, keepdims=True))
        alpha = jnp.exp2(m_old - m_new)
        p = jnp.exp2(s - m_new)
        l_sc[...] = alpha * l_sc[...] + jnp.sum(p, axis=0, keepdims=True)
        acc_sc[...] = alpha * acc_sc[...] + jnp.dot(v, p.astype(BF16), preferred_element_type=F32)
        m_sc[...] = m_new

    step(k_ref[0, 0], v_ref[0])

    @pl.when(kv == nkv - 1)
    def _():
        if has_extra:
            step(kx_ref[0, 0], vx_ref[0])
        o_ref[0] = (acc_sc[...] / l_sc[...]).astype(BF16)


def _flash(q, k, v, k_spec, kx=None, vx=None, kx_spec=None, *, tq, tk, name):
    bn, nh, _, lq = q.shape
    lk = v.shape[2]
    has_extra = kx is not None
    in_specs = [pl.BlockSpec((1, 1, QPAD, tq), lambda b, h, i, j: (b, h, 0, i)),
                k_spec,
                pl.BlockSpec((1, HEAD_DIM, tk), lambda b, h, i, j: (b, h, j))]
    args = [q, k, v]
    if has_extra:
        nx = vx.shape[2]
        in_specs += [kx_spec, pl.BlockSpec((1, HEAD_DIM, nx), lambda b, h, i, j: (b, h, 0))]
        args += [kx, vx]
    return pl.pallas_call(
        functools.partial(_flash_body, has_extra=has_extra),
        grid=(bn, nh, lq // tq, lk // tk),
        in_specs=in_specs,
        out_specs=pl.BlockSpec((1, HEAD_DIM, tq), lambda b, h, i, j: (b, h, i)),
        out_shape=jax.ShapeDtypeStruct((bn, nh * HEAD_DIM, lq), BF16),
        scratch_shapes=[pltpu.VMEM((1, tq), F32), pltpu.VMEM((1, tq), F32), pltpu.VMEM((HEAD_DIM, tq), F32)],
        compiler_params=_cparams(("arbitrary", "arbitrary", "arbitrary", "arbitrary")),
        name=name,
    )(*args)


def _mla_body(q_ref, k_ref, v_ref, kx_ref, vx_ref, o_ref, *, tk, nchunks):
    q = q_ref[0, 0]
    tq = q.shape[1]

    def scores(k):
        return jnp.dot(k, q, preferred_element_type=F32).astype(BF16)

    def absorb(s, v, carry):
        m, den, acc = carry
        m_new = jnp.maximum(m, jnp.max(s, axis=0, keepdims=True).astype(F32))
        alpha = jnp.exp2(m - m_new)
        p = jnp.exp2(s - m_new.astype(BF16))
        den = alpha * den + jnp.sum(p.astype(F32), axis=0, keepdims=True)
        acc = alpha * acc + jnp.dot(v, p, preferred_element_type=F32)
        return m_new, den, acc

    carry = (jnp.full((1, tq), -jnp.inf, F32), jnp.zeros((1, tq), F32), jnp.zeros((HEAD_DIM, tq), F32))
    s_cur = scores(k_ref[0, 0, 0:tk, :])
    for j in range(nchunks):
        if j + 1 < nchunks:
            s_next = scores(k_ref[0, 0, (j + 1) * tk:(j + 2) * tk, :])
        else:
            s_next = scores(kx_ref[0, 0])
        carry = absorb(s_cur, v_ref[0, j], carry)
        s_cur = s_next
    _, den, acc = absorb(s_cur, vx_ref[0, 0], carry)
    o_ref[0] = (acc / den).astype(BF16)


def _mla_attention(q, k, v, kx, vx, tq):
    bn, nh, _, l = q.shape
    nchunks, tk = v.shape[1], v.shape[3]
    nc = kx.shape[2]
    return pl.pallas_call(
        functools.partial(_mla_body, tk=tk, nchunks=nchunks),
        grid=(bn, nh, l // tq),
        in_specs=[pl.BlockSpec((1, 1, QPAD, tq), lambda b, h, i: (b, h, 0, i)),
                  pl.BlockSpec((1, 1, l, QPAD), lambda b, h, i: (b, h, 0, 0)),
                  pl.BlockSpec((1, nchunks, HEAD_DIM, tk), lambda b, h, i: (b, 0, h, 0)),
                  pl.BlockSpec((1, 1, nc, QPAD), lambda b, h, i: (b, h, 0, 0)),
                  pl.BlockSpec((1, 1, HEAD_DIM, nc), lambda b, h, i: (b, 0, h, 0))],
        out_specs=pl.BlockSpec((1, HEAD_DIM, tq), lambda b, h, i: (b, h, i)),
        out_shape=jax.ShapeDtypeStruct((bn, nh * HEAD_DIM, l), BF16),
        compiler_params=_cparams(("arbitrary", "arbitrary", "arbitrary")),
        name="mla_attention",
    )(q, k, v, kx, vx)


TOKEN_ROWS = 8


def _first_argmax_rows(v, row_id):
    vmax = jnp.max(v, axis=0, keepdims=True)
    idx = jnp.min(jnp.where(v == vmax, row_id, float(v.shape[0])), axis=0, keepdims=True)
    return vmax, idx


def _mixout_body(x_ref, oa_ref, ob_ref, oc_ref, wt_ref, g1_ref, sh2_ref, sc2_ref, gn2_ref, wr_ref, tri_ref,
                 *rest, tm, steps, has_base):
    if has_base:
        cnt_in_ref = rest[0]
    xo_ref, h2_ref, ei_ref, wf_ref, cnt_ref, base_sc = rest[-6:]
    i = pl.program_id(0)

    @pl.when(i == 0)
    def _():
        if has_base:
            base_sc[...] = cnt_in_ref[:, 0:1]
        else:
            base_sc[...] = jnp.zeros(base_sc.shape, F32)

    ot = jnp.concatenate([oa_ref[0], ob_ref[0], oc_ref[0]], axis=0)
    out_t = jnp.dot(wt_ref[...], ot, preferred_element_type=F32)
    xn = x_ref[0] + g1_ref[0] * out_t.T
    xo_ref[0] = xn
    h2 = xn * lax.rsqrt(jnp.mean(xn * xn, axis=-1, keepdims=True) + EPS) * gn2_ref[...]
    h2 = h2 * (1.0 + sc2_ref[0]) + sh2_ref[0]
    for s in range(TOKEN_ROWS):
        h2_ref[pl.ds(s, tm, stride=TOKEN_ROWS), :] = h2[:, s * LANE:(s + 1) * LANE]

    h_hi = h2.astype(BF16)
    h_lo = (h2 - h_hi.astype(F32)).astype(BF16)
    lg = (jnp.dot(h_hi, wr_ref[0], preferred_element_type=F32) + jnp.dot(h_lo, wr_ref[0], preferred_element_type=F32)
          + jnp.dot(h_hi, wr_ref[1], preferred_element_type=F32))
    lt = lg.T
    gl = lt[0:N_GROUPS]
    rid = lax.broadcasted_iota(jnp.int32, (N_GROUPS, tm), 0).astype(F32)
    gmax, g_idx = _first_argmax_rows(gl, rid)
    g_gate = 1.0 / jnp.sum(jnp.exp(gl - gmax), axis=0, keepdims=True)
    e_sel = jnp.zeros((EXPERTS_PER_GROUP, tm), F32)
    for g in range(N_GROUPS):
        lo = N_GROUPS + g * EXPERTS_PER_GROUP
        e_sel = jnp.where(g_idx == float(g), lt[lo:lo + EXPERTS_PER_GROUP], e_sel)
    v1, j1 = _first_argmax_rows(e_sel, rid)
    v2, j2 = _first_argmax_rows(jnp.where(rid == j1, -jnp.inf, e_sel), rid)
    t21 = jnp.exp(v2 - v1)
    w1 = g_gate / (1.0 + t21)
    w2 = g_gate * t21 / (1.0 + t21)
    e1 = g_idx * float(EXPERTS_PER_GROUP) + j1
    e2 = g_idx * float(EXPERTS_PER_GROUP) + j2

    eid = lax.broadcasted_iota(jnp.int32, (N_EXPERTS, tm), 0).astype(F32)
    oh1 = (eid == e1).astype(F32)
    oh2 = (eid == e2).astype(F32)
    tri = tri_ref[...]
    cum1 = jnp.dot(oh1.astype(BF16), tri, preferred_element_type=F32)
    cum2 = jnp.dot(oh2.astype(BF16), tri, preferred_element_type=F32)
    tot1 = jnp.sum(oh1, axis=1, keepdims=True)
    tot2 = jnp.sum(oh2, axis=1, keepdims=True)
    base = base_sc[...]
    r1 = jnp.sum(oh1 * (base + cum1), axis=0, keepdims=True)
    r2 = jnp.sum(oh2 * (base + tot1 + cum2), axis=0, keepdims=True)
    live = jnp.where(i < steps, 1.0, 0.0)
    base_new = base + live * (tot1 + tot2)
    base_sc[...] = base_new
    cnt_ref[...] = jnp.broadcast_to(base_new, cnt_ref.shape)

    zeros4 = jnp.zeros((4, tm), F32)
    ei_ref[...] = jnp.concatenate([e1, e2, r1, r2, zeros4], axis=0).astype(jnp.int32)
    wpad = jnp.concatenate([w1, w2, jnp.zeros((LANE - 2, tm), F32)], axis=0)
    wf_ref[...] = wpad.T


def _mixout(x, oa, ob, oc, g1, sh2, sc2, lw, tm, n_rows, row0=0, bufs=None, cnt_in=None):
    bn, l, _ = x.shape
    nt = l // tm
    steps = bn * nt
    extra = 0
    if bufs is None and n_rows > bn * l:
        assert n_rows - bn * l == tm, "spare rows must be exactly one tile"
        extra = 1
    blk0 = row0 // tm

    def bt(i):
        ii = jnp.minimum(i, steps - 1)
        return ii // nt, ii % nt

    const2 = lambda i: (0, 0)
    modspec = pl.BlockSpec((1, 1, D_MODEL), lambda i: (bt(i)[0], 0, 0))
    rowspec = pl.BlockSpec((1, tm, D_MODEL), lambda i: (bt(i)[0], bt(i)[1], 0))
    tri = jnp.asarray(np.triu(np.ones((tm, tm), np.float32), k=1), BF16)
    in_specs = [rowspec,
                pl.BlockSpec((1, A_WIDTH, tm), lambda i: (bt(i)[0], 0, bt(i)[1])),
                pl.BlockSpec((1, B_WIDTH, tm), lambda i: (bt(i)[0], 0, bt(i)[1])),
                pl.BlockSpec((1, C_WIDTH, tm), lambda i: (bt(i)[0], 0, bt(i)[1])),
                pl.BlockSpec((D_MODEL, D_MODEL), const2),
                modspec, modspec, modspec,
                pl.BlockSpec((1, D_MODEL), const2),
                pl.BlockSpec((2, D_MODEL, ROUTER_PAD), lambda i: (0, 0, 0)),
                pl.BlockSpec((tm, tm), const2)]
    args = [x, oa, ob, oc, lw["w_out_t"], g1, sh2, sc2, lw["norm_ffn"], lw["w_route"], tri]
    aliases = {}
    if bufs is not None:
        in_specs.append(pl.BlockSpec((N_EXPERTS, LANE), const2))
        args.append(cnt_in)
        aliases = {len(args): 1, len(args) + 1: 2, len(args) + 2: 3}
        in_specs += [pl.BlockSpec(memory_space=pl.ANY)] * 3
        args += list(bufs)
    return pl.pallas_call(
        functools.partial(_mixout_body, tm=tm, steps=steps, has_base=bufs is not None),
        grid=(steps + extra,),
        in_specs=in_specs,
        out_specs=(rowspec,
                   pl.BlockSpec((tm * TOKEN_ROWS, LANE), lambda i: (blk0 + i, 0)),
                   pl.BlockSpec((8, tm), lambda i: (0, blk0 + i)),
                   pl.BlockSpec((tm, LANE), lambda i: (blk0 + i, 0)),
                   pl.BlockSpec((N_EXPERTS, LANE), const2)),
        out_shape=(jax.ShapeDtypeStruct((bn, l, D_MODEL), F32),
                   jax.ShapeDtypeStruct((n_rows * TOKEN_ROWS, LANE), F32),
                   jax.ShapeDtypeStruct((8, n_rows), jnp.int32),
                   jax.ShapeDtypeStruct((n_rows, LANE), F32),
                   jax.ShapeDtypeStruct((N_EXPERTS, LANE), F32)),
        scratch_shapes=[pltpu.VMEM((N_EXPERTS, 1), F32)],
        input_output_aliases=aliases,
        compiler_params=_cparams(("arbitrary",)),
        name="mix_out",
    )(*args)


def _tile_rows(idx):
    return pl.ds(pl.multiple_of(idx * TOKEN_ROWS, TOKEN_ROWS), TOKEN_ROWS)


def _slot(ps_ref, ei_ref, k, r):
    return ps_ref[ei_ref[k, r]] + ei_ref[TOP_K + k, r]


def _dispatch_body(ps_ref, ei_ref, h_ref, xs_in_ref, xs_ref, sem, *, td):
    del xs_in_ref

    def one(r, carry):
        for k in range(TOP_K):
            pltpu.make_async_copy(h_ref.at[_tile_rows(r)], xs_ref.at[_tile_rows(_slot(ps_ref, ei_ref, k, r))],
                                  sem).start(priority=k)
        return carry
    lax.fori_loop(0, td, one, 0, unroll=8)
    for k in range(TOP_K):
        pltpu.make_async_copy(h_ref, xs_ref.at[pl.ds(0, td * TOKEN_ROWS)], sem).wait()


def _dispatch(pstarts, ei, h2t, n_slots, td):
    n_tok = ei.shape[1]
    xs0 = jnp.zeros((n_slots * TOKEN_ROWS, LANE), F32)
    grid_spec = pltpu.PrefetchScalarGridSpec(
        num_scalar_prefetch=1,
        grid=(n_tok // td,),
        in_specs=[pl.BlockSpec((8, td), lambda i, ps: (0, i), memory_space=pltpu.SMEM),
                  pl.BlockSpec((td * TOKEN_ROWS, LANE), lambda i, ps: (i, 0)),
                  pl.BlockSpec(memory_space=pl.ANY)],
        out_specs=pl.BlockSpec(memory_space=pl.ANY),
        scratch_shapes=[pltpu.SemaphoreType.DMA(())],
    )
    return pl.pallas_call(
        functools.partial(_dispatch_body, td=td),
        grid_spec=grid_spec,
        out_shape=jax.ShapeDtypeStruct(xs0.shape, F32),
        input_output_aliases={3: 0},
        compiler_params=_cparams(("arbitrary",)),
        name="moe_dispatch",
    )(pstarts, ei, h2t, xs0)


def _experts_body(be_ref, bi_ref, used_ref, x_ref, w1_ref, w3_ref, w2_ref, y_ref, w1b, w3b, w2b, *, tb):
    i = pl.program_id(0)
    used = used_ref[i] > 0

    @pl.when(used)
    def _():
        prev_e = be_ref[jnp.maximum(i - 1, 0)]

        @pl.when((i == 0) | (prev_e != be_ref[i]))
        def _():
            w1b[...] = w1_ref[0, 0].astype(BF16)
            w3b[...] = w3_ref[0, 0].astype(BF16)
            w2b[...] = w2_ref[0, 0].astype(BF16)

        xb = jnp.concatenate([x_ref[pl.ds(s, tb, stride=TOKEN_ROWS), :] for s in range(TOKEN_ROWS)],
                             axis=1).astype(BF16)
        a = jnp.dot(xb, w1b[...], preferred_element_type=F32)
        b = jnp.dot(xb, w3b[...], preferred_element_type=F32)
        hm = (a * (1.0 / (1.0 + jnp.exp(-a))) * b).astype(BF16)
        y = jnp.dot(hm, w2b[...], preferred_element_type=F32)
        for s in range(TOKEN_ROWS):
            y_ref[pl.ds(s, tb, stride=TOKEN_ROWS), :] = y[:, s * LANE:(s + 1) * LANE]

    @pl.when(jnp.logical_not(used))
    def _():
        y_ref[...] = jnp.zeros(y_ref.shape, F32)


def _experts(xs, block_e, block_i, used, w1, w3, w2, layer, tb):
    nb = block_e.shape[0]
    wspec_up = pl.BlockSpec((1, 1, D_MODEL, D_EXPERT), lambda i, be, bi, us: (layer, be[i], 0, 0))
    grid_spec = pltpu.PrefetchScalarGridSpec(
        num_scalar_prefetch=3,
        grid=(nb,),
        in_specs=[pl.BlockSpec((tb * TOKEN_ROWS, LANE), lambda i, be, bi, us: (bi[i], 0)),
                  wspec_up, wspec_up,
                  pl.BlockSpec((1, 1, D_EXPERT, D_MODEL), lambda i, be, bi, us: (layer, be[i], 0, 0))],
        out_specs=pl.BlockSpec((tb * TOKEN_ROWS, LANE), lambda i, be, bi, us: (i, 0)),
        scratch_shapes=[pltpu.VMEM((D_MODEL, D_EXPERT), BF16), pltpu.VMEM((D_MODEL, D_EXPERT), BF16),
                        pltpu.VMEM((D_EXPERT, D_MODEL), BF16)],
    )
    return pl.pallas_call(
        functools.partial(_experts_body, tb=tb),
        grid_spec=grid_spec,
        out_shape=jax.ShapeDtypeStruct(xs.shape, F32),
        compiler_params=_cparams(("arbitrary",)),
        name="moe_experts",
    )(block_e, block_i, used, xs, w1, w3, w2)


def _combine_body(ps_ref, ei_ref, wf_ref, x_ref, g2_ref, ys_ref, o_ref, buf0, buf1, sem, *, tm):
    bufs = (buf0, buf1)

    def one(r, carry):
        for k in range(TOP_K):
            pltpu.make_async_copy(ys_ref.at[_tile_rows(_slot(ps_ref, ei_ref, k, r))], bufs[k].at[_tile_rows(r)],
                                  sem).start(priority=k)
        return carry
    lax.fori_loop(0, tm, one, 0, unroll=8)
    for k in range(TOP_K):
        pltpu.make_async_copy(ys_ref.at[pl.ds(0, tm * TOKEN_ROWS)], bufs[k], sem).wait()
    wf = wf_ref[...]
    y = None
    for k in range(TOP_K):
        yk = jnp.concatenate([bufs[k][pl.ds(s, tm, stride=TOKEN_ROWS), :] for s in range(TOKEN_ROWS)], axis=1)
        yk = wf[:, k:k + 1] * yk
        y = yk if y is None else y + yk
    o_ref[0] = x_ref[0] + g2_ref[0] * y


def _combine(x, pstarts, ei, wf, ys, g2, row0, tm):
    bn, l, _ = x.shape
    nt = l // tm
    t0 = row0 // tm
    rowspec = pl.BlockSpec((1, tm, D_MODEL), lambda b, t, ps: (b, t, 0))
    grid_spec = pltpu.PrefetchScalarGridSpec(
        num_scalar_prefetch=1,
        grid=(bn, nt),
        in_specs=[pl.BlockSpec((8, tm), lambda b, t, ps: (0, t0 + b * nt + t), memory_space=pltpu.SMEM),
                  pl.BlockSpec((tm, LANE), lambda b, t, ps: (t0 + b * nt + t, 0)),
                  rowspec,
                  pl.BlockSpec((1, 1, D_MODEL), lambda b, t, ps: (b, 0, 0)),
                  pl.BlockSpec(memory_space=pl.ANY)],
        out_specs=rowspec,
        scratch_shapes=[pltpu.VMEM((tm * TOKEN_ROWS, LANE), F32), pltpu.VMEM((tm * TOKEN_ROWS, LANE), F32),
                        pltpu.SemaphoreType.DMA(())],
    )
    return pl.pallas_call(
        functools.partial(_combine_body, tm=tm),
        grid_spec=grid_spec,
        out_shape=jax.ShapeDtypeStruct(x.shape, F32),
        compiler_params=_cparams(("arbitrary", "arbitrary")),
        name="moe_combine",
    )(pstarts, ei, wf, x, g2, ys)


def _expert_blocks(counts, n_assign, tb):
    pcounts = (counts + tb - 1) // tb * tb
    pends = jnp.cumsum(pcounts)
    pstarts = (pends - pcounts).astype(jnp.int32)
    nb = -(-n_assign // tb) + N_EXPERTS
    n_used = pends[-1] // tb
    blk = jnp.arange(nb, dtype=jnp.int32)
    used = blk < n_used
    last = jnp.maximum(n_used - 1, 0).astype(jnp.int32)
    block_i = jnp.where(used, blk, last)
    block_e = jnp.sum((pends[None, :] <= (block_i * tb)[:, None]).astype(jnp.int32), axis=1)
    block_e = jnp.minimum(block_e, N_EXPERTS - 1)
    return pstarts, block_e, block_i, used.astype(jnp.int32), nb


_ROPE_PERM = np.concatenate([np.arange(8, 16), np.arange(0, 8), np.arange(24, 32), np.arange(16, 24)])
_ROPE_SIGN = np.concatenate([-np.ones(8), np.ones(8), -np.ones(8), np.ones(8)]).astype(np.float32)


def _rope_tables_t(l):
    half = C_ROPE // 2
    inv = ROPE_BASE ** (-jnp.arange(0, half, 2, dtype=F32) / half)
    pos = jnp.arange(l)
    ang_r = (pos // GRID_W).astype(F32)[None, :] * inv[:, None]
    ang_c = (pos % GRID_W).astype(F32)[None, :] * inv[:, None]
    cos_t = jnp.concatenate([jnp.cos(ang_r)] * 2 + [jnp.cos(ang_c)] * 2, axis=0)
    sin_t = jnp.concatenate([jnp.sin(ang_r)] * 2 + [jnp.sin(ang_c)] * 2, axis=0)
    return cos_t, sin_t


def _col(v):
    return v.astype(F32)[:, None]


def _layer_weights(i, p):
    w_in = p["w_in"][i]
    kr0 = IN_A + IN_B + C_Q_LORA + C_KV_LORA
    kr_cols = w_in[:, kr0:kr0 + C_ROPE]
    w_aug = jnp.concatenate([w_in, kr_cols[:, _ROPE_PERM] * _ROPE_SIGN], axis=1)

    wq = p["c_w_q_up"][i].reshape(C_Q_LORA, C_HEADS, C_QK)
    wq_n = wq[:, :, :C_NOPE].reshape(C_Q_LORA, C_HEADS * C_NOPE)
    wq_r = wq[:, :, C_NOPE:]
    wq_rp = (wq_r[:, :, _ROPE_PERM] * _ROPE_SIGN).reshape(C_Q_LORA, C_HEADS * C_ROPE)
    wq_all = jnp.concatenate([wq_n, wq_r.reshape(C_Q_LORA, C_HEADS * C_ROPE), wq_rp], axis=1)

    wkv = p["c_w_kv_up"][i].reshape(C_KV_LORA, C_HEADS, C_NOPE + C_VDIM)
    wkv_all = jnp.concatenate([wkv[:, :, :C_NOPE].reshape(C_KV_LORA, C_HEADS * C_NOPE),
                               wkv[:, :, C_NOPE:].reshape(C_KV_LORA, C_HEADS * C_VDIM)], axis=1)

    cqn = p["c_q_norm"][i].astype(F32) * (C_QK ** -0.5 * LOG2E)
    ckn = p["c_k_norm"][i].astype(F32)
    w_route = jnp.concatenate([p["moe_w_group"][i], p["moe_w_router"][i],
                               jnp.zeros((D_MODEL, ROUTER_PAD - N_GROUPS - N_EXPERTS), F32)], axis=1)
    return {
        "norm_mix": p["norm_mix"][i][None, :],
        "w_in_t": w_aug.T.astype(BF16),
        "avn": _col(p["a_v_norm"][i]),
        "wst": jnp.transpose(p["a_w_s"][i], (0, 2, 1)).astype(BF16),
        "bs": p["a_b_s"][i][:, None, :],
        "gq": _col(p["b_q_norm"][i] * (HEAD_DIM ** -0.5 * LOG2E)),
        "gk": _col(p["b_k_norm"][i]),
        "cqa": _col(p["c_q_a_norm"][i]),
        "wq_t": wq_all.T.astype(BF16),
        "ckva": _col(p["c_kv_a_norm"][i]),
        "wkv_t": wkv_all.T.astype(BF16),
        "gcqn": _col(cqn[:C_NOPE]),
        "gcqr": _col(cqn[C_NOPE:]),
        "gcqrp": _col(cqn[C_NOPE:][_ROPE_PERM]),
        "gckn": _col(ckn[:C_NOPE]),
        "gckr": _col(ckn[C_NOPE:]),
        "gckrp": _col(ckn[C_NOPE:][_ROPE_PERM]),
        "w_out_t": p["w_out"][i].T.astype(BF16),
        "norm_ffn": p["norm_ffn"][i][None, :],
        "w_route": jnp.stack([w_route.astype(BF16), (w_route - w_route.astype(BF16).astype(F32)).astype(BF16)]),
    }


def _pick_tile(l, pref):
    t = min(l, pref)
    while l % t:
        t //= 2
    return t


def kernel(x, c, ctx, c_ctx, w_ada, b_ada, norm_mix, w_in, a_v_norm, a_w_s, a_b_s, b_q_norm, b_k_norm, b_rpb,
           c_q_a_norm, c_w_q_up, c_kv_a_norm, c_w_kv_up, c_q_norm, c_k_norm, w_out, norm_ffn,
           moe_w_group, moe_w_router, moe_w1, moe_w3, moe_w2):
    p = dict(norm_mix=norm_mix, w_in=w_in, a_v_norm=a_v_norm, a_w_s=a_w_s, a_b_s=a_b_s, b_q_norm=b_q_norm,
             b_k_norm=b_k_norm, c_q_a_norm=c_q_a_norm, c_w_q_up=c_w_q_up, c_kv_a_norm=c_kv_a_norm,
             c_w_kv_up=c_w_kv_up, c_q_norm=c_q_norm, c_k_norm=c_k_norm, w_out=w_out, norm_ffn=norm_ffn,
             moe_w_group=moe_w_group, moe_w_router=moe_w_router)
    bn, l, d = x.shape
    nc = ctx.shape[1]
    depth = w_ada.shape[0]
    rows = l // GRID_W
    tm = _pick_tile(l, 512)
    tq = _pick_tile(l, MLA_QUERY_TILE)

    cos_t, sin_t = _rope_tables_t(l)
    cos_x = jnp.ones((C_ROPE, nc), F32)
    sin_x = jnp.zeros((C_ROPE, nc), F32)
    cc8 = jnp.concatenate([c, c_ctx[None, :], jnp.zeros((8 - bn - 1, d), F32)], axis=0)

    xc = ctx
    for i in range(depth):
        need_ctx = i < depth - 1
        lw = _layer_weights(i, p)
        mod = _ada(cc8, w_ada[i], b_ada[i][None, :])
        mods = [mod[:bn, j * d:(j + 1) * d][:, None, :] for j in range(6)]
        modx = [jnp.broadcast_to(mod[bn, j * d:(j + 1) * d][None, None, :], (bn, 1, d)) for j in range(6)]
        sh1, s1, g1, sh2, s2, g2 = mods
        sh1x, s1x, g1x, sh2x, s2x, g2x = modx

        oa, qb, kb, vb, qc, kc, vc = _mixin(x, sh1, s1, lw, cos_t, sin_t, tm)
        oax, qbx, kbx, vbx, qcx, kcx, vcx = _mixin(xc, sh1x, s1x, lw, cos_x, sin_x, nc)

        table = _na_bias_table(b_rpb[i], rows)
        ob = _na_attention(qb, kb, vb, kbx, vbx, table)
        oc = _mla_attention(qc, kc, vc, kcx, vcx, tq)
        n_lat = bn * l
        n_tok = n_lat + (bn * nc if need_ctx else 0)
        x, h2t, ei, wf, cnt = _mixout(x, oa, ob, oc, g1, sh2, s2, lw, tm, n_tok)
        if need_ctx:
            obx = _flash(qbx, kbx[:, None], vbx,
                         pl.BlockSpec((1, 1, nc, LANE), lambda b, h, ii, j: (b, 0, j, h // 2)),
                         tq=nc, tk=nc, name="ctx_dense_attention")
            ocx = _flash(qcx, kcx, vcx.reshape(bn, C_WIDTH, nc),
                         pl.BlockSpec((1, 1, nc, QPAD), lambda b, h, ii, j: (b, h, j, 0)),
                         tq=nc, tk=nc, name="ctx_mla_attention")
            xc, h2t, ei, wf, cnt = _mixout(xc, oax, obx, ocx, g1x, sh2x, s2x, lw, nc, n_tok, row0=n_lat,
                                           bufs=(h2t, ei, wf), cnt_in=cnt)

        counts = cnt[:, 0].astype(jnp.int32)
        pstarts, block_e, block_i, used, nb = _expert_blocks(counts, TOP_K * n_tok, MOE_BLOCK)
        xs = _dispatch(pstarts, ei, h2t, nb * MOE_BLOCK, tm)
        ys = _experts(xs, block_e, block_i, used, moe_w1, moe_w3, moe_w2, i, MOE_BLOCK)
        x = _combine(x, pstarts, ei, wf, ys, g2, 0, tm)
        if need_ctx:
            xc = _combine(xc, pstarts, ei, wf, ys, g2x, n_lat, nc)
    return x
```

```python
import functools

import numpy as np
import jax
import jax.numpy as jnp
from jax import lax
from jax.experimental import pallas as pl
from jax.experimental.pallas import tpu as pltpu

F32 = jnp.float32
BF16 = jnp.bfloat16

D_MODEL = 1024
GRID_W = 64
HEAD_DIM = 64
EPS = 1e-6
NEG_INF = -1e30

A_HEADS = 4
A_WIDTH = A_HEADS * HEAD_DIM
CHUNK = 128
B_HEADS = 6
B_WIDTH = B_HEADS * HEAD_DIM
NA_ROWS = 8
NA_COLS = 16
C_HEADS = 6
C_NOPE = 64
C_ROPE = 32
C_QK = C_NOPE + C_ROPE
C_VDIM = 64
C_Q_LORA = 384
C_KV_LORA = 256
C_WIDTH = C_HEADS * C_VDIM
ROPE_BASE = 10000.0
IN_A = 2 * A_WIDTH
IN_B = 3 * B_WIDTH
IN_C = C_Q_LORA + C_KV_LORA + C_ROPE
IN_AUG = IN_A + IN_B + IN_C + C_ROPE

N_GROUPS = 8
EXPERTS_PER_GROUP = 8
N_EXPERTS = N_GROUPS * EXPERTS_PER_GROUP
TOP_K = 2
D_EXPERT = D_MODEL // 2
ROUTER_PAD = 128

V7X_VMEM_LIMIT_BYTES = 56 * 1024 * 1024
LANE = 128
QPAD = 128

MLA_KEY_CHUNK = 256
MLA_QUERY_TILE = 1024
NA_QROWS = 8
NA_KBLK_ROWS = 4
MOE_BLOCK = 256

HIGHEST = lax.Precision.HIGHEST
LOG2E = 1.4426950408889634


def _cparams(sem):
    return pltpu.CompilerParams(dimension_semantics=sem, vmem_limit_bytes=V7X_VMEM_LIMIT_BYTES)


def _ada_body(c_ref, w_ref, b_ref, o_ref):
    cc = c_ref[...]
    s = cc * (1.0 / (1.0 + jnp.exp(-cc)))
    o_ref[...] = jnp.dot(s, w_ref[...], preferred_element_type=F32, precision=HIGHEST) + b_ref[...]


def _ada(cc8, w, b):
    n_out = w.shape[1]
    tn = 1024
    return pl.pallas_call(
        _ada_body,
        grid=(n_out // tn,),
        in_specs=[pl.BlockSpec((8, D_MODEL), lambda j: (0, 0)),
                  pl.BlockSpec((D_MODEL, tn), lambda j: (0, j)),
                  pl.BlockSpec((1, tn), lambda j: (0, j))],
        out_specs=pl.BlockSpec((8, tn), lambda j: (0, j)),
        out_shape=jax.ShapeDtypeStruct((8, n_out), F32),
        compiler_params=_cparams(("arbitrary",)),
        name="ada_mod",
    )(cc8, w, b)


def _rms_rows(v):
    return lax.rsqrt(jnp.mean(v * v, axis=0, keepdims=True) + EPS)


def _gelu_tanh(x):
    return 0.5 * x * (1.0 + jnp.tanh(0.7978845608028654 * (x + 0.044715 * (x * x * x))))


def _mixin_body(x_ref, sh_ref, sc_ref, gn_ref, wt_ref, avn_ref, wst_ref, bs_ref, gq_ref, gk_ref,
                cqa_ref, wq_ref, ckva_ref, wkv_ref, gcqn_ref, gcqr_ref, gcqrp_ref, gckn_ref, gckr_ref,
                gckrp_ref, cos_ref, sin_ref,
                oa_ref, qb_ref, kb_ref, vb_ref, qc_ref, kc_ref, vc_ref, *, tm):
    x = x_ref[0]
    h = x * lax.rsqrt(jnp.mean(x * x, axis=-1, keepdims=True) + EPS) * gn_ref[...]
    h = h * (1.0 + sc_ref[0]) + sh_ref[0]
    zt = lax.dot_general(wt_ref[...], h.astype(BF16), (((1,), (1,)), ((), ())),
                         preferred_element_type=F32)

    ga = _gelu_tanh(zt[0:IN_A])
    u = ga[0:A_WIDTH]
    v = ga[A_WIDTH:IN_A]
    vn = (v * _rms_rows(v) * avn_ref[...]).astype(BF16)
    for hd in range(A_HEADS):
        r0, r1 = hd * HEAD_DIM, (hd + 1) * HEAD_DIM
        for c in range(tm // CHUNK):
            c0, c1 = c * CHUNK, (c + 1) * CHUNK
            vm = jnp.dot(vn[r0:r1, c0:c1], wst_ref[hd], preferred_element_type=F32) + bs_ref[hd]
            oa_ref[0, r0:r1, c0:c1] = (u[r0:r1, c0:c1] * vm).astype(BF16)

    zb = zt[IN_A:IN_A + IN_B]
    zeros64 = jnp.zeros((HEAD_DIM, tm), F32)
    kn = []
    for hd in range(B_HEADS):
        q = zb[hd * HEAD_DIM:(hd + 1) * HEAD_DIM]
        k = zb[B_WIDTH + hd * HEAD_DIM:B_WIDTH + (hd + 1) * HEAD_DIM]
        qn = q * _rms_rows(q) * gq_ref[...]
        kn.append(k * _rms_rows(k) * gk_ref[...])
        pair = [qn, zeros64] if hd % 2 == 0 else [zeros64, qn]
        qb_ref[0, hd] = jnp.concatenate(pair, axis=0).astype(BF16)
    for p in range(B_HEADS // 2):
        kt = jnp.concatenate([kn[2 * p], kn[2 * p + 1]], axis=0)
        kb_ref[0, :, p * LANE:(p + 1) * LANE] = kt.T.astype(BF16)
    vb_ref[0] = zb[2 * B_WIDTH:3 * B_WIDTH].astype(BF16)

    zc = zt[IN_A + IN_B:IN_AUG]
    ql = zc[0:C_Q_LORA]
    kvl = zc[C_Q_LORA:C_Q_LORA + C_KV_LORA]
    kr = zc[C_Q_LORA + C_KV_LORA:C_Q_LORA + C_KV_LORA + C_ROPE]
    krp = zc[C_Q_LORA + C_KV_LORA + C_ROPE:C_Q_LORA + C_KV_LORA + 2 * C_ROPE]
    qln = (ql * _rms_rows(ql) * cqa_ref[...]).astype(BF16)
    kvln = (kvl * _rms_rows(kvl) * ckva_ref[...]).astype(BF16)
    qt = jnp.dot(wq_ref[...], qln, preferred_element_type=F32)
    kvt = jnp.dot(wkv_ref[...], kvln, preferred_element_type=F32)
    cos = cos_ref[...]
    sin = sin_ref[...]
    krn = _rms_rows(kr) * (gckr_ref[...] * kr * cos + gckrp_ref[...] * krp * sin)
    zeros32 = jnp.zeros((QPAD - C_QK, tm), F32)
    nq = C_HEADS * C_NOPE
    nr = C_HEADS * C_ROPE
    for hd in range(C_HEADS):
        qn = qt[hd * C_NOPE:(hd + 1) * C_NOPE]
        qn = qn * _rms_rows(qn) * gcqn_ref[...]
        qr = qt[nq + hd * C_ROPE:nq + (hd + 1) * C_ROPE]
        qrp = qt[nq + nr + hd * C_ROPE:nq + nr + (hd + 1) * C_ROPE]
        qrn = _rms_rows(qr) * (gcqr_ref[...] * qr * cos + gcqrp_ref[...] * qrp * sin)
        qc_ref[0, hd] = jnp.concatenate([qn, qrn, zeros32], axis=0).astype(BF16)
        kn_c = kvt[hd * C_NOPE:(hd + 1) * C_NOPE]
        kn_c = kn_c * _rms_rows(kn_c) * gckn_ref[...]
        kc_ref[0, hd] = jnp.concatenate([kn_c, krn, zeros32], axis=0).T.astype(BF16)
    vt = vc_ref.shape[3]
    v_rows = kvt[nq:nq + C_WIDTH].astype(BF16)
    for c in range(tm // vt):
        vc_ref[0, c] = v_rows[:, c * vt:(c + 1) * vt]


def _mixin(x, sh, sc, lw, cos_t, sin_t, tm):
    bn, l, _ = x.shape
    vt = min(MLA_KEY_CHUNK, tm)
    const2 = lambda b, t: (0, 0)
    const3 = lambda b, t: (0, 0, 0)
    in_specs = [
        pl.BlockSpec((1, tm, D_MODEL), lambda b, t: (b, t, 0)),
        pl.BlockSpec((1, 1, D_MODEL), lambda b, t: (b, 0, 0)),
        pl.BlockSpec((1, 1, D_MODEL), lambda b, t: (b, 0, 0)),
        pl.BlockSpec((1, D_MODEL), const2),
        pl.BlockSpec((IN_AUG, D_MODEL), const2),
        pl.BlockSpec((A_WIDTH, 1), const2),
        pl.BlockSpec((A_HEADS, CHUNK, CHUNK), const3),
        pl.BlockSpec((A_HEADS, 1, CHUNK), const3),
        pl.BlockSpec((HEAD_DIM, 1), const2),
        pl.BlockSpec((HEAD_DIM, 1), const2),
        pl.BlockSpec((C_Q_LORA, 1), const2),
        pl.BlockSpec((C_HEADS * (C_NOPE + 2 * C_ROPE), C_Q_LORA), const2),
        pl.BlockSpec((C_KV_LORA, 1), const2),
        pl.BlockSpec((C_HEADS * (C_NOPE + C_VDIM), C_KV_LORA), const2),
        pl.BlockSpec((C_NOPE, 1), const2),
        pl.BlockSpec((C_ROPE, 1), const2),
        pl.BlockSpec((C_ROPE, 1), const2),
        pl.BlockSpec((C_NOPE, 1), const2),
        pl.BlockSpec((C_ROPE, 1), const2),
        pl.BlockSpec((C_ROPE, 1), const2),
        pl.BlockSpec((C_ROPE, tm), lambda b, t: (0, t)),
        pl.BlockSpec((C_ROPE, tm), lambda b, t: (0, t)),
    ]
    out_shape = (
        jax.ShapeDtypeStruct((bn, A_WIDTH, l), BF16),
        jax.ShapeDtypeStruct((bn, B_HEADS, QPAD, l), BF16),
        jax.ShapeDtypeStruct((bn, l, B_WIDTH), BF16),
        jax.ShapeDtypeStruct((bn, B_WIDTH, l), BF16),
        jax.ShapeDtypeStruct((bn, C_HEADS, QPAD, l), BF16),
        jax.ShapeDtypeStruct((bn, C_HEADS, l, QPAD), BF16),
        jax.ShapeDtypeStruct((bn, l // vt, C_WIDTH, vt), BF16),
    )
    out_specs = (
        pl.BlockSpec((1, A_WIDTH, tm), lambda b, t: (b, 0, t)),
        pl.BlockSpec((1, B_HEADS, QPAD, tm), lambda b, t: (b, 0, 0, t)),
        pl.BlockSpec((1, tm, B_WIDTH), lambda b, t: (b, t, 0)),
        pl.BlockSpec((1, B_WIDTH, tm), lambda b, t: (b, 0, t)),
        pl.BlockSpec((1, C_HEADS, QPAD, tm), lambda b, t: (b, 0, 0, t)),
        pl.BlockSpec((1, C_HEADS, tm, QPAD), lambda b, t: (b, 0, t, 0)),
        pl.BlockSpec((1, tm // vt, C_WIDTH, vt), lambda b, t: (b, t, 0, 0)),
    )
    return pl.pallas_call(
        functools.partial(_mixin_body, tm=tm),
        grid=(bn, l // tm),
        in_specs=in_specs,
        out_specs=out_specs,
        out_shape=out_shape,
        compiler_params=_cparams(("arbitrary", "arbitrary")),
        name="mix_in",
    )(x, sh, sc, lw["norm_mix"], lw["w_in_t"], lw["avn"], lw["wst"], lw["bs"], lw["gq"], lw["gk"],
      lw["cqa"], lw["wq_t"], lw["ckva"], lw["wkv_t"], lw["gcqn"], lw["gcqr"], lw["gcqrp"], lw["gckn"],
      lw["gckr"], lw["gckrp"], cos_t, sin_t)


DEN_ROWS = 16


def _na_body(q_ref, k0_ref, k1_ref, k2_ref, k3_ref, v0_ref, v1_ref, v2_ref, v3_ref, kx_ref, vx_ref, t_ref,
             o_ref):
    k_refs = (k0_ref, k1_ref, k2_ref, k3_ref)
    v_refs = (v0_ref, v1_ref, v2_ref, v3_ref)
    nk = k0_ref.shape[1]
    for hh in range(2):
        q = q_ref[0, hh]
        rows = slice(hh * HEAD_DIM, (hh + 1) * HEAD_DIM)
        s = jnp.concatenate([jnp.dot(kr[0], q, preferred_element_type=F32) for kr in k_refs], axis=0)
        s = (s + t_ref[hh, 0]).astype(BF16)
        sx = jnp.dot(kx_ref[0], q, preferred_element_type=F32).astype(BF16)
        m = jnp.maximum(jnp.max(s, axis=0, keepdims=True), jnp.max(sx, axis=0, keepdims=True))
        p = jnp.exp2(s - m)
        px = jnp.exp2(sx - m)
        o = jnp.dot(jnp.concatenate([vx_ref[0, rows], jnp.ones((DEN_ROWS, px.shape[0]), BF16)], axis=0), px,
                    preferred_element_type=F32)
        ones = jnp.ones((DEN_ROWS, nk), BF16)
        for j, vr in enumerate(v_refs):
            o = o + jnp.dot(jnp.concatenate([vr[0, rows], ones], axis=0), p[j * nk:(j + 1) * nk],
                            preferred_element_type=F32)
        o_ref[0, rows] = (o[0:HEAD_DIM] / o[HEAD_DIM:HEAD_DIM + 1]).astype(BF16)


def _na_attention(qb, kb, vb, kxb, vxb, table):
    bn, _, _, l = qb.shape
    nq = NA_QROWS * GRID_W
    nk = NA_KBLK_ROWS * GRID_W
    nblk = l // nq
    nkb = l // nk
    nctx = kxb.shape[1]

    def kmap(j):
        return lambda b, h, i: (b, jnp.clip(2 * i - 1 + j, 0, nkb - 1), h)

    def vmap_(j):
        return lambda b, h, i: (b, h, jnp.clip(2 * i - 1 + j, 0, nkb - 1))

    def tmap(b, h, i):
        return (h, jnp.where(i == 0, 0, jnp.where(i == nblk - 1, 2, 1)), 0, 0)

    in_specs = ([pl.BlockSpec((1, 2, QPAD, nq), lambda b, h, i: (b, h, 0, i))]
                + [pl.BlockSpec((1, nk, LANE), kmap(j)) for j in range(4)]
                + [pl.BlockSpec((1, 2 * HEAD_DIM, nk), vmap_(j)) for j in range(4)]
                + [pl.BlockSpec((1, nctx, LANE), lambda b, h, i: (b, 0, h)),
                   pl.BlockSpec((1, 2 * HEAD_DIM, nctx), lambda b, h, i: (b, h, 0)),
                   pl.BlockSpec((2, 1, 4 * nk, nq), tmap)])
    return pl.pallas_call(
        _na_body,
        grid=(bn, B_HEADS // 2, nblk),
        in_specs=in_specs,
        out_specs=pl.BlockSpec((1, 2 * HEAD_DIM, nq), lambda b, h, i: (b, h, i)),
        out_shape=jax.ShapeDtypeStruct((bn, B_WIDTH, l), BF16),
        compiler_params=_cparams(("arbitrary", "arbitrary", "arbitrary")),
        name="na_attention",
    )(qb, kb, kb, kb, kb, vb, vb, vb, vb, kxb, vxb, table)


def _na_bias_table(rpb, rows):
    nblk = rows // NA_QROWS
    qc = np.arange(GRID_W)
    kc = np.arange(GRID_W)
    c0 = np.clip(qc - NA_COLS // 2, 0, GRID_W - NA_COLS)
    valid_col = (kc[:, None] >= c0[None, :]) & (kc[:, None] < c0[None, :] + NA_COLS)
    dc = np.clip(kc[:, None] - qc[None, :], -(NA_COLS - 1), NA_COLS - 1) + NA_COLS - 1
    dc_onehot = (dc[None] == np.arange(2 * NA_COLS - 1)[:, None, None]).astype(np.float32)
    tabs = []
    for i in (0, 1, nblk - 1):
        kr = NA_KBLK_ROWS * (2 * i - 1) + np.arange(4 * NA_KBLK_ROWS)
        qr = NA_QROWS * i + np.arange(NA_QROWS)
        r0 = np.clip(qr - NA_ROWS // 2, 0, rows - NA_ROWS)
        valid_row = ((kr[:, None] >= r0[None, :]) & (kr[:, None] < r0[None, :] + NA_ROWS)
                     & (kr[:, None] >= 0) & (kr[:, None] < rows))
        dr = np.clip(kr[:, None] - qr[None, :] + NA_ROWS - 1, 0, 2 * NA_ROWS - 2)
        bias = jnp.einsum("hkqd,dcx->hkcqx", rpb[:, dr].astype(F32), dc_onehot, precision=HIGHEST)
        valid = valid_row[:, None, :, None] & valid_col[None, :, None, :]
        tabs.append(jnp.where(valid[None], bias.astype(F32) * LOG2E, NEG_INF).reshape(
            rpb.shape[0], 4 * NA_KBLK_ROWS * GRID_W, NA_QROWS * GRID_W))
    return jnp.stack(tabs, axis=1)


def _flash_body(*refs, has_extra):
    if has_extra:
        q_ref, k_ref, v_ref, kx_ref, vx_ref, o_ref, m_sc, l_sc, acc_sc = refs
    else:
        q_ref, k_ref, v_ref, o_ref, m_sc, l_sc, acc_sc = refs
    kv = pl.program_id(3)
    nkv = pl.num_programs(3)

    @pl.when(kv == 0)
    def _():
        m_sc[...] = jnp.full(m_sc.shape, -jnp.inf, F32)
        l_sc[...] = jnp.zeros(l_sc.shape, F32)
        acc_sc[...] = jnp.zeros(acc_sc.shape, F32)

    def step(k, v):
        s = jnp.dot(k, q_ref[0, 0], preferred_element_type=F32)
        m_old = m_sc[...]
        m_new = jnp.maximum(m_old, jnp.max(s, axis=0, keepdims=True))
        alpha = jnp.exp2(m_old - m_new)
        p = jnp.exp2(s - m_new)
        l_sc[...] = alpha * l_sc[...] + jnp.sum(p, axis=0, keepdims=True)
        acc_sc[...] = alpha * acc_sc[...] + jnp.dot(v, p.astype(BF16), preferred_element_type=F32)
        m_sc[...] = m_new

    step(k_ref[0, 0], v_ref[0])

    @pl.when(kv == nkv - 1)
    def _():
        if has_extra:
            step(kx_ref[0, 0], vx_ref[0])
        o_ref[0] = (acc_sc[...] / l_sc[...]).astype(BF16)


def _flash(q, k, v, k_spec, kx=None, vx=None, kx_spec=None, *, tq, tk, name):
    bn, nh, _, lq = q.shape
    lk = v.shape[2]
    has_extra = kx is not None
    in_specs = [pl.BlockSpec((1, 1, QPAD, tq), lambda b, h, i, j: (b, h, 0, i)),
                k_spec,
                pl.BlockSpec((1, HEAD_DIM, tk), lambda b, h, i, j: (b, h, j))]
    args = [q, k, v]
    if has_extra:
        nx = vx.shape[2]
        in_specs += [kx_spec, pl.BlockSpec((1, HEAD_DIM, nx), lambda b, h, i, j: (b, h, 0))]
        args += [kx, vx]
    return pl.pallas_call(
        functools.partial(_flash_body, has_extra=has_extra),
        grid=(bn, nh, lq // tq, lk // tk),
        in_specs=in_specs,
        out_specs=pl.BlockSpec((1, HEAD_DIM, tq), lambda b, h, i, j: (b, h, i)),
        out_shape=jax.ShapeDtypeStruct((bn, nh * HEAD_DIM, lq), BF16),
        scratch_shapes=[pltpu.VMEM((1, tq), F32), pltpu.VMEM((1, tq), F32), pltpu.VMEM((HEAD_DIM, tq), F32)],
        compiler_params=_cparams(("arbitrary", "arbitrary", "arbitrary", "arbitrary")),
        name=name,
    )(*args)


def _mla_body(q_ref, k_ref, v_ref, kx_ref, vx_ref, o_ref, *, tk, nchunks):
    q = q_ref[0, 0]
    tq = q.shape[1]

    def scores(k):
        return jnp.dot(k, q, preferred_element_type=F32).astype(BF16)

    def absorb(s, v, carry):
        m, den, acc = carry
        m_new = jnp.maximum(m, jnp.max(s, axis=0, keepdims=True).astype(F32))
        alpha = jnp.exp2(m - m_new)
        p = jnp.exp2(s - m_new.astype(BF16))
        half = p.shape[0] // 2
        den = alpha * den + jnp.sum((p[0:half] + p[half:]).astype(F32), axis=0, keepdims=True)
        acc = alpha * acc + jnp.dot(v, p, preferred_element_type=F32)
        return m_new, den, acc

    carry = (jnp.full((1, tq), -jnp.inf, F32), jnp.zeros((1, tq), F32), jnp.zeros((HEAD_DIM, tq), F32))
    s_cur = scores(k_ref[0, 0, 0:tk, :])
    for j in range(nchunks):
        if j + 1 < nchunks:
            s_next = scores(k_ref[0, 0, (j + 1) * tk:(j + 2) * tk, :])
        else:
            s_next = scores(kx_ref[0, 0])
        carry = absorb(s_cur, v_ref[0, j], carry)
        s_cur = s_next
    _, den, acc = absorb(s_cur, vx_ref[0, 0], carry)
    o_ref[0] = (acc / den).astype(BF16)


def _mla_attention(q, k, v, kx, vx, tq):
    bn, nh, _, l = q.shape
    nchunks, tk = v.shape[1], v.shape[3]
    nc = kx.shape[2]
    return pl.pallas_call(
        functools.partial(_mla_body, tk=tk, nchunks=nchunks),
        grid=(bn, nh, l // tq),
        in_specs=[pl.BlockSpec((1, 1, QPAD, tq), lambda b, h, i: (b, h, 0, i)),
                  pl.BlockSpec((1, 1, l, QPAD), lambda b, h, i: (b, h, 0, 0)),
                  pl.BlockSpec((1, nchunks, HEAD_DIM, tk), lambda b, h, i: (b, 0, h, 0)),
                  pl.BlockSpec((1, 1, nc, QPAD), lambda b, h, i: (b, h, 0, 0)),
                  pl.BlockSpec((1, 1, HEAD_DIM, nc), lambda b, h, i: (b, 0, h, 0))],
        out_specs=pl.BlockSpec((1, HEAD_DIM, tq), lambda b, h, i: (b, h, i)),
        out_shape=jax.ShapeDtypeStruct((bn, nh * HEAD_DIM, l), BF16),
        compiler_params=_cparams(("arbitrary", "arbitrary", "arbitrary")),
        name="mla_attention",
    )(q, k, v, kx, vx)


TOKEN_ROWS = 8


def _first_argmax_rows(v, row_id):
    vmax = jnp.max(v, axis=0, keepdims=True)
    idx = jnp.min(jnp.where(v == vmax, row_id, float(v.shape[0])), axis=0, keepdims=True)
    return vmax, idx


def _mixout_body(x_ref, oa_ref, ob_ref, oc_ref, wt_ref, g1_ref, sh2_ref, sc2_ref, gn2_ref, wr_ref, tri_ref,
                 *rest, tm, steps, has_base):
    if has_base:
        cnt_in_ref = rest[0]
    xo_ref, h2_ref, ei_ref, wf_ref, cnt_ref, base_sc = rest[-6:]
    i = pl.program_id(0)

    @pl.when(i == 0)
    def _():
        if has_base:
            base_sc[...] = cnt_in_ref[:, 0:1]
        else:
            base_sc[...] = jnp.zeros(base_sc.shape, F32)

    ot = jnp.concatenate([oa_ref[0], ob_ref[0], oc_ref[0]], axis=0)
    out_t = jnp.dot(wt_ref[...], ot, preferred_element_type=F32)
    xn = x_ref[0] + g1_ref[0] * out_t.T
    xo_ref[0] = xn
    h2 = xn * lax.rsqrt(jnp.mean(xn * xn, axis=-1, keepdims=True) + EPS) * gn2_ref[...]
    h2 = h2 * (1.0 + sc2_ref[0]) + sh2_ref[0]
    for s in range(TOKEN_ROWS):
        h2_ref[pl.ds(s, tm, stride=TOKEN_ROWS), :] = h2[:, s * LANE:(s + 1) * LANE]

    h_hi = h2.astype(BF16)
    h_lo = (h2 - h_hi.astype(F32)).astype(BF16)
    lg = (jnp.dot(h_hi, wr_ref[0], preferred_element_type=F32) + jnp.dot(h_lo, wr_ref[0], preferred_element_type=F32)
          + jnp.dot(h_hi, wr_ref[1], preferred_element_type=F32))
    lt = lg.T
    gl = lt[0:N_GROUPS]
    rid = lax.broadcasted_iota(jnp.int32, (N_GROUPS, tm), 0).astype(F32)
    gmax, g_idx = _first_argmax_rows(gl, rid)
    g_gate = 1.0 / jnp.sum(jnp.exp(gl - gmax), axis=0, keepdims=True)
    e_sel = jnp.zeros((EXPERTS_PER_GROUP, tm), F32)
    for g in range(N_GROUPS):
        lo = N_GROUPS + g * EXPERTS_PER_GROUP
        e_sel = jnp.where(g_idx == float(g), lt[lo:lo + EXPERTS_PER_GROUP], e_sel)
    v1, j1 = _first_argmax_rows(e_sel, rid)
    v2, j2 = _first_argmax_rows(jnp.where(rid == j1, -jnp.inf, e_sel), rid)
    t21 = jnp.exp(v2 - v1)
    w1 = g_gate / (1.0 + t21)
    w2 = g_gate * t21 / (1.0 + t21)
    e1 = g_idx * float(EXPERTS_PER_GROUP) + j1
    e2 = g_idx * float(EXPERTS_PER_GROUP) + j2

    eid = lax.broadcasted_iota(jnp.int32, (N_EXPERTS, tm), 0).astype(F32)
    oh1 = (eid == e1).astype(F32)
    oh2 = (eid == e2).astype(F32)
    tri = tri_ref[...]
    cum1 = jnp.dot(oh1.astype(BF16), tri, preferred_element_type=F32)
    cum2 = jnp.dot(oh2.astype(BF16), tri, preferred_element_type=F32)
    tot1 = jnp.sum(oh1, axis=1, keepdims=True)
    tot2 = jnp.sum(oh2, axis=1, keepdims=True)
    base = base_sc[...]
    r1 = jnp.sum(oh1 * (base + cum1), axis=0, keepdims=True)
    r2 = jnp.sum(oh2 * (base + tot1 + cum2), axis=0, keepdims=True)
    live = jnp.where(i < steps, 1.0, 0.0)
    base_new = base + live * (tot1 + tot2)
    base_sc[...] = base_new
    cnt_ref[...] = jnp.broadcast_to(base_new, cnt_ref.shape)

    zeros4 = jnp.zeros((4, tm), F32)
    ei_ref[...] = jnp.concatenate([e1, e2, r1, r2, zeros4], axis=0).astype(jnp.int32)
    wpad = jnp.concatenate([w1, w2, jnp.zeros((LANE - 2, tm), F32)], axis=0)
    wf_ref[...] = wpad.T


def _mixout(x, oa, ob, oc, g1, sh2, sc2, lw, tm, n_rows, row0=0, bufs=None, cnt_in=None):
    bn, l, _ = x.shape
    nt = l // tm
    steps = bn * nt
    extra = 0
    if bufs is None and n_rows > bn * l:
        assert n_rows - bn * l == tm, "spare rows must be exactly one tile"
        extra = 1
    blk0 = row0 // tm

    def bt(i):
        ii = jnp.minimum(i, steps - 1)
        return ii // nt, ii % nt

    const2 = lambda i: (0, 0)
    modspec = pl.BlockSpec((1, 1, D_MODEL), lambda i: (bt(i)[0], 0, 0))
    rowspec = pl.BlockSpec((1, tm, D_MODEL), lambda i: (bt(i)[0], bt(i)[1], 0))
    tri = jnp.asarray(np.triu(np.ones((tm, tm), np.float32), k=1), BF16)
    in_specs = [rowspec,
                pl.BlockSpec((1, A_WIDTH, tm), lambda i: (bt(i)[0], 0, bt(i)[1])),
                pl.BlockSpec((1, B_WIDTH, tm), lambda i: (bt(i)[0], 0, bt(i)[1])),
                pl.BlockSpec((1, C_WIDTH, tm), lambda i: (bt(i)[0], 0, bt(i)[1])),
                pl.BlockSpec((D_MODEL, D_MODEL), const2),
                modspec, modspec, modspec,
                pl.BlockSpec((1, D_MODEL), const2),
                pl.BlockSpec((2, D_MODEL, ROUTER_PAD), lambda i: (0, 0, 0)),
                pl.BlockSpec((tm, tm), const2)]
    args = [x, oa, ob, oc, lw["w_out_t"], g1, sh2, sc2, lw["norm_ffn"], lw["w_route"], tri]
    aliases = {}
    if bufs is not None:
        in_specs.append(pl.BlockSpec((N_EXPERTS, LANE), const2))
        args.append(cnt_in)
        aliases = {len(args): 1, len(args) + 1: 2, len(args) + 2: 3}
        in_specs += [pl.BlockSpec(memory_space=pl.ANY)] * 3
        args += list(bufs)
    return pl.pallas_call(
        functools.partial(_mixout_body, tm=tm, steps=steps, has_base=bufs is not None),
        grid=(steps + extra,),
        in_specs=in_specs,
        out_specs=(rowspec,
                   pl.BlockSpec((tm * TOKEN_ROWS, LANE), lambda i: (blk0 + i, 0)),
                   pl.BlockSpec((8, tm), lambda i: (0, blk0 + i)),
                   pl.BlockSpec((tm, LANE), lambda i: (blk0 + i, 0)),
                   pl.BlockSpec((N_EXPERTS, LANE), const2)),
        out_shape=(jax.ShapeDtypeStruct((bn, l, D_MODEL), F32),
                   jax.ShapeDtypeStruct((n_rows * TOKEN_ROWS, LANE), F32),
                   jax.ShapeDtypeStruct((8, n_rows), jnp.int32),
                   jax.ShapeDtypeStruct((n_rows, LANE), F32),
                   jax.ShapeDtypeStruct((N_EXPERTS, LANE), F32)),
        scratch_shapes=[pltpu.VMEM((N_EXPERTS, 1), F32)],
        input_output_aliases=aliases,
        compiler_params=_cparams(("arbitrary",)),
        name="mix_out",
    )(*args)


def _tile_rows(idx):
    return pl.ds(pl.multiple_of(idx * TOKEN_ROWS, TOKEN_ROWS), TOKEN_ROWS)


def _slot(ps_ref, ei_ref, k, r):
    return ps_ref[ei_ref[k, r]] + ei_ref[TOP_K + k, r]


def _dispatch_body(ps_ref, ei_ref, h_ref, xs_in_ref, xs_ref, sem, *, td):
    del xs_in_ref

    def one(r, carry):
        for k in range(TOP_K):
            pltpu.make_async_copy(h_ref.at[_tile_rows(r)], xs_ref.at[_tile_rows(_slot(ps_ref, ei_ref, k, r))],
                                  sem).start(priority=k)
        return carry
    lax.fori_loop(0, td, one, 0, unroll=8)
    for k in range(TOP_K):
        pltpu.make_async_copy(h_ref, xs_ref.at[pl.ds(0, td * TOKEN_ROWS)], sem).wait()


def _dispatch(pstarts, ei, h2t, xs0, td):
    n_tok = ei.shape[1]
    grid_spec = pltpu.PrefetchScalarGridSpec(
        num_scalar_prefetch=1,
        grid=(n_tok // td,),
        in_specs=[pl.BlockSpec((8, td), lambda i, ps: (0, i), memory_space=pltpu.SMEM),
                  pl.BlockSpec((td * TOKEN_ROWS, LANE), lambda i, ps: (i, 0)),
                  pl.BlockSpec(memory_space=pl.ANY)],
        out_specs=pl.BlockSpec(memory_space=pl.ANY),
        scratch_shapes=[pltpu.SemaphoreType.DMA(())],
    )
    return pl.pallas_call(
        functools.partial(_dispatch_body, td=td),
        grid_spec=grid_spec,
        out_shape=jax.ShapeDtypeStruct(xs0.shape, F32),
        input_output_aliases={3: 0},
        compiler_params=_cparams(("arbitrary",)),
        name="moe_dispatch",
    )(pstarts, ei, h2t, xs0)


def _experts_body(be_ref, bi_ref, used_ref, x_ref, w1_ref, w3_ref, w2_ref, y_ref, w1b, w3b, w2b, *, tb):
    i = pl.program_id(0)
    used = used_ref[i] > 0

    @pl.when(used)
    def _():
        prev_e = be_ref[jnp.maximum(i - 1, 0)]

        @pl.when((i == 0) | (prev_e != be_ref[i]))
        def _():
            w1b[...] = w1_ref[0, 0].astype(BF16)
            w3b[...] = w3_ref[0, 0].astype(BF16)
            w2b[...] = w2_ref[0, 0].astype(BF16)

        xb = jnp.concatenate([x_ref[pl.ds(s, tb, stride=TOKEN_ROWS), :] for s in range(TOKEN_ROWS)],
                             axis=1).astype(BF16)
        a = jnp.dot(xb, w1b[...], preferred_element_type=F32)
        b = jnp.dot(xb, w3b[...], preferred_element_type=F32)
        hm = (a * (1.0 / (1.0 + jnp.exp(-a))) * b).astype(BF16)
        y = jnp.dot(hm, w2b[...], preferred_element_type=F32)
        for s in range(TOKEN_ROWS):
            y_ref[pl.ds(s, tb, stride=TOKEN_ROWS), :] = y[:, s * LANE:(s + 1) * LANE]

    @pl.when(jnp.logical_not(used))
    def _():
        y_ref[...] = jnp.zeros(y_ref.shape, F32)


def _experts(xs, block_e, block_i, used, w1, w3, w2, layer, tb):
    nb = block_e.shape[0]
    wspec_up = pl.BlockSpec((1, 1, D_MODEL, D_EXPERT), lambda i, be, bi, us: (layer, be[i], 0, 0))
    grid_spec = pltpu.PrefetchScalarGridSpec(
        num_scalar_prefetch=3,
        grid=(nb,),
        in_specs=[pl.BlockSpec((tb * TOKEN_ROWS, LANE), lambda i, be, bi, us: (bi[i], 0)),
                  wspec_up, wspec_up,
                  pl.BlockSpec((1, 1, D_EXPERT, D_MODEL), lambda i, be, bi, us: (layer, be[i], 0, 0))],
        out_specs=pl.BlockSpec((tb * TOKEN_ROWS, LANE), lambda i, be, bi, us: (i, 0)),
        scratch_shapes=[pltpu.VMEM((D_MODEL, D_EXPERT), BF16), pltpu.VMEM((D_MODEL, D_EXPERT), BF16),
                        pltpu.VMEM((D_EXPERT, D_MODEL), BF16)],
    )
    return pl.pallas_call(
        functools.partial(_experts_body, tb=tb),
        grid_spec=grid_spec,
        out_shape=jax.ShapeDtypeStruct(xs.shape, F32),
        compiler_params=_cparams(("arbitrary",)),
        name="moe_experts",
    )(block_e, block_i, used, xs, w1, w3, w2)


def _combine_body(ps_ref, ei_ref, wf_ref, x_ref, g2_ref, ys_ref, o_ref, buf0, buf1, sem, *, tm):
    bufs = (buf0, buf1)

    def one(r, carry):
        for k in range(TOP_K):
            pltpu.make_async_copy(ys_ref.at[_tile_rows(_slot(ps_ref, ei_ref, k, r))], bufs[k].at[_tile_rows(r)],
                                  sem).start(priority=k)
        return carry
    lax.fori_loop(0, tm, one, 0, unroll=8)
    for k in range(TOP_K):
        pltpu.make_async_copy(ys_ref.at[pl.ds(0, tm * TOKEN_ROWS)], bufs[k], sem).wait()
    wf = wf_ref[...]
    y = None
    for k in range(TOP_K):
        yk = jnp.concatenate([bufs[k][pl.ds(s, tm, stride=TOKEN_ROWS), :] for s in range(TOKEN_ROWS)], axis=1)
        yk = wf[:, k:k + 1] * yk
        y = yk if y is None else y + yk
    o_ref[0] = x_ref[0] + g2_ref[0] * y


def _combine(x, pstarts, ei, wf, ys, g2, row0, tm):
    bn, l, _ = x.shape
    nt = l // tm
    t0 = row0 // tm
    rowspec = pl.BlockSpec((1, tm, D_MODEL), lambda b, t, ps: (b, t, 0))
    grid_spec = pltpu.PrefetchScalarGridSpec(
        num_scalar_prefetch=1,
        grid=(bn, nt),
        in_specs=[pl.BlockSpec((8, tm), lambda b, t, ps: (0, t0 + b * nt + t), memory_space=pltpu.SMEM),
                  pl.BlockSpec((tm, LANE), lambda b, t, ps: (t0 + b * nt + t, 0)),
                  rowspec,
                  pl.BlockSpec((1, 1, D_MODEL), lambda b, t, ps: (b, 0, 0)),
                  pl.BlockSpec(memory_space=pl.ANY)],
        out_specs=rowspec,
        scratch_shapes=[pltpu.VMEM((tm * TOKEN_ROWS, LANE), F32), pltpu.VMEM((tm * TOKEN_ROWS, LANE), F32),
                        pltpu.SemaphoreType.DMA(())],
    )
    return pl.pallas_call(
        functools.partial(_combine_body, tm=tm),
        grid_spec=grid_spec,
        out_shape=jax.ShapeDtypeStruct(x.shape, F32),
        compiler_params=_cparams(("arbitrary", "arbitrary")),
        name="moe_combine",
    )(pstarts, ei, wf, x, g2, ys)


def _expert_blocks(counts, nb, tb):
    pcounts = (counts + tb - 1) // tb * tb
    pends = jnp.cumsum(pcounts)
    pstarts = (pends - pcounts).astype(jnp.int32)
    n_used = pends[-1] // tb
    blk = jnp.arange(nb, dtype=jnp.int32)
    used = blk < n_used
    last = jnp.maximum(n_used - 1, 0).astype(jnp.int32)
    block_i = jnp.where(used, blk, last)
    block_e = jnp.sum((pends[None, :] <= (block_i * tb)[:, None]).astype(jnp.int32), axis=1)
    block_e = jnp.minimum(block_e, N_EXPERTS - 1)
    return pstarts, block_e, block_i, used.astype(jnp.int32)


_ROPE_PERM = np.concatenate([np.arange(8, 16), np.arange(0, 8), np.arange(24, 32), np.arange(16, 24)])
_ROPE_SIGN = np.concatenate([-np.ones(8), np.ones(8), -np.ones(8), np.ones(8)]).astype(np.float32)


def _rope_tables_t(l):
    half = C_ROPE // 2
    inv = ROPE_BASE ** (-jnp.arange(0, half, 2, dtype=F32) / half)
    pos = jnp.arange(l)
    ang_r = (pos // GRID_W).astype(F32)[None, :] * inv[:, None]
    ang_c = (pos % GRID_W).astype(F32)[None, :] * inv[:, None]
    cos_t = jnp.concatenate([jnp.cos(ang_r)] * 2 + [jnp.cos(ang_c)] * 2, axis=0)
    sin_t = jnp.concatenate([jnp.sin(ang_r)] * 2 + [jnp.sin(ang_c)] * 2, axis=0)
    return cos_t, sin_t


def _col(v):
    return v.astype(F32)[:, None]


def _layer_weights(i, p):
    w_in = p["w_in"][i]
    kr0 = IN_A + IN_B + C_Q_LORA + C_KV_LORA
    kr_cols = w_in[:, kr0:kr0 + C_ROPE]
    w_aug = jnp.concatenate([w_in, kr_cols[:, _ROPE_PERM] * _ROPE_SIGN], axis=1)

    wq = p["c_w_q_up"][i].reshape(C_Q_LORA, C_HEADS, C_QK)
    wq_n = wq[:, :, :C_NOPE].reshape(C_Q_LORA, C_HEADS * C_NOPE)
    wq_r = wq[:, :, C_NOPE:]
    wq_rp = (wq_r[:, :, _ROPE_PERM] * _ROPE_SIGN).reshape(C_Q_LORA, C_HEADS * C_ROPE)
    wq_all = jnp.concatenate([wq_n, wq_r.reshape(C_Q_LORA, C_HEADS * C_ROPE), wq_rp], axis=1)

    wkv = p["c_w_kv_up"][i].reshape(C_KV_LORA, C_HEADS, C_NOPE + C_VDIM)
    wkv_all = jnp.concatenate([wkv[:, :, :C_NOPE].reshape(C_KV_LORA, C_HEADS * C_NOPE),
                               wkv[:, :, C_NOPE:].reshape(C_KV_LORA, C_HEADS * C_VDIM)], axis=1)

    cqn = p["c_q_norm"][i].astype(F32) * (C_QK ** -0.5 * LOG2E)
    ckn = p["c_k_norm"][i].astype(F32)
    w_route = jnp.concatenate([p["moe_w_group"][i], p["moe_w_router"][i],
                               jnp.zeros((D_MODEL, ROUTER_PAD - N_GROUPS - N_EXPERTS), F32)], axis=1)
    return {
        "norm_mix": p["norm_mix"][i][None, :],
        "w_in_t": w_aug.T.astype(BF16),
        "avn": _col(p["a_v_norm"][i]),
        "wst": jnp.transpose(p["a_w_s"][i], (0, 2, 1)).astype(BF16),
        "bs": p["a_b_s"][i][:, None, :],
        "gq": _col(p["b_q_norm"][i] * (HEAD_DIM ** -0.5 * LOG2E)),
        "gk": _col(p["b_k_norm"][i]),
        "cqa": _col(p["c_q_a_norm"][i]),
        "wq_t": wq_all.T.astype(BF16),
        "ckva": _col(p["c_kv_a_norm"][i]),
        "wkv_t": wkv_all.T.astype(BF16),
        "gcqn": _col(cqn[:C_NOPE]),
        "gcqr": _col(cqn[C_NOPE:]),
        "gcqrp": _col(cqn[C_NOPE:][_ROPE_PERM]),
        "gckn": _col(ckn[:C_NOPE]),
        "gckr": _col(ckn[C_NOPE:]),
        "gckrp": _col(ckn[C_NOPE:][_ROPE_PERM]),
        "w_out_t": p["w_out"][i].T.astype(BF16),
        "norm_ffn": p["norm_ffn"][i][None, :],
        "w_route": jnp.stack([w_route.astype(BF16), (w_route - w_route.astype(BF16).astype(F32)).astype(BF16)]),
    }


def _pick_tile(l, pref):
    t = min(l, pref)
    while l % t:
        t //= 2
    return t


def kernel(x, c, ctx, c_ctx, w_ada, b_ada, norm_mix, w_in, a_v_norm, a_w_s, a_b_s, b_q_norm, b_k_norm, b_rpb,
           c_q_a_norm, c_w_q_up, c_kv_a_norm, c_w_kv_up, c_q_norm, c_k_norm, w_out, norm_ffn,
           moe_w_group, moe_w_router, moe_w1, moe_w3, moe_w2):
    p = dict(norm_mix=norm_mix, w_in=w_in, a_v_norm=a_v_norm, a_w_s=a_w_s, a_b_s=a_b_s, b_q_norm=b_q_norm,
             b_k_norm=b_k_norm, c_q_a_norm=c_q_a_norm, c_w_q_up=c_w_q_up, c_kv_a_norm=c_kv_a_norm,
             c_w_kv_up=c_w_kv_up, c_q_norm=c_q_norm, c_k_norm=c_k_norm, w_out=w_out, norm_ffn=norm_ffn,
             moe_w_group=moe_w_group, moe_w_router=moe_w_router)
    bn, l, d = x.shape
    nc = ctx.shape[1]
    depth = w_ada.shape[0]
    rows = l // GRID_W
    tm = _pick_tile(l, 512)
    tq = _pick_tile(l, MLA_QUERY_TILE)

    cos_t, sin_t = _rope_tables_t(l)
    cos_x = jnp.ones((C_ROPE, nc), F32)
    sin_x = jnp.zeros((C_ROPE, nc), F32)
    cc8 = jnp.concatenate([c, c_ctx[None, :], jnp.zeros((8 - bn - 1, d), F32)], axis=0)

    nb = -(-TOP_K * bn * (l + nc) // MOE_BLOCK) + N_EXPERTS
    xs = jnp.zeros((nb * MOE_BLOCK * TOKEN_ROWS, LANE), F32)

    xc = ctx
    for i in range(depth):
        need_ctx = i < depth - 1
        lw = _layer_weights(i, p)
        mod = _ada(cc8, w_ada[i], b_ada[i][None, :])
        mods = [mod[:bn, j * d:(j + 1) * d][:, None, :] for j in range(6)]
        modx = [jnp.broadcast_to(mod[bn, j * d:(j + 1) * d][None, None, :], (bn, 1, d)) for j in range(6)]
        sh1, s1, g1, sh2, s2, g2 = mods
        sh1x, s1x, g1x, sh2x, s2x, g2x = modx

        oa, qb, kb, vb, qc, kc, vc = _mixin(x, sh1, s1, lw, cos_t, sin_t, tm)
        oax, qbx, kbx, vbx, qcx, kcx, vcx = _mixin(xc, sh1x, s1x, lw, cos_x, sin_x, nc)

        table = _na_bias_table(b_rpb[i], rows)
        ob = _na_attention(qb, kb, vb, kbx, vbx, table)
        oc = _mla_attention(qc, kc, vc, kcx, vcx, tq)
        n_lat = bn * l
        n_tok = n_lat + (bn * nc if need_ctx else 0)
        x, h2t, ei, wf, cnt = _mixout(x, oa, ob, oc, g1, sh2, s2, lw, tm, n_tok)
        if need_ctx:
            obx = _flash(qbx, kbx[:, None], vbx,
                         pl.BlockSpec((1, 1, nc, LANE), lambda b, h, ii, j: (b, 0, j, h // 2)),
                         tq=nc, tk=nc, name="ctx_dense_attention")
            ocx = _flash(qcx, kcx, vcx.reshape(bn, C_WIDTH, nc),
                         pl.BlockSpec((1, 1, nc, QPAD), lambda b, h, ii, j: (b, h, j, 0)),
                         tq=nc, tk=nc, name="ctx_mla_attention")
            xc, h2t, ei, wf, cnt = _mixout(xc, oax, obx, ocx, g1x, sh2x, s2x, lw, nc, n_tok, row0=n_lat,
                                           bufs=(h2t, ei, wf), cnt_in=cnt)

        counts = cnt[:, 0].astype(jnp.int32)
        pstarts, block_e, block_i, used = _expert_blocks(counts, nb, MOE_BLOCK)
        xs = _dispatch(pstarts, ei, h2t, xs, tm)
        ys = _experts(xs, block_e, block_i, used, moe_w1, moe_w3, moe_w2, i, MOE_BLOCK)
        x = _combine(x, pstarts, ei, wf, ys, g2, 0, tm)
        if need_ctx:
            xc = _combine(xc, pstarts, ei, wf, ys, g2x, n_lat, nc)
    return x
```

```python
import functools

import numpy as np
import jax
import jax.numpy as jnp
from jax import lax
from jax.experimental import pallas as pl
from jax.experimental.pallas import tpu as pltpu

F32 = jnp.float32
BF16 = jnp.bfloat16

D_MODEL = 1024
GRID_W = 64
HEAD_DIM = 64
EPS = 1e-6
NEG_INF = -1e30

A_HEADS = 4
A_WIDTH = A_HEADS * HEAD_DIM
CHUNK = 128
B_HEADS = 6
B_WIDTH = B_HEADS * HEAD_DIM
NA_ROWS = 8
NA_COLS = 16
C_HEADS = 6
C_NOPE = 64
C_ROPE = 32
C_QK = C_NOPE + C_ROPE
C_VDIM = 64
C_Q_LORA = 384
C_KV_LORA = 256
C_WIDTH = C_HEADS * C_VDIM
ROPE_BASE = 10000.0
IN_A = 2 * A_WIDTH
IN_B = 3 * B_WIDTH
IN_C = C_Q_LORA + C_KV_LORA + C_ROPE
IN_AUG = IN_A + IN_B + IN_C + C_ROPE

N_GROUPS = 8
EXPERTS_PER_GROUP = 8
N_EXPERTS = N_GROUPS * EXPERTS_PER_GROUP
TOP_K = 2
D_EXPERT = D_MODEL // 2
ROUTER_PAD = 128

V7X_VMEM_LIMIT_BYTES = 56 * 1024 * 1024
LANE = 128
QPAD = 128

MLA_KEY_CHUNK = 256
MLA_QUERY_TILE = 1024
NA_QROWS = 8
NA_KBLK_ROWS = 4
MOE_BLOCK = 256

HIGHEST = lax.Precision.HIGHEST
LOG2E = 1.4426950408889634


def _cparams(sem):
    return pltpu.CompilerParams(dimension_semantics=sem, vmem_limit_bytes=V7X_VMEM_LIMIT_BYTES)


def _ada_body(c_ref, w_ref, b_ref, o_ref):
    cc = c_ref[...]
    s = cc * (1.0 / (1.0 + jnp.exp(-cc)))
    o_ref[...] = jnp.dot(s, w_ref[...], preferred_element_type=F32, precision=HIGHEST) + b_ref[...]


def _ada(cc8, w, b):
    n_out = w.shape[1]
    tn = 1024
    return pl.pallas_call(
        _ada_body,
        grid=(n_out // tn,),
        in_specs=[pl.BlockSpec((8, D_MODEL), lambda j: (0, 0)),
                  pl.BlockSpec((D_MODEL, tn), lambda j: (0, j)),
                  pl.BlockSpec((1, tn), lambda j: (0, j))],
        out_specs=pl.BlockSpec((8, tn), lambda j: (0, j)),
        out_shape=jax.ShapeDtypeStruct((8, n_out), F32),
        compiler_params=_cparams(("arbitrary",)),
        name="ada_mod",
    )(cc8, w, b)


def _rms_rows(v):
    return lax.rsqrt(jnp.mean(v * v, axis=0, keepdims=True) + EPS)


def _gelu_tanh(x):
    return 0.5 * x * (1.0 + jnp.tanh(0.7978845608028654 * (x + 0.044715 * (x * x * x))))


def _mixin_body(x_ref, sh_ref, sc_ref, gn_ref, wt_ref, avn_ref, wst_ref, bs_ref, gq_ref, gk_ref,
                cqa_ref, wq_ref, ckva_ref, wkv_ref, gcqn_ref, gcqr_ref, gcqrp_ref, gckn_ref, gckr_ref,
                gckrp_ref, cos_ref, sin_ref,
                oa_ref, qb_ref, kb_ref, vb_ref, qc_ref, kc_ref, vc_ref, *, tm):
    x = x_ref[0]
    h = x * lax.rsqrt(jnp.mean(x * x, axis=-1, keepdims=True) + EPS) * gn_ref[...]
    h = h * (1.0 + sc_ref[0]) + sh_ref[0]
    zt = lax.dot_general(wt_ref[...], h.astype(BF16), (((1,), (1,)), ((), ())),
                         preferred_element_type=F32)

    ga = _gelu_tanh(zt[0:IN_A])
    u = ga[0:A_WIDTH]
    v = ga[A_WIDTH:IN_A]
    vn = (v * _rms_rows(v) * avn_ref[...]).astype(BF16)
    for hd in range(A_HEADS):
        r0, r1 = hd * HEAD_DIM, (hd + 1) * HEAD_DIM
        for c in range(tm // CHUNK):
            c0, c1 = c * CHUNK, (c + 1) * CHUNK
            vm = jnp.dot(vn[r0:r1, c0:c1], wst_ref[hd], preferred_element_type=F32) + bs_ref[hd]
            oa_ref[0, r0:r1, c0:c1] = (u[r0:r1, c0:c1] * vm).astype(BF16)

    zb = zt[IN_A:IN_A + IN_B]
    zeros64 = jnp.zeros((HEAD_DIM, tm), F32)
    kn = []
    for hd in range(B_HEADS):
        q = zb[hd * HEAD_DIM:(hd + 1) * HEAD_DIM]
        k = zb[B_WIDTH + hd * HEAD_DIM:B_WIDTH + (hd + 1) * HEAD_DIM]
        qn = q * _rms_rows(q) * gq_ref[...]
        kn.append(k * _rms_rows(k) * gk_ref[...])
        pair = [qn, zeros64] if hd % 2 == 0 else [zeros64, qn]
        qb_ref[0, hd] = jnp.concatenate(pair, axis=0).astype(BF16)
    for p in range(B_HEADS // 2):
        kt = jnp.concatenate([kn[2 * p], kn[2 * p + 1]], axis=0)
        kb_ref[0, :, p * LANE:(p + 1) * LANE] = kt.T.astype(BF16)
    vb_ref[0] = zb[2 * B_WIDTH:3 * B_WIDTH].astype(BF16)

    zc = zt[IN_A + IN_B:IN_AUG]
    ql = zc[0:C_Q_LORA]
    kvl = zc[C_Q_LORA:C_Q_LORA + C_KV_LORA]
    kr = zc[C_Q_LORA + C_KV_LORA:C_Q_LORA + C_KV_LORA + C_ROPE]
    krp = zc[C_Q_LORA + C_KV_LORA + C_ROPE:C_Q_LORA + C_KV_LORA + 2 * C_ROPE]
    qln = (ql * _rms_rows(ql) * cqa_ref[...]).astype(BF16)
    kvln = (kvl * _rms_rows(kvl) * ckva_ref[...]).astype(BF16)
    qt = jnp.dot(wq_ref[...], qln, preferred_element_type=F32)
    kvt = jnp.dot(wkv_ref[...], kvln, preferred_element_type=F32)
    cos = cos_ref[...]
    sin = sin_ref[...]
    krn = _rms_rows(kr) * (gckr_ref[...] * kr * cos + gckrp_ref[...] * krp * sin)
    zeros32 = jnp.zeros((QPAD - C_QK, tm), F32)
    nq = C_HEADS * C_NOPE
    nr = C_HEADS * C_ROPE
    for hd in range(C_HEADS):
        qn = qt[hd * C_NOPE:(hd + 1) * C_NOPE]
        qn = qn * _rms_rows(qn) * gcqn_ref[...]
        qr = qt[nq + hd * C_ROPE:nq + (hd + 1) * C_ROPE]
        qrp = qt[nq + nr + hd * C_ROPE:nq + nr + (hd + 1) * C_ROPE]
        qrn = _rms_rows(qr) * (gcqr_ref[...] * qr * cos + gcqrp_ref[...] * qrp * sin)
        qc_ref[0, hd] = jnp.concatenate([qn, qrn, zeros32], axis=0).astype(BF16)
        kn_c = kvt[hd * C_NOPE:(hd + 1) * C_NOPE]
        kn_c = kn_c * _rms_rows(kn_c) * gckn_ref[...]
        kc_ref[0, hd] = jnp.concatenate([kn_c, krn, zeros32], axis=0).T.astype(BF16)
    vt = vc_ref.shape[3]
    v_rows = kvt[nq:nq + C_WIDTH].astype(BF16)
    for c in range(tm // vt):
        vc_ref[0, c] = v_rows[:, c * vt:(c + 1) * vt]


def _mixin(x, sh, sc, lw, cos_t, sin_t, tm):
    bn, l, _ = x.shape
    vt = min(MLA_KEY_CHUNK, tm)
    const2 = lambda b, t: (0, 0)
    const3 = lambda b, t: (0, 0, 0)
    in_specs = [
        pl.BlockSpec((1, tm, D_MODEL), lambda b, t: (b, t, 0)),
        pl.BlockSpec((1, 1, D_MODEL), lambda b, t: (b, 0, 0)),
        pl.BlockSpec((1, 1, D_MODEL), lambda b, t: (b, 0, 0)),
        pl.BlockSpec((1, D_MODEL), const2),
        pl.BlockSpec((IN_AUG, D_MODEL), const2),
        pl.BlockSpec((A_WIDTH, 1), const2),
        pl.BlockSpec((A_HEADS, CHUNK, CHUNK), const3),
        pl.BlockSpec((A_HEADS, 1, CHUNK), const3),
        pl.BlockSpec((HEAD_DIM, 1), const2),
        pl.BlockSpec((HEAD_DIM, 1), const2),
        pl.BlockSpec((C_Q_LORA, 1), const2),
        pl.BlockSpec((C_HEADS * (C_NOPE + 2 * C_ROPE), C_Q_LORA), const2),
        pl.BlockSpec((C_KV_LORA, 1), const2),
        pl.BlockSpec((C_HEADS * (C_NOPE + C_VDIM), C_KV_LORA), const2),
        pl.BlockSpec((C_NOPE, 1), const2),
        pl.BlockSpec((C_ROPE, 1), const2),
        pl.BlockSpec((C_ROPE, 1), const2),
        pl.BlockSpec((C_NOPE, 1), const2),
        pl.BlockSpec((C_ROPE, 1), const2),
        pl.BlockSpec((C_ROPE, 1), const2),
        pl.BlockSpec((C_ROPE, tm), lambda b, t: (0, t)),
        pl.BlockSpec((C_ROPE, tm), lambda b, t: (0, t)),
    ]
    out_shape = (
        jax.ShapeDtypeStruct((bn, A_WIDTH, l), BF16),
        jax.ShapeDtypeStruct((bn, B_HEADS, QPAD, l), BF16),
        jax.ShapeDtypeStruct((bn, l, B_WIDTH), BF16),
        jax.ShapeDtypeStruct((bn, B_WIDTH, l), BF16),
        jax.ShapeDtypeStruct((bn, C_HEADS, QPAD, l), BF16),
        jax.ShapeDtypeStruct((bn, C_HEADS, l, QPAD), BF16),
        jax.ShapeDtypeStruct((bn, l // vt, C_WIDTH, vt), BF16),
    )
    out_specs = (
        pl.BlockSpec((1, A_WIDTH, tm), lambda b, t: (b, 0, t)),
        pl.BlockSpec((1, B_HEADS, QPAD, tm), lambda b, t: (b, 0, 0, t)),
        pl.BlockSpec((1, tm, B_WIDTH), lambda b, t: (b, t, 0)),
        pl.BlockSpec((1, B_WIDTH, tm), lambda b, t: (b, 0, t)),
        pl.BlockSpec((1, C_HEADS, QPAD, tm), lambda b, t: (b, 0, 0, t)),
        pl.BlockSpec((1, C_HEADS, tm, QPAD), lambda b, t: (b, 0, t, 0)),
        pl.BlockSpec((1, tm // vt, C_WIDTH, vt), lambda b, t: (b, t, 0, 0)),
    )
    return pl.pallas_call(
        functools.partial(_mixin_body, tm=tm),
        grid=(bn, l // tm),
        in_specs=in_specs,
        out_specs=out_specs,
        out_shape=out_shape,
        compiler_params=_cparams(("arbitrary", "arbitrary")),
        name="mix_in",
    )(x, sh, sc, lw["norm_mix"], lw["w_in_t"], lw["avn"], lw["wst"], lw["bs"], lw["gq"], lw["gk"],
      lw["cqa"], lw["wq_t"], lw["ckva"], lw["wkv_t"], lw["gcqn"], lw["gcqr"], lw["gcqrp"], lw["gckn"],
      lw["gckr"], lw["gckrp"], cos_t, sin_t)


DEN_ROWS = 16


def _na_body(q_ref, k0_ref, k1_ref, k2_ref, k3_ref, v0_ref, v1_ref, v2_ref, v3_ref, kx_ref, vx_ref, t_ref,
             o_ref):
    k_refs = (k0_ref, k1_ref, k2_ref, k3_ref)
    v_refs = (v0_ref, v1_ref, v2_ref, v3_ref)
    nk = k0_ref.shape[1]

    def scores(hd):
        q = q_ref[0, hd]
        lanes = slice((hd // 2) * LANE, (hd // 2 + 1) * LANE)
        s = jnp.concatenate([jnp.dot(kr[0, :, lanes], q, preferred_element_type=F32) for kr in k_refs], axis=0)
        s = (s + t_ref[hd, 0]).astype(BF16)
        sx = jnp.dot(kx_ref[0, :, lanes], q, preferred_element_type=F32).astype(BF16)
        return s, sx

    def finish(hd, s, sx):
        rows = slice(hd * HEAD_DIM, (hd + 1) * HEAD_DIM)
        m = jnp.maximum(jnp.max(s, axis=0, keepdims=True), jnp.max(sx, axis=0, keepdims=True))
        p = jnp.exp2(s - m)
        px = jnp.exp2(sx - m)
        o = jnp.dot(jnp.concatenate([vx_ref[0, rows], jnp.ones((DEN_ROWS, px.shape[0]), BF16)], axis=0), px,
                    preferred_element_type=F32)
        ones = jnp.ones((DEN_ROWS, nk), BF16)
        for j, vr in enumerate(v_refs):
            o = o + jnp.dot(jnp.concatenate([vr[0, rows], ones], axis=0), p[j * nk:(j + 1) * nk],
                            preferred_element_type=F32)
        o_ref[0, rows] = (o[0:HEAD_DIM] / o[HEAD_DIM:HEAD_DIM + 1]).astype(BF16)

    pending = scores(0)
    for hd in range(B_HEADS):
        nxt = scores(hd + 1) if hd + 1 < B_HEADS else None
        finish(hd, *pending)
        pending = nxt


def _na_attention(qb, kb, vb, kxb, vxb, table):
    bn, _, _, l = qb.shape
    nq = NA_QROWS * GRID_W
    nk = NA_KBLK_ROWS * GRID_W
    nblk = l // nq
    nkb = l // nk
    nctx = kxb.shape[1]

    def kmap(j):
        return lambda b, i: (b, jnp.clip(2 * i - 1 + j, 0, nkb - 1), 0)

    def vmap_(j):
        return lambda b, i: (b, 0, jnp.clip(2 * i - 1 + j, 0, nkb - 1))

    def tmap(b, i):
        return (0, jnp.where(i == 0, 0, jnp.where(i == nblk - 1, 2, 1)), 0, 0)

    in_specs = ([pl.BlockSpec((1, B_HEADS, QPAD, nq), lambda b, i: (b, 0, 0, i))]
                + [pl.BlockSpec((1, nk, B_WIDTH), kmap(j)) for j in range(4)]
                + [pl.BlockSpec((1, B_WIDTH, nk), vmap_(j)) for j in range(4)]
                + [pl.BlockSpec((1, nctx, B_WIDTH), lambda b, i: (b, 0, 0)),
                   pl.BlockSpec((1, B_WIDTH, nctx), lambda b, i: (b, 0, 0)),
                   pl.BlockSpec((B_HEADS, 1, 4 * nk, nq), tmap)])
    return pl.pallas_call(
        _na_body,
        grid=(bn, nblk),
        in_specs=in_specs,
        out_specs=pl.BlockSpec((1, B_WIDTH, nq), lambda b, i: (b, 0, i)),
        out_shape=jax.ShapeDtypeStruct((bn, B_WIDTH, l), BF16),
        compiler_params=_cparams(("arbitrary", "arbitrary")),
        name="na_attention",
    )(qb, kb, kb, kb, kb, vb, vb, vb, vb, kxb, vxb, table)


def _na_bias_table(rpb, rows):
    nblk = rows // NA_QROWS
    qc = np.arange(GRID_W)
    kc = np.arange(GRID_W)
    c0 = np.clip(qc - NA_COLS // 2, 0, GRID_W - NA_COLS)
    valid_col = (kc[:, None] >= c0[None, :]) & (kc[:, None] < c0[None, :] + NA_COLS)
    dc = np.clip(kc[:, None] - qc[None, :], -(NA_COLS - 1), NA_COLS - 1) + NA_COLS - 1
    dc_onehot = (dc[None] == np.arange(2 * NA_COLS - 1)[:, None, None]).astype(np.float32)
    tabs = []
    for i in (0, 1, nblk - 1):
        kr = NA_KBLK_ROWS * (2 * i - 1) + np.arange(4 * NA_KBLK_ROWS)
        qr = NA_QROWS * i + np.arange(NA_QROWS)
        r0 = np.clip(qr - NA_ROWS // 2, 0, rows - NA_ROWS)
        valid_row = ((kr[:, None] >= r0[None, :]) & (kr[:, None] < r0[None, :] + NA_ROWS)
                     & (kr[:, None] >= 0) & (kr[:, None] < rows))
        dr = np.clip(kr[:, None] - qr[None, :] + NA_ROWS - 1, 0, 2 * NA_ROWS - 2)
        bias = jnp.einsum("hkqd,dcx->hkcqx", rpb[:, dr].astype(F32), dc_onehot, precision=HIGHEST)
        valid = valid_row[:, None, :, None] & valid_col[None, :, None, :]
        tabs.append(jnp.where(valid[None], bias.astype(F32) * LOG2E, NEG_INF).reshape(
            rpb.shape[0], 4 * NA_KBLK_ROWS * GRID_W, NA_QROWS * GRID_W))
    return jnp.stack(tabs, axis=1)


def _flash_body(*refs, has_extra):
    if has_extra:
        q_ref, k_ref, v_ref, kx_ref, vx_ref, o_ref, m_sc, l_sc, acc_sc = refs
    else:
        q_ref, k_ref, v_ref, o_ref, m_sc, l_sc, acc_sc = refs
    kv = pl.program_id(3)
    nkv = pl.num_programs(3)

    @pl.when(kv == 0)
    def _():
        m_sc[...] = jnp.full(m_sc.shape, -jnp.inf, F32)
        l_sc[...] = jnp.zeros(l_sc.shape, F32)
        acc_sc[...] = jnp.zeros(acc_sc.shape, F32)

    def step(k, v):
        s = jnp.dot(k, q_ref[0, 0], preferred_element_type=F32)
        m_old = m_sc[...]
        m_new = jnp.maximum(m_old, jnp.max(s, axis=0, keepdims=True))
        alpha = jnp.exp2(m_old - m_new)
        p = jnp.exp2(s - m_new)
        l_sc[...] = alpha * l_sc[...] + jnp.sum(p, axis=0, keepdims=True)
        acc_sc[...] = alpha * acc_sc[...] + jnp.dot(v, p.astype(BF16), preferred_element_type=F32)
        m_sc[...] = m_new

    step(k_ref[0, 0], v_ref[0])

    @pl.when(kv == nkv - 1)
    def _():
        if has_extra:
            step(kx_ref[0, 0], vx_ref[0])
        o_ref[0] = (acc_sc[...] / l_sc[...]).astype(BF16)


def _flash(q, k, v, k_spec, kx=None, vx=None, kx_spec=None, *, tq, tk, name):
    bn, nh, _, lq = q.shape
    lk = v.shape[2]
    has_extra = kx is not None
    in_specs = [pl.BlockSpec((1, 1, QPAD, tq), lambda b, h, i, j: (b, h, 0, i)),
                k_spec,
                pl.BlockSpec((1, HEAD_DIM, tk), lambda b, h, i, j: (b, h, j))]
    args = [q, k, v]
    if has_extra:
        nx = vx.shape[2]
        in_specs += [kx_spec, pl.BlockSpec((1, HEAD_DIM, nx), lambda b, h, i, j: (b, h, 0))]
        args += [kx, vx]
    return pl.pallas_call(
        functools.partial(_flash_body, has_extra=has_extra),
        grid=(bn, nh, lq // tq, lk // tk),
        in_specs=in_specs,
        out_specs=pl.BlockSpec((1, HEAD_DIM, tq), lambda b, h, i, j: (b, h, i)),
        out_shape=jax.ShapeDtypeStruct((bn, nh * HEAD_DIM, lq), BF16),
        scratch_shapes=[pltpu.VMEM((1, tq), F32), pltpu.VMEM((1, tq), F32), pltpu.VMEM((HEAD_DIM, tq), F32)],
        compiler_params=_cparams(("arbitrary", "arbitrary", "arbitrary", "arbitrary")),
        name=name,
    )(*args)


def _mla_body(q_ref, k_ref, v_ref, kx_ref, vx_ref, o_ref, *, tk, nchunks):
    q = q_ref[0, 0]
    tq = q.shape[1]

    def scores(k):
        return jnp.dot(k, q, preferred_element_type=F32).astype(BF16)

    def absorb(s, v, carry):
        m, den, acc = carry
        m_new = jnp.maximum(m, jnp.max(s, axis=0, keepdims=True).astype(F32))
        alpha = jnp.exp2(m - m_new)
        p = jnp.exp2(s - m_new.astype(BF16))
        half = p.shape[0] // 2
        den = alpha * den + jnp.sum((p[0:half] + p[half:]).astype(F32), axis=0, keepdims=True)
        acc = alpha * acc + jnp.dot(v, p, preferred_element_type=F32)
        return m_new, den, acc

    carry = (jnp.full((1, tq), -jnp.inf, F32), jnp.zeros((1, tq), F32), jnp.zeros((HEAD_DIM, tq), F32))
    s_cur = scores(k_ref[0, 0, 0:tk, :])
    for j in range(nchunks):
        if j + 1 < nchunks:
            s_next = scores(k_ref[0, 0, (j + 1) * tk:(j + 2) * tk, :])
        else:
            s_next = scores(kx_ref[0, 0])
        carry = absorb(s_cur, v_ref[0, j], carry)
        s_cur = s_next
    _, den, acc = absorb(s_cur, vx_ref[0, 0], carry)
    o_ref[0] = (acc / den).astype(BF16)


def _mla_attention(q, k, v, kx, vx, tq):
    bn, nh, _, l = q.shape
    nchunks, tk = v.shape[1], v.shape[3]
    nc = kx.shape[2]
    return pl.pallas_call(
        functools.partial(_mla_body, tk=tk, nchunks=nchunks),
        grid=(bn, nh, l // tq),
        in_specs=[pl.BlockSpec((1, 1, QPAD, tq), lambda b, h, i: (b, h, 0, i)),
                  pl.BlockSpec((1, 1, l, QPAD), lambda b, h, i: (b, h, 0, 0)),
                  pl.BlockSpec((1, nchunks, HEAD_DIM, tk), lambda b, h, i: (b, 0, h, 0)),
                  pl.BlockSpec((1, 1, nc, QPAD), lambda b, h, i: (b, h, 0, 0)),
                  pl.BlockSpec((1, 1, HEAD_DIM, nc), lambda b, h, i: (b, 0, h, 0))],
        out_specs=pl.BlockSpec((1, HEAD_DIM, tq), lambda b, h, i: (b, h, i)),
        out_shape=jax.ShapeDtypeStruct((bn, nh * HEAD_DIM, l), BF16),
        compiler_params=_cparams(("arbitrary", "arbitrary", "arbitrary")),
        name="mla_attention",
    )(q, k, v, kx, vx)


TOKEN_ROWS = 8


def _first_argmax_rows(v, row_id):
    vmax = jnp.max(v, axis=0, keepdims=True)
    idx = jnp.min(jnp.where(v == vmax, row_id, float(v.shape[0])), axis=0, keepdims=True)
    return vmax, idx


def _mixout_body(x_ref, oa_ref, ob_ref, oc_ref, wt_ref, g1_ref, sh2_ref, sc2_ref, gn2_ref, wr_ref, tri_ref,
                 *rest, tm, steps, has_base):
    if has_base:
        cnt_in_ref = rest[0]
    xo_ref, h2_ref, ei_ref, wf_ref, cnt_ref, base_sc = rest[-6:]
    i = pl.program_id(0)

    @pl.when(i == 0)
    def _():
        if has_base:
            base_sc[...] = cnt_in_ref[:, 0:1]
        else:
            base_sc[...] = jnp.zeros(base_sc.shape, F32)

    ot = jnp.concatenate([oa_ref[0], ob_ref[0], oc_ref[0]], axis=0)
    out_t = jnp.dot(wt_ref[...], ot, preferred_element_type=F32)
    xn = x_ref[0] + g1_ref[0] * out_t.T
    xo_ref[0] = xn
    h2 = xn * lax.rsqrt(jnp.mean(xn * xn, axis=-1, keepdims=True) + EPS) * gn2_ref[...]
    h2 = h2 * (1.0 + sc2_ref[0]) + sh2_ref[0]
    for s in range(TOKEN_ROWS):
        h2_ref[pl.ds(s, tm, stride=TOKEN_ROWS), :] = h2[:, s * LANE:(s + 1) * LANE]

    h_hi = h2.astype(BF16)
    h_lo = (h2 - h_hi.astype(F32)).astype(BF16)
    lg = (jnp.dot(h_hi, wr_ref[0], preferred_element_type=F32) + jnp.dot(h_lo, wr_ref[0], preferred_element_type=F32)
          + jnp.dot(h_hi, wr_ref[1], preferred_element_type=F32))
    lt = lg.T
    gl = lt[0:N_GROUPS]
    rid = lax.broadcasted_iota(jnp.int32, (N_GROUPS, tm), 0).astype(F32)
    gmax, g_idx = _first_argmax_rows(gl, rid)
    g_gate = 1.0 / jnp.sum(jnp.exp(gl - gmax), axis=0, keepdims=True)
    e_sel = jnp.zeros((EXPERTS_PER_GROUP, tm), F32)
    for g in range(N_GROUPS):
        lo = N_GROUPS + g * EXPERTS_PER_GROUP
        e_sel = jnp.where(g_idx == float(g), lt[lo:lo + EXPERTS_PER_GROUP], e_sel)
    v1, j1 = _first_argmax_rows(e_sel, rid)
    v2, j2 = _first_argmax_rows(jnp.where(rid == j1, -jnp.inf, e_sel), rid)
    t21 = jnp.exp(v2 - v1)
    w1 = g_gate / (1.0 + t21)
    w2 = g_gate * t21 / (1.0 + t21)
    e1 = g_idx * float(EXPERTS_PER_GROUP) + j1
    e2 = g_idx * float(EXPERTS_PER_GROUP) + j2

    eid = lax.broadcasted_iota(jnp.int32, (N_EXPERTS, tm), 0).astype(F32)
    oh1 = (eid == e1).astype(F32)
    oh2 = (eid == e2).astype(F32)
    tri = tri_ref[...]
    cum1 = jnp.dot(oh1.astype(BF16), tri, preferred_element_type=F32)
    cum2 = jnp.dot(oh2.astype(BF16), tri, preferred_element_type=F32)
    tot1 = jnp.sum(oh1, axis=1, keepdims=True)
    tot2 = jnp.sum(oh2, axis=1, keepdims=True)
    base = base_sc[...]
    r1 = jnp.sum(oh1 * (base + cum1), axis=0, keepdims=True)
    r2 = jnp.sum(oh2 * (base + tot1 + cum2), axis=0, keepdims=True)
    live = jnp.where(i < steps, 1.0, 0.0)
    base_new = base + live * (tot1 + tot2)
    base_sc[...] = base_new
    cnt_ref[...] = jnp.broadcast_to(base_new, cnt_ref.shape)

    zeros4 = jnp.zeros((4, tm), F32)
    ei_ref[...] = jnp.concatenate([e1, e2, r1, r2, zeros4], axis=0).astype(jnp.int32)
    wpad = jnp.concatenate([w1, w2, jnp.zeros((LANE - 2, tm), F32)], axis=0)
    wf_ref[...] = wpad.T


def _mixout(x, oa, ob, oc, g1, sh2, sc2, lw, tm, n_rows, row0=0, bufs=None, cnt_in=None):
    bn, l, _ = x.shape
    nt = l // tm
    steps = bn * nt
    extra = 0
    if bufs is None and n_rows > bn * l:
        assert n_rows - bn * l == tm, "spare rows must be exactly one tile"
        extra = 1
    blk0 = row0 // tm

    def bt(i):
        ii = jnp.minimum(i, steps - 1)
        return ii // nt, ii % nt

    const2 = lambda i: (0, 0)
    modspec = pl.BlockSpec((1, 1, D_MODEL), lambda i: (bt(i)[0], 0, 0))
    rowspec = pl.BlockSpec((1, tm, D_MODEL), lambda i: (bt(i)[0], bt(i)[1], 0))
    tri = jnp.asarray(np.triu(np.ones((tm, tm), np.float32), k=1), BF16)
    in_specs = [rowspec,
                pl.BlockSpec((1, A_WIDTH, tm), lambda i: (bt(i)[0], 0, bt(i)[1])),
                pl.BlockSpec((1, B_WIDTH, tm), lambda i: (bt(i)[0], 0, bt(i)[1])),
                pl.BlockSpec((1, C_WIDTH, tm), lambda i: (bt(i)[0], 0, bt(i)[1])),
                pl.BlockSpec((D_MODEL, D_MODEL), const2),
                modspec, modspec, modspec,
                pl.BlockSpec((1, D_MODEL), const2),
                pl.BlockSpec((2, D_MODEL, ROUTER_PAD), lambda i: (0, 0, 0)),
                pl.BlockSpec((tm, tm), const2)]
    args = [x, oa, ob, oc, lw["w_out_t"], g1, sh2, sc2, lw["norm_ffn"], lw["w_route"], tri]
    aliases = {}
    if bufs is not None:
        in_specs.append(pl.BlockSpec((N_EXPERTS, LANE), const2))
        args.append(cnt_in)
        aliases = {len(args): 1, len(args) + 1: 2, len(args) + 2: 3}
        in_specs += [pl.BlockSpec(memory_space=pl.ANY)] * 3
        args += list(bufs)
    return pl.pallas_call(
        functools.partial(_mixout_body, tm=tm, steps=steps, has_base=bufs is not None),
        grid=(steps + extra,),
        in_specs=in_specs,
        out_specs=(rowspec,
                   pl.BlockSpec((tm * TOKEN_ROWS, LANE), lambda i: (blk0 + i, 0)),
                   pl.BlockSpec((8, tm), lambda i: (0, blk0 + i)),
                   pl.BlockSpec((tm, LANE), lambda i: (blk0 + i, 0)),
                   pl.BlockSpec((N_EXPERTS, LANE), const2)),
        out_shape=(jax.ShapeDtypeStruct((bn, l, D_MODEL), F32),
                   jax.ShapeDtypeStruct((n_rows * TOKEN_ROWS, LANE), F32),
                   jax.ShapeDtypeStruct((8, n_rows), jnp.int32),
                   jax.ShapeDtypeStruct((n_rows, LANE), F32),
                   jax.ShapeDtypeStruct((N_EXPERTS, LANE), F32)),
        scratch_shapes=[pltpu.VMEM((N_EXPERTS, 1), F32)],
        input_output_aliases=aliases,
        compiler_params=_cparams(("arbitrary",)),
        name="mix_out",
    )(*args)


def _tile_rows(idx):
    return pl.ds(pl.multiple_of(idx * TOKEN_ROWS, TOKEN_ROWS), TOKEN_ROWS)


def _slot(ps_ref, ei_ref, k, r):
    return ps_ref[ei_ref[k, r]] + ei_ref[TOP_K + k, r]


def _dispatch_body(ps_ref, ei_ref, h_ref, xs_in_ref, xs_ref, sem, *, td):
    del xs_in_ref

    def one(r, carry):
        for k in range(TOP_K):
            pltpu.make_async_copy(h_ref.at[_tile_rows(r)], xs_ref.at[_tile_rows(_slot(ps_ref, ei_ref, k, r))],
                                  sem).start(priority=k)
        return carry
    lax.fori_loop(0, td, one, 0, unroll=8)
    for k in range(TOP_K):
        pltpu.make_async_copy(h_ref, xs_ref.at[pl.ds(0, td * TOKEN_ROWS)], sem).wait()


def _dispatch(pstarts, ei, h2t, xs0, td):
    n_tok = ei.shape[1]
    grid_spec = pltpu.PrefetchScalarGridSpec(
        num_scalar_prefetch=1,
        grid=(n_tok // td,),
        in_specs=[pl.BlockSpec((8, td), lambda i, ps: (0, i), memory_space=pltpu.SMEM),
                  pl.BlockSpec((td * TOKEN_ROWS, LANE), lambda i, ps: (i, 0)),
                  pl.BlockSpec(memory_space=pl.ANY)],
        out_specs=pl.BlockSpec(memory_space=pl.ANY),
        scratch_shapes=[pltpu.SemaphoreType.DMA(())],
    )
    return pl.pallas_call(
        functools.partial(_dispatch_body, td=td),
        grid_spec=grid_spec,
        out_shape=jax.ShapeDtypeStruct(xs0.shape, F32),
        input_output_aliases={3: 0},
        compiler_params=_cparams(("arbitrary",)),
        name="moe_dispatch",
    )(pstarts, ei, h2t, xs0)


def _experts_body(be_ref, bi_ref, used_ref, x_ref, w1_ref, w3_ref, w2_ref, y_ref, w1b, w3b, w2b, *, tb):
    i = pl.program_id(0)
    used = used_ref[i] > 0

    @pl.when(used)
    def _():
        prev_e = be_ref[jnp.maximum(i - 1, 0)]

        @pl.when((i == 0) | (prev_e != be_ref[i]))
        def _():
            w1b[...] = w1_ref[0, 0].astype(BF16)
            w3b[...] = w3_ref[0, 0].astype(BF16)
            w2b[...] = w2_ref[0, 0].astype(BF16)

        xb = jnp.concatenate([x_ref[pl.ds(s, tb, stride=TOKEN_ROWS), :] for s in range(TOKEN_ROWS)],
                             axis=1).astype(BF16)
        a = jnp.dot(xb, w1b[...], preferred_element_type=F32)
        b = jnp.dot(xb, w3b[...], preferred_element_type=F32)
        hm = (a * (1.0 / (1.0 + jnp.exp(-a))) * b).astype(BF16)
        y = jnp.dot(hm, w2b[...], preferred_element_type=F32)
        for s in range(TOKEN_ROWS):
            y_ref[pl.ds(s, tb, stride=TOKEN_ROWS), :] = y[:, s * LANE:(s + 1) * LANE]

    @pl.when(jnp.logical_not(used))
    def _():
        y_ref[...] = jnp.zeros(y_ref.shape, F32)


def _experts(xs, block_e, block_i, used, w1, w3, w2, layer, tb):
    nb = block_e.shape[0]
    wspec_up = pl.BlockSpec((1, 1, D_MODEL, D_EXPERT), lambda i, be, bi, us: (layer, be[i], 0, 0))
    grid_spec = pltpu.PrefetchScalarGridSpec(
        num_scalar_prefetch=3,
        grid=(nb,),
        in_specs=[pl.BlockSpec((tb * TOKEN_ROWS, LANE), lambda i, be, bi, us: (bi[i], 0)),
                  wspec_up, wspec_up,
                  pl.BlockSpec((1, 1, D_EXPERT, D_MODEL), lambda i, be, bi, us: (layer, be[i], 0, 0))],
        out_specs=pl.BlockSpec((tb * TOKEN_ROWS, LANE), lambda i, be, bi, us: (i, 0)),
        scratch_shapes=[pltpu.VMEM((D_MODEL, D_EXPERT), BF16), pltpu.VMEM((D_MODEL, D_EXPERT), BF16),
                        pltpu.VMEM((D_EXPERT, D_MODEL), BF16)],
    )
    return pl.pallas_call(
        functools.partial(_experts_body, tb=tb),
        grid_spec=grid_spec,
        out_shape=jax.ShapeDtypeStruct(xs.shape, F32),
        compiler_params=_cparams(("arbitrary",)),
        name="moe_experts",
    )(block_e, block_i, used, xs, w1, w3, w2)


def _combine_body(ps_ref, ei_ref, wf_ref, x_ref, g2_ref, ys_ref, o_ref, buf0, buf1, sem, *, tm):
    bufs = (buf0, buf1)

    def one(r, carry):
        for k in range(TOP_K):
            pltpu.make_async_copy(ys_ref.at[_tile_rows(_slot(ps_ref, ei_ref, k, r))], bufs[k].at[_tile_rows(r)],
                                  sem).start(priority=k)
        return carry
    lax.fori_loop(0, tm, one, 0, unroll=8)
    for k in range(TOP_K):
        pltpu.make_async_copy(ys_ref.at[pl.ds(0, tm * TOKEN_ROWS)], bufs[k], sem).wait()
    wf = wf_ref[...]
    y = None
    for k in range(TOP_K):
        yk = jnp.concatenate([bufs[k][pl.ds(s, tm, stride=TOKEN_ROWS), :] for s in range(TOKEN_ROWS)], axis=1)
        yk = wf[:, k:k + 1] * yk
        y = yk if y is None else y + yk
    o_ref[0] = x_ref[0] + g2_ref[0] * y


def _combine(x, pstarts, ei, wf, ys, g2, row0, tm):
    bn, l, _ = x.shape
    nt = l // tm
    t0 = row0 // tm
    rowspec = pl.BlockSpec((1, tm, D_MODEL), lambda b, t, ps: (b, t, 0))
    grid_spec = pltpu.PrefetchScalarGridSpec(
        num_scalar_prefetch=1,
        grid=(bn, nt),
        in_specs=[pl.BlockSpec((8, tm), lambda b, t, ps: (0, t0 + b * nt + t), memory_space=pltpu.SMEM),
                  pl.BlockSpec((tm, LANE), lambda b, t, ps: (t0 + b * nt + t, 0)),
                  rowspec,
                  pl.BlockSpec((1, 1, D_MODEL), lambda b, t, ps: (b, 0, 0)),
                  pl.BlockSpec(memory_space=pl.ANY)],
        out_specs=rowspec,
        scratch_shapes=[pltpu.VMEM((tm * TOKEN_ROWS, LANE), F32), pltpu.VMEM((tm * TOKEN_ROWS, LANE), F32),
                        pltpu.SemaphoreType.DMA(())],
    )
    return pl.pallas_call(
        functools.partial(_combine_body, tm=tm),
        grid_spec=grid_spec,
        out_shape=jax.ShapeDtypeStruct(x.shape, F32),
        compiler_params=_cparams(("arbitrary", "arbitrary")),
        name="moe_combine",
    )(pstarts, ei, wf, x, g2, ys)


def _expert_blocks(counts, nb, tb):
    pcounts = (counts + tb - 1) // tb * tb
    pends = jnp.cumsum(pcounts)
    pstarts = (pends - pcounts).astype(jnp.int32)
    n_used = pends[-1] // tb
    blk = jnp.arange(nb, dtype=jnp.int32)
    used = blk < n_used
    last = jnp.maximum(n_used - 1, 0).astype(jnp.int32)
    block_i = jnp.where(used, blk, last)
    block_e = jnp.sum((pends[None, :] <= (block_i * tb)[:, None]).astype(jnp.int32), axis=1)
    block_e = jnp.minimum(block_e, N_EXPERTS - 1)
    return pstarts, block_e, block_i, used.astype(jnp.int32)


_ROPE_PERM = np.concatenate([np.arange(8, 16), np.arange(0, 8), np.arange(24, 32), np.arange(16, 24)])
_ROPE_SIGN = np.concatenate([-np.ones(8), np.ones(8), -np.ones(8), np.ones(8)]).astype(np.float32)


def _rope_tables_t(l):
    half = C_ROPE // 2
    inv = ROPE_BASE ** (-jnp.arange(0, half, 2, dtype=F32) / half)
    pos = jnp.arange(l)
    ang_r = (pos // GRID_W).astype(F32)[None, :] * inv[:, None]
    ang_c = (pos % GRID_W).astype(F32)[None, :] * inv[:, None]
    cos_t = jnp.concatenate([jnp.cos(ang_r)] * 2 + [jnp.cos(ang_c)] * 2, axis=0)
    sin_t = jnp.concatenate([jnp.sin(ang_r)] * 2 + [jnp.sin(ang_c)] * 2, axis=0)
    return cos_t, sin_t


def _col(v):
    return v.astype(F32)[:, None]


def _layer_weights(i, p):
    w_in = p["w_in"][i]
    kr0 = IN_A + IN_B + C_Q_LORA + C_KV_LORA
    kr_cols = w_in[:, kr0:kr0 + C_ROPE]
    w_aug = jnp.concatenate([w_in, kr_cols[:, _ROPE_PERM] * _ROPE_SIGN], axis=1)

    wq = p["c_w_q_up"][i].reshape(C_Q_LORA, C_HEADS, C_QK)
    wq_n = wq[:, :, :C_NOPE].reshape(C_Q_LORA, C_HEADS * C_NOPE)
    wq_r = wq[:, :, C_NOPE:]
    wq_rp = (wq_r[:, :, _ROPE_PERM] * _ROPE_SIGN).reshape(C_Q_LORA, C_HEADS * C_ROPE)
    wq_all = jnp.concatenate([wq_n, wq_r.reshape(C_Q_LORA, C_HEADS * C_ROPE), wq_rp], axis=1)

    wkv = p["c_w_kv_up"][i].reshape(C_KV_LORA, C_HEADS, C_NOPE + C_VDIM)
    wkv_all = jnp.concatenate([wkv[:, :, :C_NOPE].reshape(C_KV_LORA, C_HEADS * C_NOPE),
                               wkv[:, :, C_NOPE:].reshape(C_KV_LORA, C_HEADS * C_VDIM)], axis=1)

    cqn = p["c_q_norm"][i].astype(F32) * (C_QK ** -0.5 * LOG2E)
    ckn = p["c_k_norm"][i].astype(F32)
    w_route = jnp.concatenate([p["moe_w_group"][i], p["moe_w_router"][i],
                               jnp.zeros((D_MODEL, ROUTER_PAD - N_GROUPS - N_EXPERTS), F32)], axis=1)
    return {
        "norm_mix": p["norm_mix"][i][None, :],
        "w_in_t": w_aug.T.astype(BF16),
        "avn": _col(p["a_v_norm"][i]),
        "wst": jnp.transpose(p["a_w_s"][i], (0, 2, 1)).astype(BF16),
        "bs": p["a_b_s"][i][:, None, :],
        "gq": _col(p["b_q_norm"][i] * (HEAD_DIM ** -0.5 * LOG2E)),
        "gk": _col(p["b_k_norm"][i]),
        "cqa": _col(p["c_q_a_norm"][i]),
        "wq_t": wq_all.T.astype(BF16),
        "ckva": _col(p["c_kv_a_norm"][i]),
        "wkv_t": wkv_all.T.astype(BF16),
        "gcqn": _col(cqn[:C_NOPE]),
        "gcqr": _col(cqn[C_NOPE:]),
        "gcqrp": _col(cqn[C_NOPE:][_ROPE_PERM]),
        "gckn": _col(ckn[:C_NOPE]),
        "gckr": _col(ckn[C_NOPE:]),
        "gckrp": _col(ckn[C_NOPE:][_ROPE_PERM]),
        "w_out_t": p["w_out"][i].T.astype(BF16),
        "norm_ffn": p["norm_ffn"][i][None, :],
        "w_route": jnp.stack([w_route.astype(BF16), (w_route - w_route.astype(BF16).astype(F32)).astype(BF16)]),
    }


def _pick_tile(l, pref):
    t = min(l, pref)
    while l % t:
        t //= 2
    return t


def kernel(x, c, ctx, c_ctx, w_ada, b_ada, norm_mix, w_in, a_v_norm, a_w_s, a_b_s, b_q_norm, b_k_norm, b_rpb,
           c_q_a_norm, c_w_q_up, c_kv_a_norm, c_w_kv_up, c_q_norm, c_k_norm, w_out, norm_ffn,
           moe_w_group, moe_w_router, moe_w1, moe_w3, moe_w2):
    p = dict(norm_mix=norm_mix, w_in=w_in, a_v_norm=a_v_norm, a_w_s=a_w_s, a_b_s=a_b_s, b_q_norm=b_q_norm,
             b_k_norm=b_k_norm, c_q_a_norm=c_q_a_norm, c_w_q_up=c_w_q_up, c_kv_a_norm=c_kv_a_norm,
             c_w_kv_up=c_w_kv_up, c_q_norm=c_q_norm, c_k_norm=c_k_norm, w_out=w_out, norm_ffn=norm_ffn,
             moe_w_group=moe_w_group, moe_w_router=moe_w_router)
    bn, l, d = x.shape
    nc = ctx.shape[1]
    depth = w_ada.shape[0]
    rows = l // GRID_W
    tm = _pick_tile(l, 512)
    tq = _pick_tile(l, MLA_QUERY_TILE)

    cos_t, sin_t = _rope_tables_t(l)
    cos_x = jnp.ones((C_ROPE, nc), F32)
    sin_x = jnp.zeros((C_ROPE, nc), F32)
    cc8 = jnp.concatenate([c, c_ctx[None, :], jnp.zeros((8 - bn - 1, d), F32)], axis=0)

    nb = -(-TOP_K * bn * (l + nc) // MOE_BLOCK) + N_EXPERTS
    xs = jnp.zeros((nb * MOE_BLOCK * TOKEN_ROWS, LANE), F32)

    xc = ctx
    for i in range(depth):
        need_ctx = i < depth - 1
        lw = _layer_weights(i, p)
        mod = _ada(cc8, w_ada[i], b_ada[i][None, :])
        mods = [mod[:bn, j * d:(j + 1) * d][:, None, :] for j in range(6)]
        modx = [jnp.broadcast_to(mod[bn, j * d:(j + 1) * d][None, None, :], (bn, 1, d)) for j in range(6)]
        sh1, s1, g1, sh2, s2, g2 = mods
        sh1x, s1x, g1x, sh2x, s2x, g2x = modx

        oa, qb, kb, vb, qc, kc, vc = _mixin(x, sh1, s1, lw, cos_t, sin_t, tm)
        oax, qbx, kbx, vbx, qcx, kcx, vcx = _mixin(xc, sh1x, s1x, lw, cos_x, sin_x, nc)

        table = _na_bias_table(b_rpb[i], rows)
        ob = _na_attention(qb, kb, vb, kbx, vbx, table)
        oc = _mla_attention(qc, kc, vc, kcx, vcx, tq)
        n_lat = bn * l
        n_tok = n_lat + (bn * nc if need_ctx else 0)
        x, h2t, ei, wf, cnt = _mixout(x, oa, ob, oc, g1, sh2, s2, lw, tm, n_tok)
        if need_ctx:
            obx = _flash(qbx, kbx[:, None], vbx,
                         pl.BlockSpec((1, 1, nc, LANE), lambda b, h, ii, j: (b, 0, j, h // 2)),
                         tq=nc, tk=nc, name="ctx_dense_attention")
            ocx = _flash(qcx, kcx, vcx.reshape(bn, C_WIDTH, nc),
                         pl.BlockSpec((1, 1, nc, QPAD), lambda b, h, ii, j: (b, h, j, 0)),
                         tq=nc, tk=nc, name="ctx_mla_attention")
            xc, h2t, ei, wf, cnt = _mixout(xc, oax, obx, ocx, g1x, sh2x, s2x, lw, nc, n_tok, row0=n_lat,
                                           bufs=(h2t, ei, wf), cnt_in=cnt)

        counts = cnt[:, 0].astype(jnp.int32)
        pstarts, block_e, block_i, used = _expert_blocks(counts, nb, MOE_BLOCK)
        xs = _dispatch(pstarts, ei, h2t, xs, tm)
        ys = _experts(xs, block_e, block_i, used, moe_w1, moe_w3, moe_w2, i, MOE_BLOCK)
        x = _combine(x, pstarts, ei, wf, ys, g2, 0, tm)
        if need_ctx:
            xc = _combine(xc, pstarts, ei, wf, ys, g2x, n_lat, nc)
    return x
```

```python
import functools

import numpy as np
import jax
import jax.numpy as jnp
from jax import lax
from jax.experimental import pallas as pl
from jax.experimental.pallas import tpu as pltpu

F32 = jnp.float32
BF16 = jnp.bfloat16

D_MODEL = 1024
GRID_W = 64
HEAD_DIM = 64
EPS = 1e-6
NEG_INF = -1e30

A_HEADS = 4
A_WIDTH = A_HEADS * HEAD_DIM
CHUNK = 128
B_HEADS = 6
B_WIDTH = B_HEADS * HEAD_DIM
NA_ROWS = 8
NA_COLS = 16
C_HEADS = 6
C_NOPE = 64
C_ROPE = 32
C_QK = C_NOPE + C_ROPE
C_VDIM = 64
C_Q_LORA = 384
C_KV_LORA = 256
C_WIDTH = C_HEADS * C_VDIM
ROPE_BASE = 10000.0
IN_A = 2 * A_WIDTH
IN_B = 3 * B_WIDTH
IN_C = C_Q_LORA + C_KV_LORA + C_ROPE
IN_AUG = IN_A + IN_B + IN_C + C_ROPE

N_GROUPS = 8
EXPERTS_PER_GROUP = 8
N_EXPERTS = N_GROUPS * EXPERTS_PER_GROUP
TOP_K = 2
D_EXPERT = D_MODEL // 2
ROUTER_PAD = 128

V7X_VMEM_LIMIT_BYTES = 56 * 1024 * 1024
LANE = 128
QPAD = 128

MLA_KEY_CHUNK = 256
MLA_QUERY_TILE = 1024
NA_QROWS = 8
NA_KBLK_ROWS = 4
MOE_BLOCK = 512

HIGHEST = lax.Precision.HIGHEST
LOG2E = 1.4426950408889634


def _cparams(sem):
    return pltpu.CompilerParams(dimension_semantics=sem, vmem_limit_bytes=V7X_VMEM_LIMIT_BYTES)


def _ada_body(c_ref, w_ref, b_ref, o_ref):
    cc = c_ref[...]
    s = cc * (1.0 / (1.0 + jnp.exp(-cc)))
    o_ref[...] = jnp.dot(s, w_ref[...], preferred_element_type=F32, precision=HIGHEST) + b_ref[...]


def _ada(cc8, w, b):
    n_out = w.shape[1]
    tn = 1024
    return pl.pallas_call(
        _ada_body,
        grid=(n_out // tn,),
        in_specs=[pl.BlockSpec((8, D_MODEL), lambda j: (0, 0)),
                  pl.BlockSpec((D_MODEL, tn), lambda j: (0, j)),
                  pl.BlockSpec((1, tn), lambda j: (0, j))],
        out_specs=pl.BlockSpec((8, tn), lambda j: (0, j)),
        out_shape=jax.ShapeDtypeStruct((8, n_out), F32),
        compiler_params=_cparams(("arbitrary",)),
        name="ada_mod",
    )(cc8, w, b)


def _rms_rows(v):
    return lax.rsqrt(jnp.mean(v * v, axis=0, keepdims=True) + EPS)


def _gelu_tanh(x):
    return 0.5 * x * (1.0 + jnp.tanh(0.7978845608028654 * (x + 0.044715 * (x * x * x))))


def _mixin_body(x_ref, sh_ref, sc_ref, gn_ref, wt_ref, avn_ref, wst_ref, bs_ref, gq_ref, gk_ref,
                cqa_ref, wq_ref, ckva_ref, wkv_ref, gcqn_ref, gcqr_ref, gcqrp_ref, gckn_ref, gckr_ref,
                gckrp_ref, cos_ref, sin_ref,
                oa_ref, qb_ref, kb_ref, vb_ref, qc_ref, kc_ref, vc_ref, *, tm):
    x = x_ref[0]
    h = x * lax.rsqrt(jnp.mean(x * x, axis=-1, keepdims=True) + EPS) * gn_ref[...]
    h = h * (1.0 + sc_ref[0]) + sh_ref[0]
    zt = lax.dot_general(wt_ref[...], h.astype(BF16), (((1,), (1,)), ((), ())),
                         preferred_element_type=F32)

    ga = _gelu_tanh(zt[0:IN_A])
    u = ga[0:A_WIDTH]
    v = ga[A_WIDTH:IN_A]
    vn = (v * _rms_rows(v) * avn_ref[...]).astype(BF16)
    for hd in range(A_HEADS):
        r0, r1 = hd * HEAD_DIM, (hd + 1) * HEAD_DIM
        for c in range(tm // CHUNK):
            c0, c1 = c * CHUNK, (c + 1) * CHUNK
            vm = jnp.dot(vn[r0:r1, c0:c1], wst_ref[hd], preferred_element_type=F32) + bs_ref[hd]
            oa_ref[0, r0:r1, c0:c1] = (u[r0:r1, c0:c1] * vm).astype(BF16)

    zb = zt[IN_A:IN_A + IN_B]
    zeros64 = jnp.zeros((HEAD_DIM, tm), F32)
    kn = []
    for hd in range(B_HEADS):
        q = zb[hd * HEAD_DIM:(hd + 1) * HEAD_DIM]
        k = zb[B_WIDTH + hd * HEAD_DIM:B_WIDTH + (hd + 1) * HEAD_DIM]
        qn = q * _rms_rows(q) * gq_ref[...]
        kn.append(k * _rms_rows(k) * gk_ref[...])
        pair = [qn, zeros64] if hd % 2 == 0 else [zeros64, qn]
        qb_ref[0, hd] = jnp.concatenate(pair, axis=0).astype(BF16)
    for p in range(B_HEADS // 2):
        kt = jnp.concatenate([kn[2 * p], kn[2 * p + 1]], axis=0)
        kb_ref[0, :, p * LANE:(p + 1) * LANE] = kt.T.astype(BF16)
    vb_ref[0] = zb[2 * B_WIDTH:3 * B_WIDTH].astype(BF16)

    zc = zt[IN_A + IN_B:IN_AUG]
    ql = zc[0:C_Q_LORA]
    kvl = zc[C_Q_LORA:C_Q_LORA + C_KV_LORA]
    kr = zc[C_Q_LORA + C_KV_LORA:C_Q_LORA + C_KV_LORA + C_ROPE]
    krp = zc[C_Q_LORA + C_KV_LORA + C_ROPE:C_Q_LORA + C_KV_LORA + 2 * C_ROPE]
    qln = (ql * _rms_rows(ql) * cqa_ref[...]).astype(BF16)
    kvln = (kvl * _rms_rows(kvl) * ckva_ref[...]).astype(BF16)
    qt = jnp.dot(wq_ref[...], qln, preferred_element_type=F32)
    kvt = jnp.dot(wkv_ref[...], kvln, preferred_element_type=F32)
    cos = cos_ref[...]
    sin = sin_ref[...]
    krn = _rms_rows(kr) * (gckr_ref[...] * kr * cos + gckrp_ref[...] * krp * sin)
    zeros32 = jnp.zeros((QPAD - C_QK, tm), F32)
    nq = C_HEADS * C_NOPE
    nr = C_HEADS * C_ROPE
    for hd in range(C_HEADS):
        qn = qt[hd * C_NOPE:(hd + 1) * C_NOPE]
        qn = qn * _rms_rows(qn) * gcqn_ref[...]
        qr = qt[nq + hd * C_ROPE:nq + (hd + 1) * C_ROPE]
        qrp = qt[nq + nr + hd * C_ROPE:nq + nr + (hd + 1) * C_ROPE]
        qrn = _rms_rows(qr) * (gcqr_ref[...] * qr * cos + gcqrp_ref[...] * qrp * sin)
        qc_ref[0, hd] = jnp.concatenate([qn, qrn, zeros32], axis=0).astype(BF16)
        kn_c = kvt[hd * C_NOPE:(hd + 1) * C_NOPE]
        kn_c = kn_c * _rms_rows(kn_c) * gckn_ref[...]
        kc_ref[0, hd] = jnp.concatenate([kn_c, krn, zeros32], axis=0).T.astype(BF16)
    vt = vc_ref.shape[3]
    v_rows = kvt[nq:nq + C_WIDTH].astype(BF16)
    for c in range(tm // vt):
        vc_ref[0, c] = v_rows[:, c * vt:(c + 1) * vt]


def _mixin(x, sh, sc, lw, cos_t, sin_t, tm):
    bn, l, _ = x.shape
    vt = min(MLA_KEY_CHUNK, tm)
    const2 = lambda b, t: (0, 0)
    const3 = lambda b, t: (0, 0, 0)
    in_specs = [
        pl.BlockSpec((1, tm, D_MODEL), lambda b, t: (b, t, 0)),
        pl.BlockSpec((1, 1, D_MODEL), lambda b, t: (b, 0, 0)),
        pl.BlockSpec((1, 1, D_MODEL), lambda b, t: (b, 0, 0)),
        pl.BlockSpec((1, D_MODEL), const2),
        pl.BlockSpec((IN_AUG, D_MODEL), const2),
        pl.BlockSpec((A_WIDTH, 1), const2),
        pl.BlockSpec((A_HEADS, CHUNK, CHUNK), const3),
        pl.BlockSpec((A_HEADS, 1, CHUNK), const3),
        pl.BlockSpec((HEAD_DIM, 1), const2),
        pl.BlockSpec((HEAD_DIM, 1), const2),
        pl.BlockSpec((C_Q_LORA, 1), const2),
        pl.BlockSpec((C_HEADS * (C_NOPE + 2 * C_ROPE), C_Q_LORA), const2),
        pl.BlockSpec((C_KV_LORA, 1), const2),
        pl.BlockSpec((C_HEADS * (C_NOPE + C_VDIM), C_KV_LORA), const2),
        pl.BlockSpec((C_NOPE, 1), const2),
        pl.BlockSpec((C_ROPE, 1), const2),
        pl.BlockSpec((C_ROPE, 1), const2),
        pl.BlockSpec((C_NOPE, 1), const2),
        pl.BlockSpec((C_ROPE, 1), const2),
        pl.BlockSpec((C_ROPE, 1), const2),
        pl.BlockSpec((C_ROPE, tm), lambda b, t: (0, t)),
        pl.BlockSpec((C_ROPE, tm), lambda b, t: (0, t)),
    ]
    out_shape = (
        jax.ShapeDtypeStruct((bn, A_WIDTH, l), BF16),
        jax.ShapeDtypeStruct((bn, B_HEADS, QPAD, l), BF16),
        jax.ShapeDtypeStruct((bn, l, B_WIDTH), BF16),
        jax.ShapeDtypeStruct((bn, B_WIDTH, l), BF16),
        jax.ShapeDtypeStruct((bn, C_HEADS, QPAD, l), BF16),
        jax.ShapeDtypeStruct((bn, C_HEADS, l, QPAD), BF16),
        jax.ShapeDtypeStruct((bn, l // vt, C_WIDTH, vt), BF16),
    )
    out_specs = (
        pl.BlockSpec((1, A_WIDTH, tm), lambda b, t: (b, 0, t)),
        pl.BlockSpec((1, B_HEADS, QPAD, tm), lambda b, t: (b, 0, 0, t)),
        pl.BlockSpec((1, tm, B_WIDTH), lambda b, t: (b, t, 0)),
        pl.BlockSpec((1, B_WIDTH, tm), lambda b, t: (b, 0, t)),
        pl.BlockSpec((1, C_HEADS, QPAD, tm), lambda b, t: (b, 0, 0, t)),
        pl.BlockSpec((1, C_HEADS, tm, QPAD), lambda b, t: (b, 0, t, 0)),
        pl.BlockSpec((1, tm // vt, C_WIDTH, vt), lambda b, t: (b, t, 0, 0)),
    )
    return pl.pallas_call(
        functools.partial(_mixin_body, tm=tm),
        grid=(bn, l // tm),
        in_specs=in_specs,
        out_specs=out_specs,
        out_shape=out_shape,
        compiler_params=_cparams(("arbitrary", "arbitrary")),
        name="mix_in",
    )(x, sh, sc, lw["norm_mix"], lw["w_in_t"], lw["avn"], lw["wst"], lw["bs"], lw["gq"], lw["gk"],
      lw["cqa"], lw["wq_t"], lw["ckva"], lw["wkv_t"], lw["gcqn"], lw["gcqr"], lw["gcqrp"], lw["gckn"],
      lw["gckr"], lw["gckrp"], cos_t, sin_t)


DEN_ROWS = 16


def _na_body(q_ref, k0_ref, k1_ref, k2_ref, k3_ref, v0_ref, v1_ref, v2_ref, v3_ref, kx_ref, vx_ref, t_ref,
             o_ref):
    k_refs = (k0_ref, k1_ref, k2_ref, k3_ref)
    v_refs = (v0_ref, v1_ref, v2_ref, v3_ref)
    nk = k0_ref.shape[1]

    def scores(hd):
        q = q_ref[0, hd]
        lanes = slice((hd // 2) * LANE, (hd // 2 + 1) * LANE)
        s = jnp.concatenate([jnp.dot(kr[0, :, lanes], q, preferred_element_type=F32) for kr in k_refs], axis=0)
        s = (s + t_ref[hd, 0]).astype(BF16)
        sx = jnp.dot(kx_ref[0, :, lanes], q, preferred_element_type=F32).astype(BF16)
        return s, sx

    def finish(hd, s, sx):
        rows = slice(hd * HEAD_DIM, (hd + 1) * HEAD_DIM)
        m = jnp.maximum(jnp.max(s, axis=0, keepdims=True), jnp.max(sx, axis=0, keepdims=True))
        p = jnp.exp2(s - m)
        px = jnp.exp2(sx - m)
        o = jnp.dot(jnp.concatenate([vx_ref[0, rows], jnp.ones((DEN_ROWS, px.shape[0]), BF16)], axis=0), px,
                    preferred_element_type=F32)
        ones = jnp.ones((DEN_ROWS, nk), BF16)
        for j, vr in enumerate(v_refs):
            o = o + jnp.dot(jnp.concatenate([vr[0, rows], ones], axis=0), p[j * nk:(j + 1) * nk],
                            preferred_element_type=F32)
        o_ref[0, rows] = (o[0:HEAD_DIM] / o[HEAD_DIM:HEAD_DIM + 1]).astype(BF16)

    pending = scores(0)
    for hd in range(B_HEADS):
        nxt = scores(hd + 1) if hd + 1 < B_HEADS else None
        finish(hd, *pending)
        pending = nxt


def _na_attention(qb, kb, vb, kxb, vxb, table):
    bn, _, _, l = qb.shape
    nq = NA_QROWS * GRID_W
    nk = NA_KBLK_ROWS * GRID_W
    nblk = l // nq
    nkb = l // nk
    nctx = kxb.shape[1]

    def kmap(j):
        return lambda b, i: (b, jnp.clip(2 * i - 1 + j, 0, nkb - 1), 0)

    def vmap_(j):
        return lambda b, i: (b, 0, jnp.clip(2 * i - 1 + j, 0, nkb - 1))

    def tmap(b, i):
        return (0, jnp.where(i == 0, 0, jnp.where(i == nblk - 1, 2, 1)), 0, 0)

    in_specs = ([pl.BlockSpec((1, B_HEADS, QPAD, nq), lambda b, i: (b, 0, 0, i))]
                + [pl.BlockSpec((1, nk, B_WIDTH), kmap(j)) for j in range(4)]
                + [pl.BlockSpec((1, B_WIDTH, nk), vmap_(j)) for j in range(4)]
                + [pl.BlockSpec((1, nctx, B_WIDTH), lambda b, i: (b, 0, 0)),
                   pl.BlockSpec((1, B_WIDTH, nctx), lambda b, i: (b, 0, 0)),
                   pl.BlockSpec((B_HEADS, 1, 4 * nk, nq), tmap)])
    return pl.pallas_call(
        _na_body,
        grid=(bn, nblk),
        in_specs=in_specs,
        out_specs=pl.BlockSpec((1, B_WIDTH, nq), lambda b, i: (b, 0, i)),
        out_shape=jax.ShapeDtypeStruct((bn, B_WIDTH, l), BF16),
        compiler_params=_cparams(("arbitrary", "arbitrary")),
        name="na_attention",
    )(qb, kb, kb, kb, kb, vb, vb, vb, vb, kxb, vxb, table)


def _na_bias_table(rpb, rows):
    nblk = rows // NA_QROWS
    qc = np.arange(GRID_W)
    kc = np.arange(GRID_W)
    c0 = np.clip(qc - NA_COLS // 2, 0, GRID_W - NA_COLS)
    valid_col = (kc[:, None] >= c0[None, :]) & (kc[:, None] < c0[None, :] + NA_COLS)
    dc = np.clip(kc[:, None] - qc[None, :], -(NA_COLS - 1), NA_COLS - 1) + NA_COLS - 1
    dc_onehot = (dc[None] == np.arange(2 * NA_COLS - 1)[:, None, None]).astype(np.float32)
    tabs = []
    for i in (0, 1, nblk - 1):
        kr = NA_KBLK_ROWS * (2 * i - 1) + np.arange(4 * NA_KBLK_ROWS)
        qr = NA_QROWS * i + np.arange(NA_QROWS)
        r0 = np.clip(qr - NA_ROWS // 2, 0, rows - NA_ROWS)
        valid_row = ((kr[:, None] >= r0[None, :]) & (kr[:, None] < r0[None, :] + NA_ROWS)
                     & (kr[:, None] >= 0) & (kr[:, None] < rows))
        dr = np.clip(kr[:, None] - qr[None, :] + NA_ROWS - 1, 0, 2 * NA_ROWS - 2)
        bias = jnp.einsum("hkqd,dcx->hkcqx", rpb[:, dr].astype(F32), dc_onehot, precision=HIGHEST)
        valid = valid_row[:, None, :, None] & valid_col[None, :, None, :]
        tabs.append(jnp.where(valid[None], bias.astype(F32) * LOG2E, NEG_INF).reshape(
            rpb.shape[0], 4 * NA_KBLK_ROWS * GRID_W, NA_QROWS * GRID_W))
    return jnp.stack(tabs, axis=1)


def _flash_body(*refs, has_extra):
    if has_extra:
        q_ref, k_ref, v_ref, kx_ref, vx_ref, o_ref, m_sc, l_sc, acc_sc = refs
    else:
        q_ref, k_ref, v_ref, o_ref, m_sc, l_sc, acc_sc = refs
    kv = pl.program_id(3)
    nkv = pl.num_programs(3)

    @pl.when(kv == 0)
    def _():
        m_sc[...] = jnp.full(m_sc.shape, -jnp.inf, F32)
        l_sc[...] = jnp.zeros(l_sc.shape, F32)
        acc_sc[...] = jnp.zeros(acc_sc.shape, F32)

    def step(k, v):
        s = jnp.dot(k, q_ref[0, 0], preferred_element_type=F32)
        m_old = m_sc[...]
        m_new = jnp.maximum(m_old, jnp.max(s, axis=0, keepdims=True))
        alpha = jnp.exp2(m_old - m_new)
        p = jnp.exp2(s - m_new)
        l_sc[...] = alpha * l_sc[...] + jnp.sum(p, axis=0, keepdims=True)
        acc_sc[...] = alpha * acc_sc[...] + jnp.dot(v, p.astype(BF16), preferred_element_type=F32)
        m_sc[...] = m_new

    step(k_ref[0, 0], v_ref[0])

    @pl.when(kv == nkv - 1)
    def _():
        if has_extra:
            step(kx_ref[0, 0], vx_ref[0])
        o_ref[0] = (acc_sc[...] / l_sc[...]).astype(BF16)


def _flash(q, k, v, k_spec, kx=None, vx=None, kx_spec=None, *, tq, tk, name):
    bn, nh, _, lq = q.shape
    lk = v.shape[2]
    has_extra = kx is not None
    in_specs = [pl.BlockSpec((1, 1, QPAD, tq), lambda b, h, i, j: (b, h, 0, i)),
                k_spec,
                pl.BlockSpec((1, HEAD_DIM, tk), lambda b, h, i, j: (b, h, j))]
    args = [q, k, v]
    if has_extra:
        nx = vx.shape[2]
        in_specs += [kx_spec, pl.BlockSpec((1, HEAD_DIM, nx), lambda b, h, i, j: (b, h, 0))]
        args += [kx, vx]
    return pl.pallas_call(
        functools.partial(_flash_body, has_extra=has_extra),
        grid=(bn, nh, lq // tq, lk // tk),
        in_specs=in_specs,
        out_specs=pl.BlockSpec((1, HEAD_DIM, tq), lambda b, h, i, j: (b, h, i)),
        out_shape=jax.ShapeDtypeStruct((bn, nh * HEAD_DIM, lq), BF16),
        scratch_shapes=[pltpu.VMEM((1, tq), F32), pltpu.VMEM((1, tq), F32), pltpu.VMEM((HEAD_DIM, tq), F32)],
        compiler_params=_cparams(("arbitrary", "arbitrary", "arbitrary", "arbitrary")),
        name=name,
    )(*args)


def _mla_body(q_ref, k_ref, v_ref, kx_ref, vx_ref, o_ref, *, tk, nchunks):
    q = q_ref[0, 0]
    tq = q.shape[1]

    def scores(k):
        return jnp.dot(k, q, preferred_element_type=F32).astype(BF16)

    def absorb(s, v, carry):
        m, den, acc = carry
        m_new = jnp.maximum(m, jnp.max(s, axis=0, keepdims=True).astype(F32))
        alpha = jnp.exp2(m - m_new)
        p = jnp.exp2(s - m_new.astype(BF16))
        half = p.shape[0] // 2
        den = alpha * den + jnp.sum((p[0:half] + p[half:]).astype(F32), axis=0, keepdims=True)
        acc = alpha * acc + jnp.dot(v, p, preferred_element_type=F32)
        return m_new, den, acc

    carry = (jnp.full((1, tq), -jnp.inf, F32), jnp.zeros((1, tq), F32), jnp.zeros((HEAD_DIM, tq), F32))
    s_cur = scores(k_ref[0, 0, 0:tk, :])
    for j in range(nchunks):
        if j + 1 < nchunks:
            s_next = scores(k_ref[0, 0, (j + 1) * tk:(j + 2) * tk, :])
        else:
            s_next = scores(kx_ref[0, 0])
        carry = absorb(s_cur, v_ref[0, j], carry)
        s_cur = s_next
    _, den, acc = absorb(s_cur, vx_ref[0, 0], carry)
    o_ref[0] = (acc / den).astype(BF16)


def _mla_attention(q, k, v, kx, vx, tq):
    bn, nh, _, l = q.shape
    nchunks, tk = v.shape[1], v.shape[3]
    nc = kx.shape[2]
    return pl.pallas_call(
        functools.partial(_mla_body, tk=tk, nchunks=nchunks),
        grid=(bn, nh, l // tq),
        in_specs=[pl.BlockSpec((1, 1, QPAD, tq), lambda b, h, i: (b, h, 0, i)),
                  pl.BlockSpec((1, 1, l, QPAD), lambda b, h, i: (b, h, 0, 0)),
                  pl.BlockSpec((1, nchunks, HEAD_DIM, tk), lambda b, h, i: (b, 0, h, 0)),
                  pl.BlockSpec((1, 1, nc, QPAD), lambda b, h, i: (b, h, 0, 0)),
                  pl.BlockSpec((1, 1, HEAD_DIM, nc), lambda b, h, i: (b, 0, h, 0))],
        out_specs=pl.BlockSpec((1, HEAD_DIM, tq), lambda b, h, i: (b, h, i)),
        out_shape=jax.ShapeDtypeStruct((bn, nh * HEAD_DIM, l), BF16),
        compiler_params=_cparams(("arbitrary", "arbitrary", "arbitrary")),
        name="mla_attention",
    )(q, k, v, kx, vx)


TOKEN_ROWS = 8


def _first_argmax_rows(v, row_id):
    vmax = jnp.max(v, axis=0, keepdims=True)
    idx = jnp.min(jnp.where(v == vmax, row_id, float(v.shape[0])), axis=0, keepdims=True)
    return vmax, idx


def _mixout_body(x_ref, oa_ref, ob_ref, oc_ref, wt_ref, g1_ref, sh2_ref, sc2_ref, gn2_ref, wr_ref, tri_ref,
                 *rest, tm, steps, has_base):
    if has_base:
        cnt_in_ref = rest[0]
    xo_ref, h2_ref, ei_ref, wf_ref, cnt_ref, base_sc = rest[-6:]
    i = pl.program_id(0)

    @pl.when(i == 0)
    def _():
        if has_base:
            base_sc[...] = cnt_in_ref[:, 0:1]
        else:
            base_sc[...] = jnp.zeros(base_sc.shape, F32)

    ot = jnp.concatenate([oa_ref[0], ob_ref[0], oc_ref[0]], axis=0)
    out_t = jnp.dot(wt_ref[...], ot, preferred_element_type=F32)
    xn = x_ref[0] + g1_ref[0] * out_t.T
    xo_ref[0] = xn
    h2 = xn * lax.rsqrt(jnp.mean(xn * xn, axis=-1, keepdims=True) + EPS) * gn2_ref[...]
    h2 = h2 * (1.0 + sc2_ref[0]) + sh2_ref[0]
    for s in range(TOKEN_ROWS):
        h2_ref[pl.ds(s, tm, stride=TOKEN_ROWS), :] = h2[:, s * LANE:(s + 1) * LANE]

    h_hi = h2.astype(BF16)
    h_lo = (h2 - h_hi.astype(F32)).astype(BF16)
    lg = (jnp.dot(h_hi, wr_ref[0], preferred_element_type=F32) + jnp.dot(h_lo, wr_ref[0], preferred_element_type=F32)
          + jnp.dot(h_hi, wr_ref[1], preferred_element_type=F32))
    lt = lg.T
    gl = lt[0:N_GROUPS]
    rid = lax.broadcasted_iota(jnp.int32, (N_GROUPS, tm), 0).astype(F32)
    gmax, g_idx = _first_argmax_rows(gl, rid)
    g_gate = 1.0 / jnp.sum(jnp.exp(gl - gmax), axis=0, keepdims=True)
    e_sel = jnp.zeros((EXPERTS_PER_GROUP, tm), F32)
    for g in range(N_GROUPS):
        lo = N_GROUPS + g * EXPERTS_PER_GROUP
        e_sel = jnp.where(g_idx == float(g), lt[lo:lo + EXPERTS_PER_GROUP], e_sel)
    v1, j1 = _first_argmax_rows(e_sel, rid)
    v2, j2 = _first_argmax_rows(jnp.where(rid == j1, -jnp.inf, e_sel), rid)
    t21 = jnp.exp(v2 - v1)
    w1 = g_gate / (1.0 + t21)
    w2 = g_gate * t21 / (1.0 + t21)
    e1 = g_idx * float(EXPERTS_PER_GROUP) + j1
    e2 = g_idx * float(EXPERTS_PER_GROUP) + j2

    eid = lax.broadcasted_iota(jnp.int32, (N_EXPERTS, tm), 0).astype(F32)
    oh1 = (eid == e1).astype(F32)
    oh2 = (eid == e2).astype(F32)
    tri = tri_ref[...]
    cum1 = jnp.dot(oh1.astype(BF16), tri, preferred_element_type=F32)
    cum2 = jnp.dot(oh2.astype(BF16), tri, preferred_element_type=F32)
    tot1 = jnp.sum(oh1, axis=1, keepdims=True)
    tot2 = jnp.sum(oh2, axis=1, keepdims=True)
    base = base_sc[...]
    r1 = jnp.sum(oh1 * (base + cum1), axis=0, keepdims=True)
    r2 = jnp.sum(oh2 * (base + tot1 + cum2), axis=0, keepdims=True)
    live = jnp.where(i < steps, 1.0, 0.0)
    base_new = base + live * (tot1 + tot2)
    base_sc[...] = base_new
    cnt_ref[...] = jnp.broadcast_to(base_new, cnt_ref.shape)

    zeros4 = jnp.zeros((4, tm), F32)
    ei_ref[...] = jnp.concatenate([e1, e2, r1, r2, zeros4], axis=0).astype(jnp.int32)
    wpad = jnp.concatenate([w1, w2, jnp.zeros((LANE - 2, tm), F32)], axis=0)
    wf_ref[...] = wpad.T


def _mixout(x, oa, ob, oc, g1, sh2, sc2, lw, tm, n_rows, row0=0, bufs=None, cnt_in=None):
    bn, l, _ = x.shape
    nt = l // tm
    steps = bn * nt
    extra = 0
    if bufs is None and n_rows > bn * l:
        assert n_rows - bn * l == tm, "spare rows must be exactly one tile"
        extra = 1
    blk0 = row0 // tm

    def bt(i):
        ii = jnp.minimum(i, steps - 1)
        return ii // nt, ii % nt

    const2 = lambda i: (0, 0)
    modspec = pl.BlockSpec((1, 1, D_MODEL), lambda i: (bt(i)[0], 0, 0))
    rowspec = pl.BlockSpec((1, tm, D_MODEL), lambda i: (bt(i)[0], bt(i)[1], 0))
    tri = jnp.asarray(np.triu(np.ones((tm, tm), np.float32), k=1), BF16)
    in_specs = [rowspec,
                pl.BlockSpec((1, A_WIDTH, tm), lambda i: (bt(i)[0], 0, bt(i)[1])),
                pl.BlockSpec((1, B_WIDTH, tm), lambda i: (bt(i)[0], 0, bt(i)[1])),
                pl.BlockSpec((1, C_WIDTH, tm), lambda i: (bt(i)[0], 0, bt(i)[1])),
                pl.BlockSpec((D_MODEL, D_MODEL), const2),
                modspec, modspec, modspec,
                pl.BlockSpec((1, D_MODEL), const2),
                pl.BlockSpec((2, D_MODEL, ROUTER_PAD), lambda i: (0, 0, 0)),
                pl.BlockSpec((tm, tm), const2)]
    args = [x, oa, ob, oc, lw["w_out_t"], g1, sh2, sc2, lw["norm_ffn"], lw["w_route"], tri]
    aliases = {}
    if bufs is not None:
        in_specs.append(pl.BlockSpec((N_EXPERTS, LANE), const2))
        args.append(cnt_in)
        aliases = {len(args): 1, len(args) + 1: 2, len(args) + 2: 3}
        in_specs += [pl.BlockSpec(memory_space=pl.ANY)] * 3
        args += list(bufs)
    return pl.pallas_call(
        functools.partial(_mixout_body, tm=tm, steps=steps, has_base=bufs is not None),
        grid=(steps + extra,),
        in_specs=in_specs,
        out_specs=(rowspec,
                   pl.BlockSpec((tm * TOKEN_ROWS, LANE), lambda i: (blk0 + i, 0)),
                   pl.BlockSpec((8, tm), lambda i: (0, blk0 + i)),
                   pl.BlockSpec((tm, LANE), lambda i: (blk0 + i, 0)),
                   pl.BlockSpec((N_EXPERTS, LANE), const2)),
        out_shape=(jax.ShapeDtypeStruct((bn, l, D_MODEL), F32),
                   jax.ShapeDtypeStruct((n_rows * TOKEN_ROWS, LANE), F32),
                   jax.ShapeDtypeStruct((8, n_rows), jnp.int32),
                   jax.ShapeDtypeStruct((n_rows, LANE), F32),
                   jax.ShapeDtypeStruct((N_EXPERTS, LANE), F32)),
        scratch_shapes=[pltpu.VMEM((N_EXPERTS, 1), F32)],
        input_output_aliases=aliases,
        compiler_params=_cparams(("arbitrary",)),
        name="mix_out",
    )(*args)


def _tile_rows(idx):
    return pl.ds(pl.multiple_of(idx * TOKEN_ROWS, TOKEN_ROWS), TOKEN_ROWS)


def _slot(ps_ref, ei_ref, k, r):
    return ps_ref[ei_ref[k, r]] + ei_ref[TOP_K + k, r]


def _dispatch_body(ps_ref, ei_ref, h_ref, xs_in_ref, xs_ref, sem, *, td):
    del xs_in_ref

    def one(r, carry):
        for k in range(TOP_K):
            pltpu.make_async_copy(h_ref.at[_tile_rows(r)], xs_ref.at[_tile_rows(_slot(ps_ref, ei_ref, k, r))],
                                  sem).start(priority=k)
        return carry
    lax.fori_loop(0, td, one, 0, unroll=8)
    for k in range(TOP_K):
        pltpu.make_async_copy(h_ref, xs_ref.at[pl.ds(0, td * TOKEN_ROWS)], sem).wait()


def _dispatch(pstarts, ei, h2t, xs0, td):
    n_tok = ei.shape[1]
    grid_spec = pltpu.PrefetchScalarGridSpec(
        num_scalar_prefetch=1,
        grid=(n_tok // td,),
        in_specs=[pl.BlockSpec((8, td), lambda i, ps: (0, i), memory_space=pltpu.SMEM),
                  pl.BlockSpec((td * TOKEN_ROWS, LANE), lambda i, ps: (i, 0)),
                  pl.BlockSpec(memory_space=pl.ANY)],
        out_specs=pl.BlockSpec(memory_space=pl.ANY),
        scratch_shapes=[pltpu.SemaphoreType.DMA(())],
    )
    return pl.pallas_call(
        functools.partial(_dispatch_body, td=td),
        grid_spec=grid_spec,
        out_shape=jax.ShapeDtypeStruct(xs0.shape, F32),
        input_output_aliases={3: 0},
        compiler_params=_cparams(("arbitrary",)),
        name="moe_dispatch",
    )(pstarts, ei, h2t, xs0)


def _experts_body(be_ref, bi_ref, used_ref, x_ref, w1_ref, w3_ref, w2_ref, y_ref, w1b, w3b, w2b, *, tb):
    i = pl.program_id(0)
    used = used_ref[i] > 0

    @pl.when(used)
    def _():
        prev_e = be_ref[jnp.maximum(i - 1, 0)]

        @pl.when((i == 0) | (prev_e != be_ref[i]))
        def _():
            w1b[...] = w1_ref[0, 0].astype(BF16)
            w3b[...] = w3_ref[0, 0].astype(BF16)
            w2b[...] = w2_ref[0, 0].astype(BF16)

        xb = jnp.concatenate([x_ref[pl.ds(s, tb, stride=TOKEN_ROWS), :] for s in range(TOKEN_ROWS)],
                             axis=1).astype(BF16)
        a = jnp.dot(xb, w1b[...], preferred_element_type=F32)
        b = jnp.dot(xb, w3b[...], preferred_element_type=F32)
        hm = (a * (1.0 / (1.0 + jnp.exp(-a))) * b).astype(BF16)
        y = jnp.dot(hm, w2b[...], preferred_element_type=F32)
        for s in range(TOKEN_ROWS):
            y_ref[pl.ds(s, tb, stride=TOKEN_ROWS), :] = y[:, s * LANE:(s + 1) * LANE]

    @pl.when(jnp.logical_not(used))
    def _():
        y_ref[...] = jnp.zeros(y_ref.shape, F32)


def _experts(xs, block_e, block_i, used, w1, w3, w2, layer, tb):
    nb = block_e.shape[0]
    wspec_up = pl.BlockSpec((1, 1, D_MODEL, D_EXPERT), lambda i, be, bi, us: (layer, be[i], 0, 0))
    grid_spec = pltpu.PrefetchScalarGridSpec(
        num_scalar_prefetch=3,
        grid=(nb,),
        in_specs=[pl.BlockSpec((tb * TOKEN_ROWS, LANE), lambda i, be, bi, us: (bi[i], 0)),
                  wspec_up, wspec_up,
                  pl.BlockSpec((1, 1, D_EXPERT, D_MODEL), lambda i, be, bi, us: (layer, be[i], 0, 0))],
        out_specs=pl.BlockSpec((tb * TOKEN_ROWS, LANE), lambda i, be, bi, us: (i, 0)),
        scratch_shapes=[pltpu.VMEM((D_MODEL, D_EXPERT), BF16), pltpu.VMEM((D_MODEL, D_EXPERT), BF16),
                        pltpu.VMEM((D_EXPERT, D_MODEL), BF16)],
    )
    return pl.pallas_call(
        functools.partial(_experts_body, tb=tb),
        grid_spec=grid_spec,
        out_shape=jax.ShapeDtypeStruct(xs.shape, F32),
        compiler_params=_cparams(("arbitrary",)),
        name="moe_experts",
    )(block_e, block_i, used, xs, w1, w3, w2)


def _combine_body(ps_ref, ei_ref, wf_ref, x_ref, g2_ref, ys_ref, o_ref, buf0, buf1, sem, *, tm):
    bufs = (buf0, buf1)

    def one(r, carry):
        for k in range(TOP_K):
            pltpu.make_async_copy(ys_ref.at[_tile_rows(_slot(ps_ref, ei_ref, k, r))], bufs[k].at[_tile_rows(r)],
                                  sem).start(priority=k)
        return carry
    lax.fori_loop(0, tm, one, 0, unroll=8)
    for k in range(TOP_K):
        pltpu.make_async_copy(ys_ref.at[pl.ds(0, tm * TOKEN_ROWS)], bufs[k], sem).wait()
    wf = wf_ref[...]
    y = None
    for k in range(TOP_K):
        yk = jnp.concatenate([bufs[k][pl.ds(s, tm, stride=TOKEN_ROWS), :] for s in range(TOKEN_ROWS)], axis=1)
        yk = wf[:, k:k + 1] * yk
        y = yk if y is None else y + yk
    o_ref[0] = x_ref[0] + g2_ref[0] * y


def _combine(x, pstarts, ei, wf, ys, g2, row0, tm):
    bn, l, _ = x.shape
    nt = l // tm
    t0 = row0 // tm
    rowspec = pl.BlockSpec((1, tm, D_MODEL), lambda b, t, ps: (b, t, 0))
    grid_spec = pltpu.PrefetchScalarGridSpec(
        num_scalar_prefetch=1,
        grid=(bn, nt),
        in_specs=[pl.BlockSpec((8, tm), lambda b, t, ps: (0, t0 + b * nt + t), memory_space=pltpu.SMEM),
                  pl.BlockSpec((tm, LANE), lambda b, t, ps: (t0 + b * nt + t, 0)),
                  rowspec,
                  pl.BlockSpec((1, 1, D_MODEL), lambda b, t, ps: (b, 0, 0)),
                  pl.BlockSpec(memory_space=pl.ANY)],
        out_specs=rowspec,
        scratch_shapes=[pltpu.VMEM((tm * TOKEN_ROWS, LANE), F32), pltpu.VMEM((tm * TOKEN_ROWS, LANE), F32),
                        pltpu.SemaphoreType.DMA(())],
    )
    return pl.pallas_call(
        functools.partial(_combine_body, tm=tm),
        grid_spec=grid_spec,
        out_shape=jax.ShapeDtypeStruct(x.shape, F32),
        compiler_params=_cparams(("arbitrary", "arbitrary")),
        name="moe_combine",
    )(pstarts, ei, wf, x, g2, ys)


def _expert_blocks(counts, nb, tb):
    pcounts = (counts + tb - 1) // tb * tb
    pends = jnp.cumsum(pcounts)
    pstarts = (pends - pcounts).astype(jnp.int32)
    n_used = pends[-1] // tb
    blk = jnp.arange(nb, dtype=jnp.int32)
    used = blk < n_used
    last = jnp.maximum(n_used - 1, 0).astype(jnp.int32)
    block_i = jnp.where(used, blk, last)
    block_e = jnp.sum((pends[None, :] <= (block_i * tb)[:, None]).astype(jnp.int32), axis=1)
    block_e = jnp.minimum(block_e, N_EXPERTS - 1)
    return pstarts, block_e, block_i, used.astype(jnp.int32)


_ROPE_PERM = np.concatenate([np.arange(8, 16), np.arange(0, 8), np.arange(24, 32), np.arange(16, 24)])
_ROPE_SIGN = np.concatenate([-np.ones(8), np.ones(8), -np.ones(8), np.ones(8)]).astype(np.float32)


def _rope_tables_t(l):
    half = C_ROPE // 2
    inv = ROPE_BASE ** (-jnp.arange(0, half, 2, dtype=F32) / half)
    pos = jnp.arange(l)
    ang_r = (pos // GRID_W).astype(F32)[None, :] * inv[:, None]
    ang_c = (pos % GRID_W).astype(F32)[None, :] * inv[:, None]
    cos_t = jnp.concatenate([jnp.cos(ang_r)] * 2 + [jnp.cos(ang_c)] * 2, axis=0)
    sin_t = jnp.concatenate([jnp.sin(ang_r)] * 2 + [jnp.sin(ang_c)] * 2, axis=0)
    return cos_t, sin_t


def _col(v):
    return v.astype(F32)[:, None]


def _layer_weights(i, p):
    w_in = p["w_in"][i]
    kr0 = IN_A + IN_B + C_Q_LORA + C_KV_LORA
    kr_cols = w_in[:, kr0:kr0 + C_ROPE]
    w_aug = jnp.concatenate([w_in, kr_cols[:, _ROPE_PERM] * _ROPE_SIGN], axis=1)

    wq = p["c_w_q_up"][i].reshape(C_Q_LORA, C_HEADS, C_QK)
    wq_n = wq[:, :, :C_NOPE].reshape(C_Q_LORA, C_HEADS * C_NOPE)
    wq_r = wq[:, :, C_NOPE:]
    wq_rp = (wq_r[:, :, _ROPE_PERM] * _ROPE_SIGN).reshape(C_Q_LORA, C_HEADS * C_ROPE)
    wq_all = jnp.concatenate([wq_n, wq_r.reshape(C_Q_LORA, C_HEADS * C_ROPE), wq_rp], axis=1)

    wkv = p["c_w_kv_up"][i].reshape(C_KV_LORA, C_HEADS, C_NOPE + C_VDIM)
    wkv_all = jnp.concatenate([wkv[:, :, :C_NOPE].reshape(C_KV_LORA, C_HEADS * C_NOPE),
                               wkv[:, :, C_NOPE:].reshape(C_KV_LORA, C_HEADS * C_VDIM)], axis=1)

    cqn = p["c_q_norm"][i].astype(F32) * (C_QK ** -0.5 * LOG2E)
    ckn = p["c_k_norm"][i].astype(F32)
    w_route = jnp.concatenate([p["moe_w_group"][i], p["moe_w_router"][i],
                               jnp.zeros((D_MODEL, ROUTER_PAD - N_GROUPS - N_EXPERTS), F32)], axis=1)
    return {
        "norm_mix": p["norm_mix"][i][None, :],
        "w_in_t": w_aug.T.astype(BF16),
        "avn": _col(p["a_v_norm"][i]),
        "wst": jnp.transpose(p["a_w_s"][i], (0, 2, 1)).astype(BF16),
        "bs": p["a_b_s"][i][:, None, :],
        "gq": _col(p["b_q_norm"][i] * (HEAD_DIM ** -0.5 * LOG2E)),
        "gk": _col(p["b_k_norm"][i]),
        "cqa": _col(p["c_q_a_norm"][i]),
        "wq_t": wq_all.T.astype(BF16),
        "ckva": _col(p["c_kv_a_norm"][i]),
        "wkv_t": wkv_all.T.astype(BF16),
        "gcqn": _col(cqn[:C_NOPE]),
        "gcqr": _col(cqn[C_NOPE:]),
        "gcqrp": _col(cqn[C_NOPE:][_ROPE_PERM]),
        "gckn": _col(ckn[:C_NOPE]),
        "gckr": _col(ckn[C_NOPE:]),
        "gckrp": _col(ckn[C_NOPE:][_ROPE_PERM]),
        "w_out_t": p["w_out"][i].T.astype(BF16),
        "norm_ffn": p["norm_ffn"][i][None, :],
        "w_route": jnp.stack([w_route.astype(BF16), (w_route - w_route.astype(BF16).astype(F32)).astype(BF16)]),
    }


def _pick_tile(l, pref):
    t = min(l, pref)
    while l % t:
        t //= 2
    return t


def kernel(x, c, ctx, c_ctx, w_ada, b_ada, norm_mix, w_in, a_v_norm, a_w_s, a_b_s, b_q_norm, b_k_norm, b_rpb,
           c_q_a_norm, c_w_q_up, c_kv_a_norm, c_w_kv_up, c_q_norm, c_k_norm, w_out, norm_ffn,
           moe_w_group, moe_w_router, moe_w1, moe_w3, moe_w2):
    p = dict(norm_mix=norm_mix, w_in=w_in, a_v_norm=a_v_norm, a_w_s=a_w_s, a_b_s=a_b_s, b_q_norm=b_q_norm,
             b_k_norm=b_k_norm, c_q_a_norm=c_q_a_norm, c_w_q_up=c_w_q_up, c_kv_a_norm=c_kv_a_norm,
             c_w_kv_up=c_w_kv_up, c_q_norm=c_q_norm, c_k_norm=c_k_norm, w_out=w_out, norm_ffn=norm_ffn,
             moe_w_group=moe_w_group, moe_w_router=moe_w_router)
    bn, l, d = x.shape
    nc = ctx.shape[1]
    depth = w_ada.shape[0]
    rows = l // GRID_W
    tm = _pick_tile(l, 512)
    tq = _pick_tile(l, MLA_QUERY_TILE)

    cos_t, sin_t = _rope_tables_t(l)
    cos_x = jnp.ones((C_ROPE, nc), F32)
    sin_x = jnp.zeros((C_ROPE, nc), F32)
    cc8 = jnp.concatenate([c, c_ctx[None, :], jnp.zeros((8 - bn - 1, d), F32)], axis=0)

    nb = -(-TOP_K * bn * (l + nc) // MOE_BLOCK) + N_EXPERTS
    xs = jnp.zeros((nb * MOE_BLOCK * TOKEN_ROWS, LANE), F32)

    xc = ctx
    for i in range(depth):
        need_ctx = i < depth - 1
        lw = _layer_weights(i, p)
        mod = _ada(cc8, w_ada[i], b_ada[i][None, :])
        mods = [mod[:bn, j * d:(j + 1) * d][:, None, :] for j in range(6)]
        modx = [jnp.broadcast_to(mod[bn, j * d:(j + 1) * d][None, None, :], (bn, 1, d)) for j in range(6)]
        sh1, s1, g1, sh2, s2, g2 = mods
        sh1x, s1x, g1x, sh2x, s2x, g2x = modx

        oa, qb, kb, vb, qc, kc, vc = _mixin(x, sh1, s1, lw, cos_t, sin_t, tm)
        oax, qbx, kbx, vbx, qcx, kcx, vcx = _mixin(xc, sh1x, s1x, lw, cos_x, sin_x, nc)

        table = _na_bias_table(b_rpb[i], rows)
        ob = _na_attention(qb, kb, vb, kbx, vbx, table)
        oc = _mla_attention(qc, kc, vc, kcx, vcx, tq)
        n_lat = bn * l
        n_tok = n_lat + (bn * nc if need_ctx else 0)
        x, h2t, ei, wf, cnt = _mixout(x, oa, ob, oc, g1, sh2, s2, lw, tm, n_tok)
        if need_ctx:
            obx = _flash(qbx, kbx[:, None], vbx,
                         pl.BlockSpec((1, 1, nc, LANE), lambda b, h, ii, j: (b, 0, j, h // 2)),
                         tq=nc, tk=nc, name="ctx_dense_attention")
            ocx = _flash(qcx, kcx, vcx.reshape(bn, C_WIDTH, nc),
                         pl.BlockSpec((1, 1, nc, QPAD), lambda b, h, ii, j: (b, h, j, 0)),
                         tq=nc, tk=nc, name="ctx_mla_attention")
            xc, h2t, ei, wf, cnt = _mixout(xc, oax, obx, ocx, g1x, sh2x, s2x, lw, nc, n_tok, row0=n_lat,
                                           bufs=(h2t, ei, wf), cnt_in=cnt)

        counts = cnt[:, 0].astype(jnp.int32)
        pstarts, block_e, block_i, used = _expert_blocks(counts, nb, MOE_BLOCK)
        xs = _dispatch(pstarts, ei, h2t, xs, tm)
        ys = _experts(xs, block_e, block_i, used, moe_w1, moe_w3, moe_w2, i, MOE_BLOCK)
        x = _combine(x, pstarts, ei, wf, ys, g2, 0, tm)
        if need_ctx:
            xc = _combine(xc, pstarts, ei, wf, ys, g2x, n_lat, nc)
    return x
```

```python
import functools

import numpy as np
import jax
import jax.numpy as jnp
from jax import lax
from jax.experimental import pallas as pl
from jax.experimental.pallas import tpu as pltpu

F32 = jnp.float32
BF16 = jnp.bfloat16

D_MODEL = 1024
GRID_W = 64
HEAD_DIM = 64
EPS = 1e-6
NEG_INF = -1e30

A_HEADS = 4
A_WIDTH = A_HEADS * HEAD_DIM
CHUNK = 128
B_HEADS = 6
B_WIDTH = B_HEADS * HEAD_DIM
NA_ROWS = 8
NA_COLS = 16
C_HEADS = 6
C_NOPE = 64
C_ROPE = 32
C_QK = C_NOPE + C_ROPE
C_VDIM = 64
C_Q_LORA = 384
C_KV_LORA = 256
C_WIDTH = C_HEADS * C_VDIM
ROPE_BASE = 10000.0
IN_A = 2 * A_WIDTH
IN_B = 3 * B_WIDTH
IN_C = C_Q_LORA + C_KV_LORA + C_ROPE
IN_AUG = IN_A + IN_B + IN_C + C_ROPE

N_GROUPS = 8
EXPERTS_PER_GROUP = 8
N_EXPERTS = N_GROUPS * EXPERTS_PER_GROUP
TOP_K = 2
D_EXPERT = D_MODEL // 2
ROUTER_PAD = 128

V7X_VMEM_LIMIT_BYTES = 56 * 1024 * 1024
LANE = 128
QPAD = 128

MLA_KEY_CHUNK = 256
MLA_QUERY_TILE = 1024
NA_QROWS = 8
NA_KBLK_ROWS = 4
MOE_BLOCK = 512

HIGHEST = lax.Precision.HIGHEST
LOG2E = 1.4426950408889634


def _cparams(sem):
    return pltpu.CompilerParams(dimension_semantics=sem, vmem_limit_bytes=V7X_VMEM_LIMIT_BYTES)


def _ada_body(c_ref, w_ref, b_ref, o_ref):
    cc = c_ref[...]
    s = cc * (1.0 / (1.0 + jnp.exp(-cc)))
    o_ref[...] = jnp.dot(s, w_ref[...], preferred_element_type=F32, precision=HIGHEST) + b_ref[...]


def _ada(cc8, w, b):
    n_out = w.shape[1]
    tn = 1024
    return pl.pallas_call(
        _ada_body,
        grid=(n_out // tn,),
        in_specs=[pl.BlockSpec((8, D_MODEL), lambda j: (0, 0)),
                  pl.BlockSpec((D_MODEL, tn), lambda j: (0, j)),
                  pl.BlockSpec((1, tn), lambda j: (0, j))],
        out_specs=pl.BlockSpec((8, tn), lambda j: (0, j)),
        out_shape=jax.ShapeDtypeStruct((8, n_out), F32),
        compiler_params=_cparams(("arbitrary",)),
        name="ada_mod",
    )(cc8, w, b)


def _rms_rows(v):
    return lax.rsqrt(jnp.mean(v * v, axis=0, keepdims=True) + EPS)


def _gelu_tanh(x):
    return 0.5 * x * (1.0 + jnp.tanh(0.7978845608028654 * (x + 0.044715 * (x * x * x))))


def _mixin_body(x_ref, sh_ref, sc_ref, gn_ref, wt_ref, avn_ref, wst_ref, bs_ref, gq_ref, gk_ref,
                cqa_ref, wq_ref, ckva_ref, wkv_ref, gcqn_ref, gcqr_ref, gcqrp_ref, gckn_ref, gckr_ref,
                gckrp_ref, cos_ref, sin_ref,
                oa_ref, qb_ref, kb_ref, vb_ref, qc_ref, kc_ref, vc_ref, *, tm):
    x = x_ref[0]
    h = x * lax.rsqrt(jnp.mean(x * x, axis=-1, keepdims=True) + EPS) * gn_ref[...]
    h = h * (1.0 + sc_ref[0]) + sh_ref[0]
    hb16 = h.astype(BF16)

    def proj(r0, r1):
        return lax.dot_general(wt_ref[r0:r1, :], hb16, (((1,), (1,)), ((), ())), preferred_element_type=F32)

    zt_c = proj(IN_A + IN_B, IN_AUG)
    zt_a = proj(0, IN_A)
    zt_b = proj(IN_A, IN_A + IN_B)

    zc = zt_c
    ql = zc[0:C_Q_LORA]
    kvl = zc[C_Q_LORA:C_Q_LORA + C_KV_LORA]
    kr = zc[C_Q_LORA + C_KV_LORA:C_Q_LORA + C_KV_LORA + C_ROPE]
    krp = zc[C_Q_LORA + C_KV_LORA + C_ROPE:C_Q_LORA + C_KV_LORA + 2 * C_ROPE]
    qln = (ql * _rms_rows(ql) * cqa_ref[...]).astype(BF16)
    kvln = (kvl * _rms_rows(kvl) * ckva_ref[...]).astype(BF16)
    qt = jnp.dot(wq_ref[...], qln, preferred_element_type=F32)
    kvt = jnp.dot(wkv_ref[...], kvln, preferred_element_type=F32)

    ga = _gelu_tanh(zt_a)
    u = ga[0:A_WIDTH]
    v = ga[A_WIDTH:IN_A]
    vn = (v * _rms_rows(v) * avn_ref[...]).astype(BF16)
    for hd in range(A_HEADS):
        r0, r1 = hd * HEAD_DIM, (hd + 1) * HEAD_DIM
        for c in range(tm // CHUNK):
            c0, c1 = c * CHUNK, (c + 1) * CHUNK
            vm = jnp.dot(vn[r0:r1, c0:c1], wst_ref[hd], preferred_element_type=F32) + bs_ref[hd]
            oa_ref[0, r0:r1, c0:c1] = (u[r0:r1, c0:c1] * vm).astype(BF16)

    zb = zt_b
    zeros64 = jnp.zeros((HEAD_DIM, tm), F32)
    kn = []
    for hd in range(B_HEADS):
        q = zb[hd * HEAD_DIM:(hd + 1) * HEAD_DIM]
        k = zb[B_WIDTH + hd * HEAD_DIM:B_WIDTH + (hd + 1) * HEAD_DIM]
        qn = q * _rms_rows(q) * gq_ref[...]
        kn.append(k * _rms_rows(k) * gk_ref[...])
        pair = [qn, zeros64] if hd % 2 == 0 else [zeros64, qn]
        qb_ref[0, hd] = jnp.concatenate(pair, axis=0).astype(BF16)
    for p in range(B_HEADS // 2):
        kt = jnp.concatenate([kn[2 * p], kn[2 * p + 1]], axis=0)
        kb_ref[0, :, p * LANE:(p + 1) * LANE] = kt.T.astype(BF16)
    vb_ref[0] = zb[2 * B_WIDTH:3 * B_WIDTH].astype(BF16)

    cos = cos_ref[...]
    sin = sin_ref[...]
    krn = _rms_rows(kr) * (gckr_ref[...] * kr * cos + gckrp_ref[...] * krp * sin)
    zeros32 = jnp.zeros((QPAD - C_QK, tm), F32)
    nq = C_HEADS * C_NOPE
    nr = C_HEADS * C_ROPE
    for hd in range(C_HEADS):
        qn = qt[hd * C_NOPE:(hd + 1) * C_NOPE]
        qn = qn * _rms_rows(qn) * gcqn_ref[...]
        qr = qt[nq + hd * C_ROPE:nq + (hd + 1) * C_ROPE]
        qrp = qt[nq + nr + hd * C_ROPE:nq + nr + (hd + 1) * C_ROPE]
        qrn = _rms_rows(qr) * (gcqr_ref[...] * qr * cos + gcqrp_ref[...] * qrp * sin)
        qc_ref[0, hd] = jnp.concatenate([qn, qrn, zeros32], axis=0).astype(BF16)
        kn_c = kvt[hd * C_NOPE:(hd + 1) * C_NOPE]
        kn_c = kn_c * _rms_rows(kn_c) * gckn_ref[...]
        kc_ref[0, hd] = jnp.concatenate([kn_c, krn, zeros32], axis=0).T.astype(BF16)
    vt = vc_ref.shape[3]
    v_rows = kvt[nq:nq + C_WIDTH].astype(BF16)
    for c in range(tm // vt):
        vc_ref[0, c] = v_rows[:, c * vt:(c + 1) * vt]


def _mixin(x, sh, sc, lw, cos_t, sin_t, tm):
    bn, l, _ = x.shape
    vt = min(MLA_KEY_CHUNK, tm)
    const2 = lambda b, t: (0, 0)
    const3 = lambda b, t: (0, 0, 0)
    in_specs = [
        pl.BlockSpec((1, tm, D_MODEL), lambda b, t: (b, t, 0)),
        pl.BlockSpec((1, 1, D_MODEL), lambda b, t: (b, 0, 0)),
        pl.BlockSpec((1, 1, D_MODEL), lambda b, t: (b, 0, 0)),
        pl.BlockSpec((1, D_MODEL), const2),
        pl.BlockSpec((IN_AUG, D_MODEL), const2),
        pl.BlockSpec((A_WIDTH, 1), const2),
        pl.BlockSpec((A_HEADS, CHUNK, CHUNK), const3),
        pl.BlockSpec((A_HEADS, 1, CHUNK), const3),
        pl.BlockSpec((HEAD_DIM, 1), const2),
        pl.BlockSpec((HEAD_DIM, 1), const2),
        pl.BlockSpec((C_Q_LORA, 1), const2),
        pl.BlockSpec((C_HEADS * (C_NOPE + 2 * C_ROPE), C_Q_LORA), const2),
        pl.BlockSpec((C_KV_LORA, 1), const2),
        pl.BlockSpec((C_HEADS * (C_NOPE + C_VDIM), C_KV_LORA), const2),
        pl.BlockSpec((C_NOPE, 1), const2),
        pl.BlockSpec((C_ROPE, 1), const2),
        pl.BlockSpec((C_ROPE, 1), const2),
        pl.BlockSpec((C_NOPE, 1), const2),
        pl.BlockSpec((C_ROPE, 1), const2),
        pl.BlockSpec((C_ROPE, 1), const2),
        pl.BlockSpec((C_ROPE, tm), lambda b, t: (0, t)),
        pl.BlockSpec((C_ROPE, tm), lambda b, t: (0, t)),
    ]
    out_shape = (
        jax.ShapeDtypeStruct((bn, A_WIDTH, l), BF16),
        jax.ShapeDtypeStruct((bn, B_HEADS, QPAD, l), BF16),
        jax.ShapeDtypeStruct((bn, l, B_WIDTH), BF16),
        jax.ShapeDtypeStruct((bn, B_WIDTH, l), BF16),
        jax.ShapeDtypeStruct((bn, C_HEADS, QPAD, l), BF16),
        jax.ShapeDtypeStruct((bn, C_HEADS, l, QPAD), BF16),
        jax.ShapeDtypeStruct((bn, l // vt, C_WIDTH, vt), BF16),
    )
    out_specs = (
        pl.BlockSpec((1, A_WIDTH, tm), lambda b, t: (b, 0, t)),
        pl.BlockSpec((1, B_HEADS, QPAD, tm), lambda b, t: (b, 0, 0, t)),
        pl.BlockSpec((1, tm, B_WIDTH), lambda b, t: (b, t, 0)),
        pl.BlockSpec((1, B_WIDTH, tm), lambda b, t: (b, 0, t)),
        pl.BlockSpec((1, C_HEADS, QPAD, tm), lambda b, t: (b, 0, 0, t)),
        pl.BlockSpec((1, C_HEADS, tm, QPAD), lambda b, t: (b, 0, t, 0)),
        pl.BlockSpec((1, tm // vt, C_WIDTH, vt), lambda b, t: (b, t, 0, 0)),
    )
    return pl.pallas_call(
        functools.partial(_mixin_body, tm=tm),
        grid=(bn, l // tm),
        in_specs=in_specs,
        out_specs=out_specs,
        out_shape=out_shape,
        compiler_params=_cparams(("arbitrary", "arbitrary")),
        name="mix_in",
    )(x, sh, sc, lw["norm_mix"], lw["w_in_t"], lw["avn"], lw["wst"], lw["bs"], lw["gq"], lw["gk"],
      lw["cqa"], lw["wq_t"], lw["ckva"], lw["wkv_t"], lw["gcqn"], lw["gcqr"], lw["gcqrp"], lw["gckn"],
      lw["gckr"], lw["gckrp"], cos_t, sin_t)


DEN_ROWS = 16


def _na_body(q_ref, k0_ref, k1_ref, k2_ref, k3_ref, v0_ref, v1_ref, v2_ref, v3_ref, kx_ref, vx_ref, t_ref,
             o_ref):
    k_refs = (k0_ref, k1_ref, k2_ref, k3_ref)
    v_refs = (v0_ref, v1_ref, v2_ref, v3_ref)
    nk = k0_ref.shape[1]

    def scores(hd):
        q = q_ref[0, hd]
        lanes = slice((hd // 2) * LANE, (hd // 2 + 1) * LANE)
        s = jnp.concatenate([jnp.dot(kr[0, :, lanes], q, preferred_element_type=F32) for kr in k_refs], axis=0)
        s = (s + t_ref[hd, 0]).astype(BF16)
        sx = jnp.dot(kx_ref[0, :, lanes], q, preferred_element_type=F32).astype(BF16)
        return s, sx

    def finish(hd, s, sx):
        rows = slice(hd * HEAD_DIM, (hd + 1) * HEAD_DIM)
        m = jnp.maximum(jnp.max(s, axis=0, keepdims=True), jnp.max(sx, axis=0, keepdims=True))
        p = jnp.exp2(s - m)
        px = jnp.exp2(sx - m)
        o = jnp.dot(jnp.concatenate([vx_ref[0, rows], jnp.ones((DEN_ROWS, px.shape[0]), BF16)], axis=0), px,
                    preferred_element_type=F32)
        ones = jnp.ones((DEN_ROWS, nk), BF16)
        for j, vr in enumerate(v_refs):
            o = o + jnp.dot(jnp.concatenate([vr[0, rows], ones], axis=0), p[j * nk:(j + 1) * nk],
                            preferred_element_type=F32)
        o_ref[0, rows] = (o[0:HEAD_DIM] / o[HEAD_DIM:HEAD_DIM + 1]).astype(BF16)

    pending = scores(0)
    for hd in range(B_HEADS):
        nxt = scores(hd + 1) if hd + 1 < B_HEADS else None
        finish(hd, *pending)
        pending = nxt


def _na_attention(qb, kb, vb, kxb, vxb, table):
    bn, _, _, l = qb.shape
    nq = NA_QROWS * GRID_W
    nk = NA_KBLK_ROWS * GRID_W
    nblk = l // nq
    nkb = l // nk
    nctx = kxb.shape[1]

    def kmap(j):
        return lambda b, i: (b, jnp.clip(2 * i - 1 + j, 0, nkb - 1), 0)

    def vmap_(j):
        return lambda b, i: (b, 0, jnp.clip(2 * i - 1 + j, 0, nkb - 1))

    def tmap(b, i):
        return (0, jnp.where(i == 0, 0, jnp.where(i == nblk - 1, 2, 1)), 0, 0)

    in_specs = ([pl.BlockSpec((1, B_HEADS, QPAD, nq), lambda b, i: (b, 0, 0, i))]
                + [pl.BlockSpec((1, nk, B_WIDTH), kmap(j)) for j in range(4)]
                + [pl.BlockSpec((1, B_WIDTH, nk), vmap_(j)) for j in range(4)]
                + [pl.BlockSpec((1, nctx, B_WIDTH), lambda b, i: (b, 0, 0)),
                   pl.BlockSpec((1, B_WIDTH, nctx), lambda b, i: (b, 0, 0)),
                   pl.BlockSpec((B_HEADS, 1, 4 * nk, nq), tmap)])
    return pl.pallas_call(
        _na_body,
        grid=(bn, nblk),
        in_specs=in_specs,
        out_specs=pl.BlockSpec((1, B_WIDTH, nq), lambda b, i: (b, 0, i)),
        out_shape=jax.ShapeDtypeStruct((bn, B_WIDTH, l), BF16),
        compiler_params=_cparams(("arbitrary", "arbitrary")),
        name="na_attention",
    )(qb, kb, kb, kb, kb, vb, vb, vb, vb, kxb, vxb, table)


def _na_bias_table(rpb, rows):
    nblk = rows // NA_QROWS
    qc = np.arange(GRID_W)
    kc = np.arange(GRID_W)
    c0 = np.clip(qc - NA_COLS // 2, 0, GRID_W - NA_COLS)
    valid_col = (kc[:, None] >= c0[None, :]) & (kc[:, None] < c0[None, :] + NA_COLS)
    dc = np.clip(kc[:, None] - qc[None, :], -(NA_COLS - 1), NA_COLS - 1) + NA_COLS - 1
    dc_onehot = (dc[None] == np.arange(2 * NA_COLS - 1)[:, None, None]).astype(np.float32)
    tabs = []
    for i in (0, 1, nblk - 1):
        kr = NA_KBLK_ROWS * (2 * i - 1) + np.arange(4 * NA_KBLK_ROWS)
        qr = NA_QROWS * i + np.arange(NA_QROWS)
        r0 = np.clip(qr - NA_ROWS // 2, 0, rows - NA_ROWS)
        valid_row = ((kr[:, None] >= r0[None, :]) & (kr[:, None] < r0[None, :] + NA_ROWS)
                     & (kr[:, None] >= 0) & (kr[:, None] < rows))
        dr = np.clip(kr[:, None] - qr[None, :] + NA_ROWS - 1, 0, 2 * NA_ROWS - 2)
        bias = jnp.einsum("hkqd,dcx->hkcqx", rpb[:, dr].astype(F32), dc_onehot, precision=HIGHEST)
        valid = valid_row[:, None, :, None] & valid_col[None, :, None, :]
        tabs.append(jnp.where(valid[None], bias.astype(F32) * LOG2E, NEG_INF).reshape(
            rpb.shape[0], 4 * NA_KBLK_ROWS * GRID_W, NA_QROWS * GRID_W))
    return jnp.stack(tabs, axis=1)


def _flash_body(*refs, has_extra):
    if has_extra:
        q_ref, k_ref, v_ref, kx_ref, vx_ref, o_ref, m_sc, l_sc, acc_sc = refs
    else:
        q_ref, k_ref, v_ref, o_ref, m_sc, l_sc, acc_sc = refs
    kv = pl.program_id(3)
    nkv = pl.num_programs(3)

    @pl.when(kv == 0)
    def _():
        m_sc[...] = jnp.full(m_sc.shape, -jnp.inf, F32)
        l_sc[...] = jnp.zeros(l_sc.shape, F32)
        acc_sc[...] = jnp.zeros(acc_sc.shape, F32)

    def step(k, v):
        s = jnp.dot(k, q_ref[0, 0], preferred_element_type=F32)
        m_old = m_sc[...]
        m_new = jnp.maximum(m_old, jnp.max(s, axis=0, keepdims=True))
        alpha = jnp.exp2(m_old - m_new)
        p = jnp.exp2(s - m_new)
        l_sc[...] = alpha * l_sc[...] + jnp.sum(p, axis=0, keepdims=True)
        acc_sc[...] = alpha * acc_sc[...] + jnp.dot(v, p.astype(BF16), preferred_element_type=F32)
        m_sc[...] = m_new

    step(k_ref[0, 0], v_ref[0])

    @pl.when(kv == nkv - 1)
    def _():
        if has_extra:
            step(kx_ref[0, 0], vx_ref[0])
        o_ref[0] = (acc_sc[...] / l_sc[...]).astype(BF16)


def _flash(q, k, v, k_spec, kx=None, vx=None, kx_spec=None, *, tq, tk, name):
    bn, nh, _, lq = q.shape
    lk = v.shape[2]
    has_extra = kx is not None
    in_specs = [pl.BlockSpec((1, 1, QPAD, tq), lambda b, h, i, j: (b, h, 0, i)),
                k_spec,
                pl.BlockSpec((1, HEAD_DIM, tk), lambda b, h, i, j: (b, h, j))]
    args = [q, k, v]
    if has_extra:
        nx = vx.shape[2]
        in_specs += [kx_spec, pl.BlockSpec((1, HEAD_DIM, nx), lambda b, h, i, j: (b, h, 0))]
        args += [kx, vx]
    return pl.pallas_call(
        functools.partial(_flash_body, has_extra=has_extra),
        grid=(bn, nh, lq // tq, lk // tk),
        in_specs=in_specs,
        out_specs=pl.BlockSpec((1, HEAD_DIM, tq), lambda b, h, i, j: (b, h, i)),
        out_shape=jax.ShapeDtypeStruct((bn, nh * HEAD_DIM, lq), BF16),
        scratch_shapes=[pltpu.VMEM((1, tq), F32), pltpu.VMEM((1, tq), F32), pltpu.VMEM((HEAD_DIM, tq), F32)],
        compiler_params=_cparams(("arbitrary", "arbitrary", "arbitrary", "arbitrary")),
        name=name,
    )(*args)


def _mla_body(q_ref, k_ref, v_ref, kx_ref, vx_ref, o_ref, *, tk, nchunks):
    q = q_ref[0, 0]
    tq = q.shape[1]

    def scores(k):
        return jnp.dot(k, q, preferred_element_type=F32).astype(BF16)

    def absorb(s, v, carry):
        m, den, acc = carry
        m_new = jnp.maximum(m, jnp.max(s, axis=0, keepdims=True).astype(F32))
        alpha = jnp.exp2(m - m_new)
        p = jnp.exp2(s - m_new.astype(BF16))
        half = p.shape[0] // 2
        den = alpha * den + jnp.sum((p[0:half] + p[half:]).astype(F32), axis=0, keepdims=True)
        acc = alpha * acc + jnp.dot(v, p, preferred_element_type=F32)
        return m_new, den, acc

    carry = (jnp.full((1, tq), -jnp.inf, F32), jnp.zeros((1, tq), F32), jnp.zeros((HEAD_DIM, tq), F32))
    s_cur = scores(k_ref[0, 0, 0:tk, :])
    for j in range(nchunks):
        if j + 1 < nchunks:
            s_next = scores(k_ref[0, 0, (j + 1) * tk:(j + 2) * tk, :])
        else:
            s_next = scores(kx_ref[0, 0])
        carry = absorb(s_cur, v_ref[0, j], carry)
        s_cur = s_next
    _, den, acc = absorb(s_cur, vx_ref[0, 0], carry)
    o_ref[0] = (acc / den).astype(BF16)


def _mla_attention(q, k, v, kx, vx, tq):
    bn, nh, _, l = q.shape
    nchunks, tk = v.shape[1], v.shape[3]
    nc = kx.shape[2]
    return pl.pallas_call(
        functools.partial(_mla_body, tk=tk, nchunks=nchunks),
        grid=(bn, nh, l // tq),
        in_specs=[pl.BlockSpec((1, 1, QPAD, tq), lambda b, h, i: (b, h, 0, i)),
                  pl.BlockSpec((1, 1, l, QPAD), lambda b, h, i: (b, h, 0, 0)),
                  pl.BlockSpec((1, nchunks, HEAD_DIM, tk), lambda b, h, i: (b, 0, h, 0)),
                  pl.BlockSpec((1, 1, nc, QPAD), lambda b, h, i: (b, h, 0, 0)),
                  pl.BlockSpec((1, 1, HEAD_DIM, nc), lambda b, h, i: (b, 0, h, 0))],
        out_specs=pl.BlockSpec((1, HEAD_DIM, tq), lambda b, h, i: (b, h, i)),
        out_shape=jax.ShapeDtypeStruct((bn, nh * HEAD_DIM, l), BF16),
        compiler_params=_cparams(("arbitrary", "arbitrary", "arbitrary")),
        name="mla_attention",
    )(q, k, v, kx, vx)


TOKEN_ROWS = 8


def _first_argmax_rows(v, row_id):
    vmax = jnp.max(v, axis=0, keepdims=True)
    idx = jnp.min(jnp.where(v == vmax, row_id, float(v.shape[0])), axis=0, keepdims=True)
    return vmax, idx


def _mixout_body(x_ref, oa_ref, ob_ref, oc_ref, wt_ref, g1_ref, sh2_ref, sc2_ref, gn2_ref, wr_ref, tri_ref,
                 *rest, tm, steps, has_base):
    if has_base:
        cnt_in_ref = rest[0]
    xo_ref, h2_ref, ei_ref, wf_ref, cnt_ref, base_sc = rest[-6:]
    i = pl.program_id(0)

    @pl.when(i == 0)
    def _():
        if has_base:
            base_sc[...] = cnt_in_ref[:, 0:1]
        else:
            base_sc[...] = jnp.zeros(base_sc.shape, F32)

    ot = jnp.concatenate([oa_ref[0], ob_ref[0], oc_ref[0]], axis=0)
    slab = D_MODEL // 4
    out = jnp.concatenate([jnp.dot(wt_ref[r * slab:(r + 1) * slab, :], ot, preferred_element_type=F32).T
                           for r in range(4)], axis=1)
    xn = x_ref[0] + g1_ref[0] * out
    xo_ref[0] = xn
    h2 = xn * lax.rsqrt(jnp.mean(xn * xn, axis=-1, keepdims=True) + EPS) * gn2_ref[...]
    h2 = h2 * (1.0 + sc2_ref[0]) + sh2_ref[0]
    for s in range(TOKEN_ROWS):
        h2_ref[pl.ds(s, tm, stride=TOKEN_ROWS), :] = h2[:, s * LANE:(s + 1) * LANE]

    h_hi = h2.astype(BF16)
    h_lo = (h2 - h_hi.astype(F32)).astype(BF16)
    lg = (jnp.dot(h_hi, wr_ref[0], preferred_element_type=F32) + jnp.dot(h_lo, wr_ref[0], preferred_element_type=F32)
          + jnp.dot(h_hi, wr_ref[1], preferred_element_type=F32))
    lt = lg.T
    gl = lt[0:N_GROUPS]
    rid = lax.broadcasted_iota(jnp.int32, (N_GROUPS, tm), 0).astype(F32)
    gmax, g_idx = _first_argmax_rows(gl, rid)
    g_gate = 1.0 / jnp.sum(jnp.exp(gl - gmax), axis=0, keepdims=True)
    e_sel = jnp.zeros((EXPERTS_PER_GROUP, tm), F32)
    for g in range(N_GROUPS):
        lo = N_GROUPS + g * EXPERTS_PER_GROUP
        e_sel = jnp.where(g_idx == float(g), lt[lo:lo + EXPERTS_PER_GROUP], e_sel)
    v1, j1 = _first_argmax_rows(e_sel, rid)
    v2, j2 = _first_argmax_rows(jnp.where(rid == j1, -jnp.inf, e_sel), rid)
    t21 = jnp.exp(v2 - v1)
    w1 = g_gate / (1.0 + t21)
    w2 = g_gate * t21 / (1.0 + t21)
    e1 = g_idx * float(EXPERTS_PER_GROUP) + j1
    e2 = g_idx * float(EXPERTS_PER_GROUP) + j2

    eid = lax.broadcasted_iota(jnp.int32, (N_EXPERTS, tm), 0).astype(F32)
    oh1 = (eid == e1).astype(F32)
    oh2 = (eid == e2).astype(F32)
    tri = tri_ref[...]
    cum1 = jnp.dot(oh1.astype(BF16), tri, preferred_element_type=F32)
    cum2 = jnp.dot(oh2.astype(BF16), tri, preferred_element_type=F32)
    tot1 = jnp.sum(oh1, axis=1, keepdims=True)
    tot2 = jnp.sum(oh2, axis=1, keepdims=True)
    base = base_sc[...]
    r1 = jnp.sum(oh1 * (base + cum1), axis=0, keepdims=True)
    r2 = jnp.sum(oh2 * (base + tot1 + cum2), axis=0, keepdims=True)
    live = jnp.where(i < steps, 1.0, 0.0)
    base_new = base + live * (tot1 + tot2)
    base_sc[...] = base_new
    cnt_ref[...] = jnp.broadcast_to(base_new, cnt_ref.shape)

    zeros4 = jnp.zeros((4, tm), F32)
    ei_ref[...] = jnp.concatenate([e1, e2, r1, r2, zeros4], axis=0).astype(jnp.int32)
    wpad = jnp.concatenate([w1, w2, jnp.zeros((LANE - 2, tm), F32)], axis=0)
    wf_ref[...] = wpad.T


def _mixout(x, oa, ob, oc, g1, sh2, sc2, lw, tm, n_rows, row0=0, bufs=None, cnt_in=None):
    bn, l, _ = x.shape
    nt = l // tm
    steps = bn * nt
    extra = 0
    if bufs is None and n_rows > bn * l:
        assert n_rows - bn * l == tm, "spare rows must be exactly one tile"
        extra = 1
    blk0 = row0 // tm

    def bt(i):
        ii = jnp.minimum(i, steps - 1)
        return ii // nt, ii % nt

    const2 = lambda i: (0, 0)
    modspec = pl.BlockSpec((1, 1, D_MODEL), lambda i: (bt(i)[0], 0, 0))
    rowspec = pl.BlockSpec((1, tm, D_MODEL), lambda i: (bt(i)[0], bt(i)[1], 0))
    tri = jnp.asarray(np.triu(np.ones((tm, tm), np.float32), k=1), BF16)
    in_specs = [rowspec,
                pl.BlockSpec((1, A_WIDTH, tm), lambda i: (bt(i)[0], 0, bt(i)[1])),
                pl.BlockSpec((1, B_WIDTH, tm), lambda i: (bt(i)[0], 0, bt(i)[1])),
                pl.BlockSpec((1, C_WIDTH, tm), lambda i: (bt(i)[0], 0, bt(i)[1])),
                pl.BlockSpec((D_MODEL, D_MODEL), const2),
                modspec, modspec, modspec,
                pl.BlockSpec((1, D_MODEL), const2),
                pl.BlockSpec((2, D_MODEL, ROUTER_PAD), lambda i: (0, 0, 0)),
                pl.BlockSpec((tm, tm), const2)]
    args = [x, oa, ob, oc, lw["w_out_t"], g1, sh2, sc2, lw["norm_ffn"], lw["w_route"], tri]
    aliases = {}
    if bufs is not None:
        in_specs.append(pl.BlockSpec((N_EXPERTS, LANE), const2))
        args.append(cnt_in)
        aliases = {len(args): 1, len(args) + 1: 2, len(args) + 2: 3}
        in_specs += [pl.BlockSpec(memory_space=pl.ANY)] * 3
        args += list(bufs)
    return pl.pallas_call(
        functools.partial(_mixout_body, tm=tm, steps=steps, has_base=bufs is not None),
        grid=(steps + extra,),
        in_specs=in_specs,
        out_specs=(rowspec,
                   pl.BlockSpec((tm * TOKEN_ROWS, LANE), lambda i: (blk0 + i, 0)),
                   pl.BlockSpec((8, tm), lambda i: (0, blk0 + i)),
                   pl.BlockSpec((tm, LANE), lambda i: (blk0 + i, 0)),
                   pl.BlockSpec((N_EXPERTS, LANE), const2)),
        out_shape=(jax.ShapeDtypeStruct((bn, l, D_MODEL), F32),
                   jax.ShapeDtypeStruct((n_rows * TOKEN_ROWS, LANE), F32),
                   jax.ShapeDtypeStruct((8, n_rows), jnp.int32),
                   jax.ShapeDtypeStruct((n_rows, LANE), F32),
                   jax.ShapeDtypeStruct((N_EXPERTS, LANE), F32)),
        scratch_shapes=[pltpu.VMEM((N_EXPERTS, 1), F32)],
        input_output_aliases=aliases,
        compiler_params=_cparams(("arbitrary",)),
        name="mix_out",
    )(*args)


def _tile_rows(idx):
    return pl.ds(pl.multiple_of(idx * TOKEN_ROWS, TOKEN_ROWS), TOKEN_ROWS)


def _slot(ps_ref, ei_ref, k, r):
    return ps_ref[ei_ref[k, r]] + ei_ref[TOP_K + k, r]


def _dispatch_body(ps_ref, ei_ref, h_ref, xs_in_ref, xs_ref, sem, *, td):
    del xs_in_ref

    def one(r, carry):
        for k in range(TOP_K):
            pltpu.make_async_copy(h_ref.at[_tile_rows(r)], xs_ref.at[_tile_rows(_slot(ps_ref, ei_ref, k, r))],
                                  sem).start(priority=k)
        return carry
    lax.fori_loop(0, td, one, 0, unroll=8)
    for k in range(TOP_K):
        pltpu.make_async_copy(h_ref, xs_ref.at[pl.ds(0, td * TOKEN_ROWS)], sem).wait()


def _dispatch(pstarts, ei, h2t, xs0, td):
    n_tok = ei.shape[1]
    grid_spec = pltpu.PrefetchScalarGridSpec(
        num_scalar_prefetch=1,
        grid=(n_tok // td,),
        in_specs=[pl.BlockSpec((8, td), lambda i, ps: (0, i), memory_space=pltpu.SMEM),
                  pl.BlockSpec((td * TOKEN_ROWS, LANE), lambda i, ps: (i, 0)),
                  pl.BlockSpec(memory_space=pl.ANY)],
        out_specs=pl.BlockSpec(memory_space=pl.ANY),
        scratch_shapes=[pltpu.SemaphoreType.DMA(())],
    )
    return pl.pallas_call(
        functools.partial(_dispatch_body, td=td),
        grid_spec=grid_spec,
        out_shape=jax.ShapeDtypeStruct(xs0.shape, F32),
        input_output_aliases={3: 0},
        compiler_params=_cparams(("arbitrary",)),
        name="moe_dispatch",
    )(pstarts, ei, h2t, xs0)


def _experts_body(be_ref, bi_ref, used_ref, x_ref, w1_ref, w3_ref, w2_ref, y_ref, w1b, w3b, w2b, *, tb):
    i = pl.program_id(0)
    used = used_ref[i] > 0

    @pl.when(used)
    def _():
        prev_e = be_ref[jnp.maximum(i - 1, 0)]

        @pl.when((i == 0) | (prev_e != be_ref[i]))
        def _():
            w1b[...] = w1_ref[0, 0].astype(BF16)
            w3b[...] = w3_ref[0, 0].astype(BF16)
            w2b[...] = w2_ref[0, 0].astype(BF16)

        xb = jnp.concatenate([x_ref[pl.ds(s, tb, stride=TOKEN_ROWS), :] for s in range(TOKEN_ROWS)],
                             axis=1).astype(BF16)
        a = jnp.dot(xb, w1b[...], preferred_element_type=F32)
        b = jnp.dot(xb, w3b[...], preferred_element_type=F32)
        hm = (a * (1.0 / (1.0 + jnp.exp(-a))) * b).astype(BF16)
        y = jnp.dot(hm, w2b[...], preferred_element_type=F32)
        for s in range(TOKEN_ROWS):
            y_ref[pl.ds(s, tb, stride=TOKEN_ROWS), :] = y[:, s * LANE:(s + 1) * LANE]

    @pl.when(jnp.logical_not(used))
    def _():
        y_ref[...] = jnp.zeros(y_ref.shape, F32)


def _experts(xs, block_e, block_i, used, w1, w3, w2, layer, tb):
    nb = block_e.shape[0]
    wspec_up = pl.BlockSpec((1, 1, D_MODEL, D_EXPERT), lambda i, be, bi, us: (layer, be[i], 0, 0))
    grid_spec = pltpu.PrefetchScalarGridSpec(
        num_scalar_prefetch=3,
        grid=(nb,),
        in_specs=[pl.BlockSpec((tb * TOKEN_ROWS, LANE), lambda i, be, bi, us: (bi[i], 0)),
                  wspec_up, wspec_up,
                  pl.BlockSpec((1, 1, D_EXPERT, D_MODEL), lambda i, be, bi, us: (layer, be[i], 0, 0))],
        out_specs=pl.BlockSpec((tb * TOKEN_ROWS, LANE), lambda i, be, bi, us: (i, 0)),
        scratch_shapes=[pltpu.VMEM((D_MODEL, D_EXPERT), BF16), pltpu.VMEM((D_MODEL, D_EXPERT), BF16),
                        pltpu.VMEM((D_EXPERT, D_MODEL), BF16)],
    )
    return pl.pallas_call(
        functools.partial(_experts_body, tb=tb),
        grid_spec=grid_spec,
        out_shape=jax.ShapeDtypeStruct(xs.shape, F32),
        compiler_params=_cparams(("arbitrary",)),
        name="moe_experts",
    )(block_e, block_i, used, xs, w1, w3, w2)


def _combine_body(ps_ref, ei_ref, wf_ref, x_ref, g2_ref, ys_ref, o_ref, buf0, buf1, sem, *, tm):
    bufs = (buf0, buf1)

    def one(r, carry):
        for k in range(TOP_K):
            pltpu.make_async_copy(ys_ref.at[_tile_rows(_slot(ps_ref, ei_ref, k, r))], bufs[k].at[_tile_rows(r)],
                                  sem).start(priority=k)
        return carry
    lax.fori_loop(0, tm, one, 0, unroll=8)
    for k in range(TOP_K):
        pltpu.make_async_copy(ys_ref.at[pl.ds(0, tm * TOKEN_ROWS)], bufs[k], sem).wait()
    wf = wf_ref[...]
    y = None
    for k in range(TOP_K):
        yk = jnp.concatenate([bufs[k][pl.ds(s, tm, stride=TOKEN_ROWS), :] for s in range(TOKEN_ROWS)], axis=1)
        yk = wf[:, k:k + 1] * yk
        y = yk if y is None else y + yk
    o_ref[0] = x_ref[0] + g2_ref[0] * y


def _combine(x, pstarts, ei, wf, ys, g2, row0, tm):
    bn, l, _ = x.shape
    nt = l // tm
    t0 = row0 // tm
    rowspec = pl.BlockSpec((1, tm, D_MODEL), lambda b, t, ps: (b, t, 0))
    grid_spec = pltpu.PrefetchScalarGridSpec(
        num_scalar_prefetch=1,
        grid=(bn, nt),
        in_specs=[pl.BlockSpec((8, tm), lambda b, t, ps: (0, t0 + b * nt + t), memory_space=pltpu.SMEM),
                  pl.BlockSpec((tm, LANE), lambda b, t, ps: (t0 + b * nt + t, 0)),
                  rowspec,
                  pl.BlockSpec((1, 1, D_MODEL), lambda b, t, ps: (b, 0, 0)),
                  pl.BlockSpec(memory_space=pl.ANY)],
        out_specs=rowspec,
        scratch_shapes=[pltpu.VMEM((tm * TOKEN_ROWS, LANE), F32), pltpu.VMEM((tm * TOKEN_ROWS, LANE), F32),
                        pltpu.SemaphoreType.DMA(())],
    )
    return pl.pallas_call(
        functools.partial(_combine_body, tm=tm),
        grid_spec=grid_spec,
        out_shape=jax.ShapeDtypeStruct(x.shape, F32),
        compiler_params=_cparams(("arbitrary", "arbitrary")),
        name="moe_combine",
    )(pstarts, ei, wf, x, g2, ys)


def _expert_blocks(counts, nb, tb):
    pcounts = (counts + tb - 1) // tb * tb
    pends = jnp.cumsum(pcounts)
    pstarts = (pends - pcounts).astype(jnp.int32)
    n_used = pends[-1] // tb
    blk = jnp.arange(nb, dtype=jnp.int32)
    used = blk < n_used
    last = jnp.maximum(n_used - 1, 0).astype(jnp.int32)
    block_i = jnp.where(used, blk, last)
    block_e = jnp.sum((pends[None, :] <= (block_i * tb)[:, None]).astype(jnp.int32), axis=1)
    block_e = jnp.minimum(block_e, N_EXPERTS - 1)
    return pstarts, block_e, block_i, used.astype(jnp.int32)


_ROPE_PERM = np.concatenate([np.arange(8, 16), np.arange(0, 8), np.arange(24, 32), np.arange(16, 24)])
_ROPE_SIGN = np.concatenate([-np.ones(8), np.ones(8), -np.ones(8), np.ones(8)]).astype(np.float32)


def _rope_tables_t(l):
    half = C_ROPE // 2
    inv = ROPE_BASE ** (-jnp.arange(0, half, 2, dtype=F32) / half)
    pos = jnp.arange(l)
    ang_r = (pos // GRID_W).astype(F32)[None, :] * inv[:, None]
    ang_c = (pos % GRID_W).astype(F32)[None, :] * inv[:, None]
    cos_t = jnp.concatenate([jnp.cos(ang_r)] * 2 + [jnp.cos(ang_c)] * 2, axis=0)
    sin_t = jnp.concatenate([jnp.sin(ang_r)] * 2 + [jnp.sin(ang_c)] * 2, axis=0)
    return cos_t, sin_t


def _col(v):
    return v.astype(F32)[:, None]


def _layer_weights(i, p):
    w_in = p["w_in"][i]
    kr0 = IN_A + IN_B + C_Q_LORA + C_KV_LORA
    kr_cols = w_in[:, kr0:kr0 + C_ROPE]
    w_aug = jnp.concatenate([w_in, kr_cols[:, _ROPE_PERM] * _ROPE_SIGN], axis=1)

    wq = p["c_w_q_up"][i].reshape(C_Q_LORA, C_HEADS, C_QK)
    wq_n = wq[:, :, :C_NOPE].reshape(C_Q_LORA, C_HEADS * C_NOPE)
    wq_r = wq[:, :, C_NOPE:]
    wq_rp = (wq_r[:, :, _ROPE_PERM] * _ROPE_SIGN).reshape(C_Q_LORA, C_HEADS * C_ROPE)
    wq_all = jnp.concatenate([wq_n, wq_r.reshape(C_Q_LORA, C_HEADS * C_ROPE), wq_rp], axis=1)

    wkv = p["c_w_kv_up"][i].reshape(C_KV_LORA, C_HEADS, C_NOPE + C_VDIM)
    wkv_all = jnp.concatenate([wkv[:, :, :C_NOPE].reshape(C_KV_LORA, C_HEADS * C_NOPE),
                               wkv[:, :, C_NOPE:].reshape(C_KV_LORA, C_HEADS * C_VDIM)], axis=1)

    cqn = p["c_q_norm"][i].astype(F32) * (C_QK ** -0.5 * LOG2E)
    ckn = p["c_k_norm"][i].astype(F32)
    w_route = jnp.concatenate([p["moe_w_group"][i], p["moe_w_router"][i],
                               jnp.zeros((D_MODEL, ROUTER_PAD - N_GROUPS - N_EXPERTS), F32)], axis=1)
    return {
        "norm_mix": p["norm_mix"][i][None, :],
        "w_in_t": w_aug.T.astype(BF16),
        "avn": _col(p["a_v_norm"][i]),
        "wst": jnp.transpose(p["a_w_s"][i], (0, 2, 1)).astype(BF16),
        "bs": p["a_b_s"][i][:, None, :],
        "gq": _col(p["b_q_norm"][i] * (HEAD_DIM ** -0.5 * LOG2E)),
        "gk": _col(p["b_k_norm"][i]),
        "cqa": _col(p["c_q_a_norm"][i]),
        "wq_t": wq_all.T.astype(BF16),
        "ckva": _col(p["c_kv_a_norm"][i]),
        "wkv_t": wkv_all.T.astype(BF16),
        "gcqn": _col(cqn[:C_NOPE]),
        "gcqr": _col(cqn[C_NOPE:]),
        "gcqrp": _col(cqn[C_NOPE:][_ROPE_PERM]),
        "gckn": _col(ckn[:C_NOPE]),
        "gckr": _col(ckn[C_NOPE:]),
        "gckrp": _col(ckn[C_NOPE:][_ROPE_PERM]),
        "w_out_t": p["w_out"][i].T.astype(BF16),
        "norm_ffn": p["norm_ffn"][i][None, :],
        "w_route": jnp.stack([w_route.astype(BF16), (w_route - w_route.astype(BF16).astype(F32)).astype(BF16)]),
    }


def _pick_tile(l, pref):
    t = min(l, pref)
    while l % t:
        t //= 2
    return t


def kernel(x, c, ctx, c_ctx, w_ada, b_ada, norm_mix, w_in, a_v_norm, a_w_s, a_b_s, b_q_norm, b_k_norm, b_rpb,
           c_q_a_norm, c_w_q_up, c_kv_a_norm, c_w_kv_up, c_q_norm, c_k_norm, w_out, norm_ffn,
           moe_w_group, moe_w_router, moe_w1, moe_w3, moe_w2):
    p = dict(norm_mix=norm_mix, w_in=w_in, a_v_norm=a_v_norm, a_w_s=a_w_s, a_b_s=a_b_s, b_q_norm=b_q_norm,
             b_k_norm=b_k_norm, c_q_a_norm=c_q_a_norm, c_w_q_up=c_w_q_up, c_kv_a_norm=c_kv_a_norm,
             c_w_kv_up=c_w_kv_up, c_q_norm=c_q_norm, c_k_norm=c_k_norm, w_out=w_out, norm_ffn=norm_ffn,
             moe_w_group=moe_w_group, moe_w_router=moe_w_router)
    bn, l, d = x.shape
    nc = ctx.shape[1]
    depth = w_ada.shape[0]
    rows = l // GRID_W
    tm = _pick_tile(l, 512)
    tq = _pick_tile(l, MLA_QUERY_TILE)

    cos_t, sin_t = _rope_tables_t(l)
    cos_x = jnp.ones((C_ROPE, nc), F32)
    sin_x = jnp.zeros((C_ROPE, nc), F32)
    cc8 = jnp.concatenate([c, c_ctx[None, :], jnp.zeros((8 - bn - 1, d), F32)], axis=0)

    nb = -(-TOP_K * bn * (l + nc) // MOE_BLOCK) + N_EXPERTS
    xs = jnp.zeros((nb * MOE_BLOCK * TOKEN_ROWS, LANE), F32)

    xc = ctx
    for i in range(depth):
        need_ctx = i < depth - 1
        lw = _layer_weights(i, p)
        mod = _ada(cc8, w_ada[i], b_ada[i][None, :])
        mods = [mod[:bn, j * d:(j + 1) * d][:, None, :] for j in range(6)]
        modx = [jnp.broadcast_to(mod[bn, j * d:(j + 1) * d][None, None, :], (bn, 1, d)) for j in range(6)]
        sh1, s1, g1, sh2, s2, g2 = mods
        sh1x, s1x, g1x, sh2x, s2x, g2x = modx

        oa, qb, kb, vb, qc, kc, vc = _mixin(x, sh1, s1, lw, cos_t, sin_t, tm)
        oax, qbx, kbx, vbx, qcx, kcx, vcx = _mixin(xc, sh1x, s1x, lw, cos_x, sin_x, nc)

        table = _na_bias_table(b_rpb[i], rows)
        ob = _na_attention(qb, kb, vb, kbx, vbx, table)
        oc = _mla_attention(qc, kc, vc, kcx, vcx, tq)
        n_lat = bn * l
        n_tok = n_lat + (bn * nc if need_ctx else 0)
        x, h2t, ei, wf, cnt = _mixout(x, oa, ob, oc, g1, sh2, s2, lw, tm, n_tok)
        if need_ctx:
            obx = _flash(qbx, kbx[:, None], vbx,
                         pl.BlockSpec((1, 1, nc, LANE), lambda b, h, ii, j: (b, 0, j, h // 2)),
                         tq=nc, tk=nc, name="ctx_dense_attention")
            ocx = _flash(qcx, kcx, vcx.reshape(bn, C_WIDTH, nc),
                         pl.BlockSpec((1, 1, nc, QPAD), lambda b, h, ii, j: (b, h, j, 0)),
                         tq=nc, tk=nc, name="ctx_mla_attention")
            xc, h2t, ei, wf, cnt = _mixout(xc, oax, obx, ocx, g1x, sh2x, s2x, lw, nc, n_tok, row0=n_lat,
                                           bufs=(h2t, ei, wf), cnt_in=cnt)

        counts = cnt[:, 0].astype(jnp.int32)
        pstarts, block_e, block_i, used = _expert_blocks(counts, nb, MOE_BLOCK)
        xs = _dispatch(pstarts, ei, h2t, xs, tm)
        ys = _experts(xs, block_e, block_i, used, moe_w1, moe_w3, moe_w2, i, MOE_BLOCK)
        x = _combine(x, pstarts, ei, wf, ys, g2, 0, tm)
        if need_ctx:
            xc = _combine(xc, pstarts, ei, wf, ys, g2x, n_lat, nc)
    return x
```
